```python
import math
import jax, jax.numpy as jnp
from jax import lax
import numpy as np

D_MODEL = 2048
BATCH = 1
SEQ = 8192
DEPTH = 4

CHUNK = 64
Q_BLOCK = 128
M_HEADS = 4
M_QK_DIM = 256
M_V_DIM = 512
M_QK = M_HEADS * M_QK_DIM
M_V = M_HEADS * M_V_DIM
M_CONV = 4
FGATE_BIAS_LO = 3.0
FGATE_BIAS_HI = 6.0
B_HEADS = 16
B_HEAD_DIM = 64
B_V_DIM = 2 * B_HEAD_DIM
B_QK = B_HEADS * 2 * B_HEAD_DIM
B_V = B_HEADS * B_V_DIM
ROPE_THETA = 500000.0
ROPE_DIM = B_HEAD_DIM // 4
D_FF = 5632
FFN_CONV = 3
IN_SIZES = (M_QK, M_QK, M_V, M_V, M_HEADS, M_HEADS, B_QK, B_QK, B_V, D_MODEL, D_MODEL)
N_IN = 2 * M_QK + 2 * M_V + 2 * M_HEADS + 2 * B_QK + B_V + 2 * D_MODEL
EPS = 1e-6

kernel_name = "hybrid_mlstm_diffattn_convffn_trunk"


def rmsnorm(x, g):
    xf = x.astype(jnp.float32)
    y = xf * lax.rsqrt(jnp.mean(xf * xf, axis=-1, keepdims=True) + EPS)
    return (y * g.astype(jnp.float32)).astype(x.dtype)


def causal_dwconv(x, w, b):
    K = w.shape[0]
    S = x.shape[1]
    xp = jnp.pad(x, ((0, 0), (K - 1, 0), (0, 0)))
    out = b
    for j in range(K):
        out = out + w[j] * xp[:, j:j + S]
    return out


def rope_tables(positions, dtype):
    half = ROPE_DIM // 2
    inv = jnp.power(ROPE_THETA, -(jnp.arange(half, dtype=jnp.float32) * 2.0 / ROPE_DIM))
    ang = positions.astype(jnp.float32)[..., None] * inv
    cos = jnp.cos(ang)[:, :, None, None, :].astype(dtype)
    sin = jnp.sin(ang)[:, :, None, None, :].astype(dtype)
    return cos, sin


def rope_partial(x, cos, sin):
    half = ROPE_DIM // 2
    x1 = x[..., :half]
    x2 = x[..., half:ROPE_DIM]
    rot = jnp.concatenate([x1 * cos - x2 * sin, x2 * cos + x1 * sin], axis=-1)
    return jnp.concatenate([rot, x[..., ROPE_DIM:]], axis=-1)


def mlstm(q, k, v, i_pre, f_pre):
    Bn, S, H, dqk = q.shape
    dv = v.shape[-1]
    NC = S // CHUNK
    q = q * (dqk ** -0.5)
    logf = jax.nn.log_sigmoid(f_pre)

    def chunks4(a):
        return a.reshape(Bn, NC, CHUNK, H, a.shape[-1]).transpose(1, 0, 3, 2, 4)

    def chunks3(a):
        return a.reshape(Bn, NC, CHUNK, H).transpose(1, 0, 3, 2)

    tril = jnp.tril(jnp.ones((CHUNK, CHUNK), dtype=bool))

    def step(carry, xs):
        C, n, m = carry
        qc, kc, vc, ic, lfc = xs
        b = jnp.cumsum(lfc, axis=-1)
        dmat = b[..., :, None] - b[..., None, :] + ic[..., None, :]
        dmat = jnp.where(tril, dmat, -jnp.inf)
        inter_log = b + m[..., None]
        m_t = jnp.maximum(inter_log, jnp.max(dmat, axis=-1))
        w_intra = jnp.exp(dmat - m_t[..., None])
        w_inter = jnp.exp(inter_log - m_t)
        s = jnp.einsum('bhtd,bhsd->bhts', qc, kc) * w_intra
        num = (w_inter[..., None] * jnp.einsum('bhvd,bhtd->bhtv', C, qc)
               + jnp.einsum('bhts,bhsv->bhtv', s, vc))
        den = w_inter * jnp.einsum('bhd,bhtd->bht', n, qc) + jnp.sum(s, axis=-1)
        h = num / jnp.maximum(jnp.abs(den), jnp.exp(-m_t))[..., None]
        bL = b[..., -1]
        log_s = bL[..., None] - b + ic
        m_new = jnp.maximum(bL + m, jnp.max(log_s, axis=-1))
        w_s = jnp.exp(log_s - m_new[..., None])
        decay = jnp.exp(bL + m - m_new)
        C_new = decay[..., None, None] * C + jnp.einsum('bhs,bhsv,bhsd->bhvd', w_s, vc, kc)
        n_new = decay[..., None] * n + jnp.einsum('bhs,bhsd->bhd', w_s, kc)
        return (C_new, n_new, m_new), h

    init = (jnp.zeros((Bn, H, dv, dqk), jnp.float32),
            jnp.zeros((Bn, H, dqk), jnp.float32),
            jnp.zeros((Bn, H), jnp.float32))
    xs = (chunks4(q), chunks4(k), chunks4(v), chunks3(i_pre), chunks3(logf))
    _, hs = lax.scan(step, init, xs)
    return hs.transpose(1, 0, 3, 2, 4).reshape(Bn, S, H, dv)


def diff_attention(q, k, v, lam):
    Bn, S, H, _, d = q.shape
    dv = v.shape[-1]
    NB = S // Q_BLOCK
    scale = d ** -0.5
    kh = k.transpose(0, 2, 3, 1, 4)
    vh = v.transpose(0, 2, 1, 3)
    qb = q.transpose(0, 2, 3, 1, 4).reshape(Bn, H, 2, NB, Q_BLOCK, d).transpose(3, 0, 1, 2, 4, 5)
    starts = jnp.arange(NB, dtype=jnp.int32) * Q_BLOCK
    key_chunk = jnp.arange(S, dtype=jnp.int32) // CHUNK

    def block(args):
        qblk, start = args
        s = jnp.einsum('bhcqd,bhckd->bhcqk', qblk, kh).astype(jnp.float32) * scale
        q_chunk = (start + jnp.arange(Q_BLOCK, dtype=jnp.int32)) // CHUNK
        mask = key_chunk[None, :] <= q_chunk[:, None]
        p = jax.nn.softmax(jnp.where(mask, s, -jnp.inf), axis=-1)
        a = p[:, :, 0] - lam * p[:, :, 1]
        return jnp.einsum('bhqk,bhkd->bhqd', a.astype(vh.dtype), vh)

    out = lax.map(block, (qb, starts))
    return out.transpose(1, 0, 3, 2, 4).reshape(Bn, S, H, dv)


def token_mixer(h, w_in, conv_qk_w, conv_qk_b, b_igate, b_fgate, g_mlstm, lambdas, g_diff,
                w_a, w_b, w_out, cos, sin, lam_init):
    Bn, S, _ = h.shape
    f32 = jnp.float32
    offsets = [int(o) for o in np.cumsum(IN_SIZES)[:-1]]
    proj = h @ w_in
    q_m, k_m, v_m, o_m, i_m, f_m, q_b, k_b, v_b, g_a, g_b = jnp.split(proj, offsets, axis=-1)

    qk_m = jax.nn.silu(causal_dwconv(jnp.concatenate([q_m, k_m], axis=-1), conv_qk_w, conv_qk_b))
    q_m, k_m = qk_m[..., :M_QK], qk_m[..., M_QK:]
    h_m = mlstm(q_m.reshape(Bn, S, M_HEADS, M_QK_DIM).astype(f32),
                k_m.reshape(Bn, S, M_HEADS, M_QK_DIM).astype(f32),
                v_m.reshape(Bn, S, M_HEADS, M_V_DIM).astype(f32),
                (i_m + b_igate).astype(f32),
                (f_m + b_fgate).astype(f32))
    h_m = rmsnorm(h_m, g_mlstm.reshape(M_HEADS, M_V_DIM)).astype(h.dtype).reshape(Bn, S, M_V)
    y_a = (jax.nn.sigmoid(o_m) * h_m) @ w_a

    qd = rope_partial(q_b.reshape(Bn, S, B_HEADS, 2, B_HEAD_DIM), cos, sin)
    kd = rope_partial(k_b.reshape(Bn, S, B_HEADS, 2, B_HEAD_DIM), cos, sin)
    vd = v_b.reshape(Bn, S, B_HEADS, B_V_DIM)
    lf = lambdas.astype(f32)
    lam = jnp.exp(jnp.sum(lf[0] * lf[1])) - jnp.exp(jnp.sum(lf[2] * lf[3])) + lam_init
    o_b = diff_attention(qd, kd, vd, lam)
    o_b = (rmsnorm(o_b, g_diff.reshape(B_HEADS, B_V_DIM)) * (1.0 - lam_init)).reshape(Bn, S, B_V)
    y_b = o_b @ w_b

    merged = jax.nn.sigmoid(g_a) * y_a + jax.nn.sigmoid(g_b) * y_b
    return merged @ w_out


def channel_mixer(h, w_gate, w_up, conv_w, conv_b, w_down):
    g = causal_dwconv(h @ w_gate, conv_w, conv_b)
    return (jax.nn.silu(g) * (h @ w_up)) @ w_down


def setup_inputs(seed: int = 0) -> dict:
    key = jax.random.key(seed)
    ks = jax.random.split(key, 24)
    nrm = jax.random.normal
    f32 = jnp.float32
    D = D_MODEL

    def gain(k, n):
        return 1.0 + 0.1 * nrm(k, (DEPTH, n), f32)

    x = nrm(ks[0], (BATCH, SEQ, D), f32)
    c = nrm(ks[1], (BATCH, D), f32)
    positions = jnp.broadcast_to(jnp.arange(SEQ, dtype=jnp.int32), (BATCH, SEQ))
    w_ada = nrm(ks[2], (DEPTH, D, 6 * D), f32) * (0.5 * D ** -0.5)
    b_ada = 0.01 * nrm(ks[3], (DEPTH, 6 * D), f32)
    g_pre_mix = gain(ks[4], D)
    g_post_mix = gain(ks[5], D)
    g_pre_ffn = gain(ks[6], D)
    g_post_ffn = gain(ks[7], D)
    w_in = nrm(ks[8], (DEPTH, D, N_IN), f32) * D ** -0.5
    conv_qk_w = nrm(ks[9], (DEPTH, M_CONV, 2 * M_QK), f32) * M_CONV ** -0.5
    conv_qk_b = 0.01 * nrm(ks[10], (DEPTH, 2 * M_QK), f32)
    b_igate = 0.1 * nrm(ks[11], (DEPTH, M_HEADS), f32)
    b_fgate = (jnp.linspace(FGATE_BIAS_LO, FGATE_BIAS_HI, M_HEADS, dtype=f32)[None, :]
               + 0.1 * nrm(ks[12], (DEPTH, M_HEADS), f32))
    g_mlstm = gain(ks[13], M_V)
    lambdas = 0.1 * nrm(ks[14], (DEPTH, 4, B_HEAD_DIM), f32)
    g_diff = gain(ks[15], B_V)
    w_a = nrm(ks[16], (DEPTH, M_V, D), f32) * M_V ** -0.5
    w_b = nrm(ks[17], (DEPTH, B_V, D), f32) * B_V ** -0.5
    w_out = nrm(ks[18], (DEPTH, D, D), f32) * D ** -0.5
    w_gate = nrm(ks[19], (DEPTH, D, D_FF), f32) * D ** -0.5
    w_up = nrm(ks[20], (DEPTH, D, D_FF), f32) * D ** -0.5
    conv_ffn_w = nrm(ks[21], (DEPTH, FFN_CONV, D_FF), f32) * FFN_CONV ** -0.5
    conv_ffn_b = 0.01 * nrm(ks[22], (DEPTH, D_FF), f32)
    w_down = nrm(ks[23], (DEPTH, D_FF, D), f32) * D_FF ** -0.5
    return {"x": x, "c": c, "positions": positions, "w_ada": w_ada, "b_ada": b_ada,
            "g_pre_mix": g_pre_mix, "g_post_mix": g_post_mix, "g_pre_ffn": g_pre_ffn,
            "g_post_ffn": g_post_ffn, "w_in": w_in, "conv_qk_w": conv_qk_w, "conv_qk_b": conv_qk_b,
            "b_igate": b_igate, "b_fgate": b_fgate, "g_mlstm": g_mlstm, "lambdas": lambdas,
            "g_diff": g_diff, "w_a": w_a, "w_b": w_b, "w_out": w_out, "w_gate": w_gate,
            "w_up": w_up, "conv_ffn_w": conv_ffn_w, "conv_ffn_b": conv_ffn_b, "w_down": w_down}


def reference(x, c, positions, w_ada, b_ada, g_pre_mix, g_post_mix, g_pre_ffn, g_post_ffn,
              w_in, conv_qk_w, conv_qk_b, b_igate, b_fgate, g_mlstm, lambdas, g_diff,
              w_a, w_b, w_out, w_gate, w_up, conv_ffn_w, conv_ffn_b, w_down):
    cos, sin = rope_tables(positions, x.dtype)
    c_act = jax.nn.silu(c)
    for l in range(DEPTH):
        lam_init = 0.8 - 0.6 * math.exp(-0.3 * l)
        mod = (c_act @ w_ada[l] + b_ada[l])[:, None, :]
        sh1, sc1, gt1, sh2, sc2, gt2 = jnp.split(mod, 6, axis=-1)
        h = rmsnorm(x, g_pre_mix[l]) * (1.0 + sc1) + sh1
        y = token_mixer(h, w_in[l], conv_qk_w[l], conv_qk_b[l], b_igate[l], b_fgate[l],
                        g_mlstm[l], lambdas[l], g_diff[l], w_a[l], w_b[l], w_out[l],
                        cos, sin, lam_init)
        x = x + gt1 * rmsnorm(y, g_post_mix[l])
        h = rmsnorm(x, g_pre_ffn[l]) * (1.0 + sc2) + sh2
        y = channel_mixer(h, w_gate[l], w_up[l], conv_ffn_w[l], conv_ffn_b[l], w_down[l])
        x = x + gt2 * rmsnorm(y, g_post_ffn[l])
    return x
```

```python
import functools
import math

import jax
import jax.numpy as jnp
from jax import lax
from jax.experimental import pallas as pl
from jax.experimental.pallas import tpu as pltpu

F32 = jnp.float32
BF16 = jnp.bfloat16

M_HEADS = 4
M_QK_DIM = 256
M_V_DIM = 512
M_QK = M_HEADS * M_QK_DIM
M_V = M_HEADS * M_V_DIM
B_HEADS = 16
B_HEAD_DIM = 64
B_V_DIM = 2 * B_HEAD_DIM
B_QK = B_HEADS * 2 * B_HEAD_DIM
B_V = B_HEADS * B_V_DIM
CHUNK = 64
ROPE_THETA = 500000.0
ROPE_DIM = B_HEAD_DIM // 4
EPS = 1e-6

LANES = 128
SUBLANES = 8
V7X_VMEM_REQUEST_CAP = 60000 * 1024
V7X_COMPILER_SCRATCH = 8 * 1024 * 1024

MLSTM_L = 256
ATT_TQ = 512
ATT_TK = 256
NEG = -1e30


def _nbytes(shape, dtype):
    return math.prod(shape) * jnp.dtype(dtype).itemsize


def _vmem_limit(*terms):
    total = sum(_nbytes(s, d) * n for s, d, n in terms) + V7X_COMPILER_SCRATCH
    return int(min(V7X_VMEM_REQUEST_CAP, total))


def _params(sem, limit):
    return pltpu.CompilerParams(dimension_semantics=sem, vmem_limit_bytes=limit)


def _silu(x):
    return x * jax.nn.sigmoid(x)


def _dot(a, b):
    return jnp.dot(a, b, preferred_element_type=F32)


def _dot_nt(a, b):
    return lax.dot_general(a, b, (((1,), (1,)), ((), ())), preferred_element_type=F32)


def _dot_tn(a, b):
    return lax.dot_general(a, b, (((0,), (0,)), ((), ())), preferred_element_type=F32)


def _adaln_kernel(c_ref, w_ref, b_ref, o_ref):
    c = c_ref[...]
    o_ref[...] = jnp.sum(_silu(c) * w_ref[...], axis=0, keepdims=True) + b_ref[...]


def _adaln(c_col, w_ada, b_ada):
    depth, d, n = w_ada.shape
    tn = 1024
    return pl.pallas_call(
        _adaln_kernel,
        out_shape=jax.ShapeDtypeStruct((depth, 1, n), F32),
        grid=(depth, n // tn),
        in_specs=[pl.BlockSpec((d, 1), lambda l, j: (0, 0)),
                  pl.BlockSpec((None, d, tn), lambda l, j: (l, 0, j)),
                  pl.BlockSpec((None, 1, tn), lambda l, j: (l, 0, j))],
        out_specs=pl.BlockSpec((None, 1, tn), lambda l, j: (l, 0, j)),
        compiler_params=_params(("parallel", "parallel"),
                                _vmem_limit(((d, LANES), F32, 2), ((d, tn), F32, 3))),
        name="adaln",
    )(c_col, w_ada, b_ada.reshape(depth, 1, n))


def _prenorm_kernel(x_ref, g_ref, sc_ref, sh_ref, o_ref):
    x = x_ref[...]
    ms = jnp.mean(x * x, axis=-1, keepdims=True)
    y = x * lax.rsqrt(ms + EPS) * g_ref[...]
    o_ref[...] = (y * (1.0 + sc_ref[...]) + sh_ref[...]).astype(o_ref.dtype)


def _prenorm(x, gains, mod, l, sc_blk, sh_blk):
    s, d = x.shape
    tm = min(512, s)
    return pl.pallas_call(
        _prenorm_kernel,
        out_shape=jax.ShapeDtypeStruct((s, d), BF16),
        grid=(s // tm,),
        in_specs=[pl.BlockSpec((tm, d), lambda i: (i, 0)),
                  pl.BlockSpec((None, 1, d), lambda i: (l, 0, 0)),
                  pl.BlockSpec((None, 1, d), lambda i: (l, 0, sc_blk)),
                  pl.BlockSpec((None, 1, d), lambda i: (l, 0, sh_blk))],
        out_specs=pl.BlockSpec((tm, d), lambda i: (i, 0)),
        compiler_params=_params(("parallel",), _vmem_limit(((tm, d), F32, 4), ((tm, d), BF16, 2))),
        name="prenorm",
    )(x, gains, mod, mod)


def _causal_conv(acc, buf_ref, cw_ref, cb_ref, first, tm):
    kconv = cw_ref.shape[0]
    tn = acc.shape[1]

    @pl.when(first)
    def _():
        buf_ref[0:SUBLANES, :] = jnp.zeros((SUBLANES, tn), F32)

    buf_ref[SUBLANES:SUBLANES + tm, :] = acc
    out = cb_ref[...] + cw_ref[kconv - 1:kconv, :] * acc
    for j in range(kconv - 1):
        off = SUBLANES - (kconv - 1) + j
        out = out + cw_ref[j:j + 1, :] * buf_ref[off:off + tm, :]
    buf_ref[0:SUBLANES, :] = buf_ref[tm:tm + SUBLANES, :]
    return out


def _proj_conv_kernel(h_ref, w_ref, cw_ref, cb_ref, sc_ref, o_ref, buf_ref, *, tm):
    acc = _dot(h_ref[...], w_ref[...])
    out = _causal_conv(acc, buf_ref, cw_ref, cb_ref, pl.program_id(1) == 0, tm)
    o_ref[...] = (_silu(out) * sc_ref[...]).astype(o_ref.dtype)


def _proj_plain_kernel(h_ref, w_ref, o_ref):
    o_ref[...] = _dot(h_ref[...], w_ref[...]).astype(o_ref.dtype)


def _proj_sigmoid_kernel(h_ref, w_ref, o_ref):
    o_ref[...] = jax.nn.sigmoid(_dot(h_ref[...], w_ref[...])).astype(o_ref.dtype)


def _proj_rope_kernel(h_ref, w_ref, ca_ref, cb_ref, cc_ref, o_ref, *, n_qtiles, qscale):
    acc = _dot(h_ref[...], w_ref[...])
    scale = jnp.where(pl.program_id(0) < n_qtiles, qscale, 1.0).astype(F32)
    ca = ca_ref[...] * scale
    cb = cb_ref[...] * scale
    cc = cc_ref[...] * scale
    for s in range(acc.shape[1] // LANES):
        xs = acc[:, s * LANES:(s + 1) * LANES]
        r = xs * ca + pltpu.roll(xs, LANES - ROPE_DIM // 2, 1) * cb + pltpu.roll(xs, ROPE_DIM // 2, 1) * cc
        o_ref[:, s * LANES:(s + 1) * LANES] = r.astype(o_ref.dtype)


def _proj(kernel, h, w, out_dtype, *, tn, extra=(), extra_specs=(), scratch=(), sequential=False,
          extra_vmem=(), name):
    s, d = h.shape
    n = w.shape[1]
    tm = min(1024, s)
    sem = ("arbitrary", "arbitrary") if sequential else ("parallel", "parallel")
    return pl.pallas_call(
        kernel,
        out_shape=jax.ShapeDtypeStruct((s, n), out_dtype),
        grid=(n // tn, s // tm),
        in_specs=[pl.BlockSpec((tm, d), lambda j, i: (i, 0)),
                  pl.BlockSpec((d, tn), lambda j, i: (0, j))] + list(extra_specs),
        out_specs=pl.BlockSpec((tm, tn), lambda j, i: (i, j)),
        scratch_shapes=list(scratch),
        compiler_params=_params(sem, _vmem_limit(((tm, d), BF16, 2), ((d, tn), BF16, 2),
                                                 ((tm, tn), out_dtype, 2), ((tm, tn), F32, 3),
                                                 *extra_vmem)),
        name=name,
    )(h, w, *extra)


def _mlstm_kernel(qk_ref, v_ref, gate_ref, bias_ref, so_ref, g_ref, o_ref, st_ref, m_ref):
    L = MLSTM_L

    @pl.when(pl.program_id(0) == 0)
    def _():
        st_ref[...] = jnp.zeros(st_ref.shape, F32)
        m_ref[...] = jnp.zeros(m_ref.shape, F32)

    gates = gate_ref[...] + bias_ref[...]
    ipre = gates[:, :LANES]
    fpre = gates[:, LANES:]
    logf = jnp.minimum(fpre, 0.0) - jnp.log1p(jnp.exp(-jnp.abs(fpre)))

    row = lax.broadcasted_iota(jnp.int32, (L, L), 0)
    col = lax.broadcasted_iota(jnp.int32, (L, L), 1)
    tri = col <= row
    tri_b = jnp.where(tri, 1.0, 0.0).astype(BF16)
    hi = logf.astype(BF16)
    r1 = logf - hi.astype(F32)
    mid = r1.astype(BF16)
    lo = (r1 - mid.astype(F32)).astype(BF16)
    b = _dot(tri_b, hi) + _dot(tri_b, mid) + _dot(tri_b, lo)
    a = ipre - b
    a_t = a.T
    m_all = m_ref[...]
    lane = lax.broadcasted_iota(jnp.int32, (1, LANES), 1)
    ones_col = jnp.where(lax.broadcasted_iota(jnp.int32, (L, LANES), 1) == 0, 1.0, 0.0).astype(BF16)
    m_next = m_all

    for h in range(M_HEADS):
        a_row = a_t[h:h + 1, :]
        a_col = a[:, h:h + 1]
        b_col = b[:, h:h + 1]
        m_prev = m_all[:, h:h + 1]
        amask = jnp.where(tri, a_row, NEG)
        m_run = jnp.maximum(jnp.max(amask, axis=1, keepdims=True), m_prev)
        dmat = jnp.exp(amask - m_run)
        w_inter = jnp.exp(m_prev - m_run)
        q = qk_ref[:, h * M_QK_DIM:(h + 1) * M_QK_DIM]
        k = qk_ref[:, M_QK + h * M_QK_DIM:M_QK + (h + 1) * M_QK_DIM]
        v_ext = jnp.concatenate([v_ref[:, h * M_V_DIM:(h + 1) * M_V_DIM], ones_col], axis=1)
        sd = (_dot_nt(q, k) * dmat).astype(BF16)
        st = st_ref[h]
        tot = w_inter * _dot(q, st.astype(BF16)) + _dot(sd, v_ext)
        num = tot[:, :M_V_DIM]
        den = tot[:, M_V_DIM:M_V_DIM + 1]
        hval = num / jnp.maximum(jnp.abs(den), jnp.exp(-(b_col + m_run)))
        m_last = m_run[L - 1:L, :]
        w_s = jnp.exp(a_col - m_last)
        decay = jnp.exp(m_prev - m_last)
        rhs = (w_s * v_ext.astype(F32)).astype(BF16)
        st_ref[h] = decay * st + _dot_tn(k, rhs)
        m_next = jnp.where(lane == h, b_col[L - 1:L, :] + m_last, m_next)
        ms = jnp.mean(hval * hval, axis=-1, keepdims=True)
        sl = slice(h * M_V_DIM, (h + 1) * M_V_DIM)
        hn = hval * lax.rsqrt(ms + EPS) * g_ref[:, sl]
        o_ref[:, sl] = (so_ref[:, sl].astype(F32) * hn).astype(o_ref.dtype)

    m_ref[...] = m_next


def _mlstm(qk, plain, gates, gate_bias, sig, g_mlstm, l):
    s = qk.shape[0]
    L = MLSTM_L
    st_shape = (M_HEADS, M_QK_DIM, M_V_DIM + LANES)
    return pl.pallas_call(
        _mlstm_kernel,
        out_shape=jax.ShapeDtypeStruct((s, M_V), BF16),
        grid=(s // L,),
        in_specs=[pl.BlockSpec((L, 2 * M_QK), lambda c: (c, 0)),
                  pl.BlockSpec((L, M_V), lambda c: (c, 0)),
                  pl.BlockSpec((L, 2 * LANES), lambda c: (c, 0)),
                  pl.BlockSpec((None, 1, 2 * LANES), lambda c: (l, 0, 0)),
                  pl.BlockSpec((L, M_V), lambda c: (c, 0)),
                  pl.BlockSpec((None, 1, M_V), lambda c: (l, 0, 0))],
        out_specs=pl.BlockSpec((L, M_V), lambda c: (c, 0)),
        scratch_shapes=[pltpu.VMEM(st_shape, F32), pltpu.VMEM((1, LANES), F32)],
        compiler_params=_params(("arbitrary",), _vmem_limit(
            ((L, 2 * M_QK), BF16, 2), ((L, M_V), BF16, 6), ((L, 2 * LANES), F32, 2),
            (st_shape, F32, 2), ((L, M_V + LANES), F32, 8), ((L, L), F32, 8))),
        name="mlstm",
    )(qk, plain, gates, gate_bias, sig, g_mlstm)


def _attn_kernel(q_ref, k_ref, v_ref, lam_ref, g_ref, o_ref, vt_ref, acc_ref, m_ref, l_ref, *,
                 lam_init, seq):
    tq, tk = ATT_TQ, ATT_TK
    qi = pl.program_id(1)

    @pl.when(qi == 0)
    def _():
        for kb in range(seq // tk):
            vt_ref[kb] = v_ref[kb * tk:(kb + 1) * tk, :].astype(F32).T.astype(BF16)

    q = q_ref[...]
    qlane = lax.broadcasted_iota(jnp.int32, q.shape, 1)
    zero = jnp.zeros_like(q)
    qcat = jnp.concatenate([jnp.where(qlane < B_HEAD_DIM, q, zero),
                            jnp.where(qlane >= B_HEAD_DIM, q, zero)], axis=0)

    acc_ref[...] = jnp.zeros(acc_ref.shape, F32)
    m_ref[...] = jnp.full(m_ref.shape, NEG, F32)
    l_ref[...] = jnp.zeros(l_ref.shape, F32)

    def step(kb, mask):
        kblk = k_ref[pl.ds(pl.multiple_of(kb * tk, tk), tk), :]
        s_t = _dot_nt(kblk, qcat)
        if mask is not None:
            s_t = jnp.where(mask, s_t, NEG)
        m_old = m_ref[...]
        m_new = jnp.maximum(m_old, jnp.max(s_t, axis=0, keepdims=True))
        alpha = jnp.exp(m_old - m_new)
        p = jnp.exp(s_t - m_new)
        l_ref[...] = alpha * l_ref[...] + jnp.sum(p, axis=0, keepdims=True)
        acc_ref[...] = alpha * acc_ref[...] + _dot(vt_ref[kb], p.astype(BF16))
        m_ref[...] = m_new

    per_q = tq // tk

    def body(it, carry):
        for u in range(per_q):
            step(it * per_q + u, None)
        return carry

    lax.fori_loop(0, qi, body, 0)

    shift = CHUNK.bit_length() - 1
    kchunk = lax.broadcasted_iota(jnp.int32, (tk, 2 * tq), 0) >> shift
    qchunk = (lax.broadcasted_iota(jnp.int32, (tk, 2 * tq), 1) & (tq - 1)) >> shift
    for u in range(per_q):
        step(qi * per_q + u, kchunk + (u * tk >> shift) <= qchunk)

    lf = lam_ref[...]
    lam = (jnp.exp(jnp.sum(lf[0:1] * lf[1:2], axis=1, keepdims=True))
           - jnp.exp(jnp.sum(lf[2:3] * lf[3:4], axis=1, keepdims=True)) + lam_init)
    acc = acc_ref[...]
    l_all = l_ref[...]
    o_t = acc[:, :tq] / l_all[:, :tq] - lam * (acc[:, tq:] / l_all[:, tq:])
    o = o_t.T
    ms = jnp.mean(o * o, axis=-1, keepdims=True)
    o_ref[...] = (o * lax.rsqrt(ms + EPS) * (g_ref[...] * (1.0 - lam_init))).astype(o_ref.dtype)


def _attention(qkb, plain, lambdas, g_diff, l, lam_init):
    s = qkb.shape[0]
    tq, tk = ATT_TQ, ATT_TK
    kern = functools.partial(_attn_kernel, lam_init=lam_init, seq=s)
    return pl.pallas_call(
        kern,
        out_shape=jax.ShapeDtypeStruct((s, B_V), BF16),
        grid=(B_HEADS, s // tq),
        in_specs=[pl.BlockSpec((tq, LANES), lambda h, i: (i, h)),
                  pl.BlockSpec((s, LANES), lambda h, i: (0, B_HEADS + h)),
                  pl.BlockSpec((s, LANES), lambda h, i: (0, B_HEADS + h)),
                  pl.BlockSpec((None, 4, B_HEAD_DIM), lambda h, i: (l, 0, 0)),
                  pl.BlockSpec((None, 1, LANES), lambda h, i: (l, 0, h))],
        out_specs=pl.BlockSpec((tq, LANES), lambda h, i: (i, h)),
        scratch_shapes=[pltpu.VMEM((s // tk, LANES, tk), BF16),
                        pltpu.VMEM((LANES, 2 * tq), F32),
                        pltpu.VMEM((1, 2 * tq), F32),
                        pltpu.VMEM((1, 2 * tq), F32)],
        compiler_params=_params(("arbitrary", "arbitrary"), _vmem_limit(
            ((s, LANES), BF16, 5), ((tq, LANES), BF16, 6), ((tk, 2 * tq), F32, 6),
            ((LANES, 2 * tq), F32, 4))),
        name="diff_attn",
    )(qkb, qkb, plain, lambdas, g_diff)


def _merge_kernel(a_ref, b_ref, wa_ref, wb_ref, ga_ref, gb_ref, o_ref):
    ya = _dot(a_ref[...], wa_ref[...])
    yb = _dot(b_ref[...], wb_ref[...])
    o_ref[...] = (ga_ref[...].astype(F32) * ya + gb_ref[...].astype(F32) * yb).astype(o_ref.dtype)


def _merge(ya_in, yb_in, w_a, w_b, sig):
    s, d = ya_in.shape
    n = w_a.shape[1]
    tm, tn = min(512, s), 1024
    ga0 = M_V // tn
    gb0 = ga0 + n // tn
    return pl.pallas_call(
        _merge_kernel,
        out_shape=jax.ShapeDtypeStruct((s, n), BF16),
        grid=(n // tn, s // tm),
        in_specs=[pl.BlockSpec((tm, d), lambda j, i: (i, 0)),
                  pl.BlockSpec((tm, d), lambda j, i: (i, 0)),
                  pl.BlockSpec((d, tn), lambda j, i: (0, j)),
                  pl.BlockSpec((d, tn), lambda j, i: (0, j)),
                  pl.BlockSpec((tm, tn), lambda j, i: (i, ga0 + j)),
                  pl.BlockSpec((tm, tn), lambda j, i: (i, gb0 + j))],
        out_specs=pl.BlockSpec((tm, tn), lambda j, i: (i, j)),
        compiler_params=_params(("parallel", "parallel"), _vmem_limit(
            ((tm, d), BF16, 4), ((d, tn), BF16, 4), ((tm, tn), BF16, 6), ((tm, tn), F32, 4))),
        name="merge",
    )(ya_in, yb_in, w_a, w_b, sig, sig)


def _proj_res_kernel(a_ref, w_ref, x_ref, g_ref, gt_ref, o_ref):
    y = _dot(a_ref[...], w_ref[...])
    ms = jnp.mean(y * y, axis=-1, keepdims=True)
    o_ref[...] = x_ref[...] + gt_ref[...] * (y * lax.rsqrt(ms + EPS) * g_ref[...])


def _proj_res(a, w, x, gains, mod, l, gt_blk, tm, name):
    s, k = a.shape
    d = w.shape[1]
    tm = min(tm, s)
    return pl.pallas_call(
        _proj_res_kernel,
        out_shape=jax.ShapeDtypeStruct((s, d), F32),
        grid=(s // tm,),
        in_specs=[pl.BlockSpec((tm, k), lambda i: (i, 0)),
                  pl.BlockSpec((k, d), lambda i: (0, 0), pipeline_mode=pl.Buffered(1)),
                  pl.BlockSpec((tm, d), lambda i: (i, 0)),
                  pl.BlockSpec((None, 1, d), lambda i: (l, 0, 0)),
                  pl.BlockSpec((None, 1, d), lambda i: (l, 0, gt_blk))],
        out_specs=pl.BlockSpec((tm, d), lambda i: (i, 0)),
        compiler_params=_params(("parallel",), _vmem_limit(
            ((tm, k), BF16, 2), ((k, d), BF16, 1), ((tm, d), F32, 7))),
        name=name,
    )(a, w, x, gains, mod)


def _ffn_up_kernel(h_ref, wg_ref, wu_ref, cw_ref, cb_ref, o_ref, buf_ref, *, tm):
    h = h_ref[...]
    g = _causal_conv(_dot(h, wg_ref[...]), buf_ref, cw_ref, cb_ref, pl.program_id(1) == 0, tm)
    o_ref[...] = (_silu(g) * _dot(h, wu_ref[...])).astype(o_ref.dtype)


def _ffn_up(h, w_gate, w_up, conv_w, conv_b, l):
    s, d = h.shape
    n = w_gate.shape[1]
    tm, tn = min(1024, s), 512
    kconv = conv_w.shape[1]
    kern = functools.partial(_ffn_up_kernel, tm=tm)
    return pl.pallas_call(
        kern,
        out_shape=jax.ShapeDtypeStruct((s, n), BF16),
        grid=(n // tn, s // tm),
        in_specs=[pl.BlockSpec((tm, d), lambda j, i: (i, 0)),
                  pl.BlockSpec((d, tn), lambda j, i: (0, j)),
                  pl.BlockSpec((d, tn), lambda j, i: (0, j)),
                  pl.BlockSpec((None, kconv, tn), lambda j, i: (l, 0, j)),
                  pl.BlockSpec((None, 1, tn), lambda j, i: (l, 0, j))],
        out_specs=pl.BlockSpec((tm, tn), lambda j, i: (i, j)),
        scratch_shapes=[pltpu.VMEM((tm + SUBLANES, tn), F32)],
        compiler_params=_params(("arbitrary", "arbitrary"), _vmem_limit(
            ((tm, d), BF16, 2), ((d, tn), BF16, 4), ((tm, tn), BF16, 2), ((tm, tn), F32, 6))),
        name="ffn_up",
    )(h, w_gate, w_up, conv_w, conv_b)


def _rope_tables(positions):
    half = ROPE_DIM // 2
    inv = jnp.power(ROPE_THETA, -(jnp.arange(half, dtype=F32) * 2.0 / ROPE_DIM))
    ang = positions.astype(F32)[:, None] * inv
    cos, sin = jnp.cos(ang), jnp.sin(ang)
    s = positions.shape[0]
    rest = B_HEAD_DIM - ROPE_DIM
    reps = LANES // B_HEAD_DIM
    ca = jnp.tile(jnp.concatenate([cos, cos, jnp.ones((s, rest), F32)], axis=1), (1, reps))
    cb = jnp.tile(jnp.concatenate([-sin, jnp.zeros((s, B_HEAD_DIM - half), F32)], axis=1), (1, reps))
    cc = jnp.tile(jnp.concatenate([jnp.zeros((s, half), F32), sin, jnp.zeros((s, rest), F32)], axis=1),
                  (1, reps))
    return ca, cb, cc


def kernel(x, c, positions, w_ada, b_ada, g_pre_mix, g_post_mix, g_pre_ffn, g_post_ffn, w_in, conv_qk_w, conv_qk_b, b_igate, b_fgate, g_mlstm, lambdas, g_diff, w_a, w_b, w_out, w_gate, w_up, conv_ffn_w, conv_ffn_b, w_down):
    batch, s, d = x.shape
    depth = w_in.shape[0]
    assert batch == 1 and s % 1024 == 0 and d == M_V == B_V
    xs = x.reshape(s, d)
    tm_proj = min(1024, s)

    mod = _adaln(c.reshape(d, 1), w_ada, b_ada)
    ca, cb, cc = _rope_tables(positions[0])
    rope_specs = [pl.BlockSpec((tm_proj, LANES), lambda j, i: (i, 0))] * 3

    def gains(g):
        return g.reshape(depth, 1, g.shape[1])

    g_pre_mix, g_post_mix, g_pre_ffn, g_post_ffn = map(gains, (g_pre_mix, g_post_mix, g_pre_ffn, g_post_ffn))
    g_mlstm3, g_diff3 = gains(g_mlstm), gains(g_diff)
    conv_qk_b3, conv_ffn_b3 = gains(conv_qk_b), gains(conv_ffn_b)
    o_qm, o_km, o_vm, o_om = 0, M_QK, 2 * M_QK, 2 * M_QK + M_V
    o_i = o_om + M_V
    o_f = o_i + M_HEADS
    o_qb = o_f + M_HEADS
    o_kb, o_vb = o_qb + B_QK, o_qb + 2 * B_QK
    o_ga = o_vb + B_V
    o_gb = o_ga + d
    gate_pad = jnp.zeros((d, LANES - M_HEADS), F32)
    bias_pad = jnp.zeros((depth, LANES - M_HEADS), F32)
    gate_bias = jnp.concatenate([b_igate, bias_pad, b_fgate, bias_pad], axis=1).reshape(depth, 1, 2 * LANES)
    q_scale = jnp.concatenate([jnp.full((1, M_QK), M_QK_DIM ** -0.5, F32), jnp.ones((1, M_QK), F32)], axis=1)

    for l in range(depth):
        lam_init = 0.8 - 0.6 * math.exp(-0.3 * l)
        wl = w_in[l]
        w_conv = wl[:, o_qm:o_vm].astype(BF16)
        w_plain = jnp.concatenate([wl[:, o_vm:o_om], wl[:, o_vb:o_ga]], axis=1).astype(BF16)
        w_sig = jnp.concatenate([wl[:, o_om:o_i], wl[:, o_ga:o_gb + d]], axis=1).astype(BF16)
        w_rope = wl[:, o_qb:o_vb].astype(BF16)
        w_gates = jnp.concatenate([wl[:, o_i:o_f], gate_pad, wl[:, o_f:o_qb], gate_pad], axis=1).astype(BF16)

        h = _prenorm(xs, g_pre_mix, mod, l, 1, 0)
        tn = 1024
        qk = _proj(functools.partial(_proj_conv_kernel, tm=tm_proj), h, w_conv, BF16, tn=tn,
                   extra=(conv_qk_w, conv_qk_b3, q_scale),
                   extra_specs=[pl.BlockSpec((None, conv_qk_w.shape[1], tn), lambda j, i: (l, 0, j)),
                                pl.BlockSpec((None, 1, tn), lambda j, i: (l, 0, j)),
                                pl.BlockSpec((1, tn), lambda j, i: (0, j))],
                   scratch=[pltpu.VMEM((tm_proj + SUBLANES, tn), F32)], sequential=True,
                   extra_vmem=(((tm_proj, tn), F32, 3),), name="proj_conv")
        plain = _proj(_proj_plain_kernel, h, w_plain, BF16, tn=tn, name="proj_plain")
        sig = _proj(_proj_sigmoid_kernel, h, w_sig, BF16, tn=tn, name="proj_sigmoid")
        qkb = _proj(functools.partial(_proj_rope_kernel, n_qtiles=B_QK // tn, qscale=B_HEAD_DIM ** -0.5),
                    h, w_rope, BF16, tn=tn, extra=(ca, cb, cc), extra_specs=rope_specs,
                    extra_vmem=(((tm_proj, LANES), F32, 9),), name="proj_rope")
        gates = _proj(_proj_plain_kernel, h, w_gates, F32, tn=2 * LANES, name="proj_gates")

        y_a_in = _mlstm(qk, plain, gates, gate_bias, sig, g_mlstm3, l)
        y_b_in = _attention(qkb, plain, lambdas, g_diff3, l, lam_init)
        merged = _merge(y_a_in, y_b_in, w_a[l].astype(BF16), w_b[l].astype(BF16), sig)
        xs = _proj_res(merged, w_out[l].astype(BF16), xs, g_post_mix, mod, l, 2, 512, "out_proj")

        h = _prenorm(xs, g_pre_ffn, mod, l, 4, 3)
        act = _ffn_up(h, w_gate[l].astype(BF16), w_up[l].astype(BF16), conv_ffn_w, conv_ffn_b3, l)
        xs = _proj_res(act, w_down[l].astype(BF16), xs, g_post_ffn, mod, l, 5, 256, "ffn_down")

    return xs.reshape(batch, s, d)
```

```python
import functools
import math

import jax
import jax.numpy as jnp
from jax import lax
from jax.experimental import pallas as pl
from jax.experimental.pallas import tpu as pltpu

F32 = jnp.float32
BF16 = jnp.bfloat16

M_HEADS = 4
M_QK_DIM = 256
M_V_DIM = 512
M_QK = M_HEADS * M_QK_DIM
M_V = M_HEADS * M_V_DIM
B_HEADS = 16
B_HEAD_DIM = 64
B_V_DIM = 2 * B_HEAD_DIM
B_QK = B_HEADS * 2 * B_HEAD_DIM
B_V = B_HEADS * B_V_DIM
CHUNK = 64
ROPE_THETA = 500000.0
ROPE_DIM = B_HEAD_DIM // 4
EPS = 1e-6

LANES = 128
SUBLANES = 8
V7X_VMEM_REQUEST_CAP = 60000 * 1024
V7X_COMPILER_SCRATCH = 8 * 1024 * 1024

MLSTM_L = 256
ATT_T = 512
ATT_VROWS = B_V_DIM + 16
LOG2E = math.log2(math.e)
NEG = -1e30


def _nbytes(shape, dtype):
    return math.prod(shape) * jnp.dtype(dtype).itemsize


def _vmem_limit(*terms):
    total = sum(_nbytes(s, d) * n for s, d, n in terms) + V7X_COMPILER_SCRATCH
    return int(min(V7X_VMEM_REQUEST_CAP, total))


def _params(sem, limit):
    return pltpu.CompilerParams(dimension_semantics=sem, vmem_limit_bytes=limit)


def _silu(x):
    return x * jax.nn.sigmoid(x)


def _dot(a, b):
    return jnp.dot(a, b, preferred_element_type=F32)


def _dot_nt(a, b):
    return lax.dot_general(a, b, (((1,), (1,)), ((), ())), preferred_element_type=F32)


def _dot_tn(a, b):
    return lax.dot_general(a, b, (((0,), (0,)), ((), ())), preferred_element_type=F32)


def _adaln_kernel(c_ref, w_ref, b_ref, o_ref):
    c = c_ref[...]
    o_ref[...] = jnp.sum(_silu(c) * w_ref[...], axis=0, keepdims=True) + b_ref[...]


def _adaln(c_col, w_ada, b_ada):
    depth, d, n = w_ada.shape
    tn = 1024
    return pl.pallas_call(
        _adaln_kernel,
        out_shape=jax.ShapeDtypeStruct((depth, 1, n), F32),
        grid=(depth, n // tn),
        in_specs=[pl.BlockSpec((d, 1), lambda l, j: (0, 0)),
                  pl.BlockSpec((None, d, tn), lambda l, j: (l, 0, j)),
                  pl.BlockSpec((None, 1, tn), lambda l, j: (l, 0, j))],
        out_specs=pl.BlockSpec((None, 1, tn), lambda l, j: (l, 0, j)),
        compiler_params=_params(("parallel", "parallel"),
                                _vmem_limit(((d, LANES), F32, 2), ((d, tn), F32, 3))),
        name="adaln",
    )(c_col, w_ada, b_ada.reshape(depth, 1, n))


def _prenorm_kernel(x_ref, g_ref, sc_ref, sh_ref, o_ref):
    x = x_ref[...]
    ms = jnp.mean(x * x, axis=-1, keepdims=True)
    y = x * lax.rsqrt(ms + EPS) * g_ref[...]
    o_ref[...] = (y * (1.0 + sc_ref[...]) + sh_ref[...]).astype(o_ref.dtype)


def _prenorm(x, gains, mod, l, sc_blk, sh_blk):
    s, d = x.shape
    tm = min(512, s)
    return pl.pallas_call(
        _prenorm_kernel,
        out_shape=jax.ShapeDtypeStruct((s, d), BF16),
        grid=(s // tm,),
        in_specs=[pl.BlockSpec((tm, d), lambda i: (i, 0)),
                  pl.BlockSpec((None, 1, d), lambda i: (l, 0, 0)),
                  pl.BlockSpec((None, 1, d), lambda i: (l, 0, sc_blk)),
                  pl.BlockSpec((None, 1, d), lambda i: (l, 0, sh_blk))],
        out_specs=pl.BlockSpec((tm, d), lambda i: (i, 0)),
        compiler_params=_params(("parallel",), _vmem_limit(((tm, d), F32, 4), ((tm, d), BF16, 2))),
        name="prenorm",
    )(x, gains, mod, mod)


def _causal_conv(acc, buf_ref, cw_ref, cb_ref, first, tm):
    kconv = cw_ref.shape[0]
    tn = acc.shape[1]

    @pl.when(first)
    def _():
        buf_ref[0:SUBLANES, :] = jnp.zeros((SUBLANES, tn), F32)

    buf_ref[SUBLANES:SUBLANES + tm, :] = acc
    out = cb_ref[...] + cw_ref[kconv - 1:kconv, :] * acc
    for j in range(kconv - 1):
        off = SUBLANES - (kconv - 1) + j
        out = out + cw_ref[j:j + 1, :] * buf_ref[off:off + tm, :]
    buf_ref[0:SUBLANES, :] = buf_ref[tm:tm + SUBLANES, :]
    return out


def _proj_conv_kernel(h_ref, w_ref, cw_ref, cb_ref, sc_ref, o_ref, buf_ref, *, tm):
    acc = _dot(h_ref[...], w_ref[...])
    out = _causal_conv(acc, buf_ref, cw_ref, cb_ref, pl.program_id(1) == 0, tm)
    o_ref[...] = (_silu(out) * sc_ref[...]).astype(o_ref.dtype)


def _proj_plain_kernel(h_ref, w_ref, o_ref):
    o_ref[...] = _dot(h_ref[...], w_ref[...]).astype(o_ref.dtype)


def _proj_sigmoid_kernel(h_ref, w_ref, o_ref):
    o_ref[...] = jax.nn.sigmoid(_dot(h_ref[...], w_ref[...])).astype(o_ref.dtype)


def _proj_rope_kernel(h_ref, w_ref, ca_ref, cb_ref, cc_ref, o_ref, *, n_qtiles, qscale):
    acc = _dot(h_ref[...], w_ref[...])
    scale = jnp.where(pl.program_id(0) < n_qtiles, qscale, 1.0).astype(F32)
    ca = ca_ref[...] * scale
    cb = cb_ref[...] * scale
    cc = cc_ref[...] * scale
    for s in range(acc.shape[1] // LANES):
        xs = acc[:, s * LANES:(s + 1) * LANES]
        r = xs * ca + pltpu.roll(xs, LANES - ROPE_DIM // 2, 1) * cb + pltpu.roll(xs, ROPE_DIM // 2, 1) * cc
        o_ref[:, s * LANES:(s + 1) * LANES] = r.astype(o_ref.dtype)


def _proj(kernel, h, w, out_dtype, *, tn, extra=(), extra_specs=(), scratch=(), sequential=False,
          extra_vmem=(), name):
    s, d = h.shape
    n = w.shape[1]
    tm = min(1024, s)
    sem = ("arbitrary", "arbitrary") if sequential else ("parallel", "parallel")
    return pl.pallas_call(
        kernel,
        out_shape=jax.ShapeDtypeStruct((s, n), out_dtype),
        grid=(n // tn, s // tm),
        in_specs=[pl.BlockSpec((tm, d), lambda j, i: (i, 0)),
                  pl.BlockSpec((d, tn), lambda j, i: (0, j))] + list(extra_specs),
        out_specs=pl.BlockSpec((tm, tn), lambda j, i: (i, j)),
        scratch_shapes=list(scratch),
        compiler_params=_params(sem, _vmem_limit(((tm, d), BF16, 2), ((d, tn), BF16, 2),
                                                 ((tm, tn), out_dtype, 2), ((tm, tn), F32, 3),
                                                 *extra_vmem)),
        name=name,
    )(h, w, *extra)


def _mlstm_kernel(qk_ref, v_ref, gate_ref, bias_ref, so_ref, g_ref, o_ref, st_ref, m_ref):
    L = MLSTM_L

    @pl.when(pl.program_id(0) == 0)
    def _():
        st_ref[...] = jnp.zeros(st_ref.shape, F32)
        m_ref[...] = jnp.zeros(m_ref.shape, F32)

    gates = gate_ref[...] + bias_ref[...]
    ipre = gates[:, :LANES]
    fpre = gates[:, LANES:]
    logf = jnp.minimum(fpre, 0.0) - jnp.log1p(jnp.exp(-jnp.abs(fpre)))

    row = lax.broadcasted_iota(jnp.int32, (L, L), 0)
    col = lax.broadcasted_iota(jnp.int32, (L, L), 1)
    tri = col <= row
    tri_b = jnp.where(tri, 1.0, 0.0).astype(BF16)
    hi = logf.astype(BF16)
    r1 = logf - hi.astype(F32)
    mid = r1.astype(BF16)
    lo = (r1 - mid.astype(F32)).astype(BF16)
    b = _dot(tri_b, hi) + _dot(tri_b, mid) + _dot(tri_b, lo)
    a = ipre - b
    a_t = a.T
    m_all = m_ref[...]
    lane = lax.broadcasted_iota(jnp.int32, (1, LANES), 1)
    ones_col = jnp.where(lax.broadcasted_iota(jnp.int32, (L, LANES), 1) == 0, 1.0, 0.0).astype(BF16)
    m_next = m_all

    for h in range(M_HEADS):
        a_row = a_t[h:h + 1, :]
        a_col = a[:, h:h + 1]
        b_col = b[:, h:h + 1]
        m_prev = m_all[:, h:h + 1]
        amask = jnp.where(tri, a_row, NEG)
        m_run = jnp.maximum(jnp.max(amask, axis=1, keepdims=True), m_prev)
        dmat = jnp.exp(amask - m_run)
        w_inter = jnp.exp(m_prev - m_run)
        q = qk_ref[:, h * M_QK_DIM:(h + 1) * M_QK_DIM]
        k = qk_ref[:, M_QK + h * M_QK_DIM:M_QK + (h + 1) * M_QK_DIM]
        v_ext = jnp.concatenate([v_ref[:, h * M_V_DIM:(h + 1) * M_V_DIM], ones_col], axis=1)
        sd = (_dot_nt(q, k) * dmat).astype(BF16)
        st = st_ref[h]
        tot = w_inter * _dot(q, st.astype(BF16)) + _dot(sd, v_ext)
        num = tot[:, :M_V_DIM]
        den = tot[:, M_V_DIM:M_V_DIM + 1]
        hval = num / jnp.maximum(jnp.abs(den), jnp.exp(-(b_col + m_run)))
        m_last = m_run[L - 1:L, :]
        w_s = jnp.exp(a_col - m_last)
        decay = jnp.exp(m_prev - m_last)
        rhs = (w_s * v_ext.astype(F32)).astype(BF16)
        st_ref[h] = decay * st + _dot_tn(k, rhs)
        m_next = jnp.where(lane == h, b_col[L - 1:L, :] + m_last, m_next)
        ms = jnp.mean(hval * hval, axis=-1, keepdims=True)
        sl = slice(h * M_V_DIM, (h + 1) * M_V_DIM)
        hn = hval * lax.rsqrt(ms + EPS) * g_ref[:, sl]
        o_ref[:, sl] = (so_ref[:, sl].astype(F32) * hn).astype(o_ref.dtype)

    m_ref[...] = m_next


def _mlstm(qk, plain, gates, gate_bias, sig, g_mlstm, l):
    s = qk.shape[0]
    L = MLSTM_L
    st_shape = (M_HEADS, M_QK_DIM, M_V_DIM + LANES)
    return pl.pallas_call(
        _mlstm_kernel,
        out_shape=jax.ShapeDtypeStruct((s, M_V), BF16),
        grid=(s // L,),
        in_specs=[pl.BlockSpec((L, 2 * M_QK), lambda c: (c, 0)),
                  pl.BlockSpec((L, M_V), lambda c: (c, 0)),
                  pl.BlockSpec((L, 2 * LANES), lambda c: (c, 0)),
                  pl.BlockSpec((None, 1, 2 * LANES), lambda c: (l, 0, 0)),
                  pl.BlockSpec((L, M_V), lambda c: (c, 0)),
                  pl.BlockSpec((None, 1, M_V), lambda c: (l, 0, 0))],
        out_specs=pl.BlockSpec((L, M_V), lambda c: (c, 0)),
        scratch_shapes=[pltpu.VMEM(st_shape, F32), pltpu.VMEM((1, LANES), F32)],
        compiler_params=_params(("arbitrary",), _vmem_limit(
            ((L, 2 * M_QK), BF16, 2), ((L, M_V), BF16, 6), ((L, 2 * LANES), F32, 2),
            (st_shape, F32, 2), ((L, M_V + LANES), F32, 8), ((L, L), F32, 8))),
        name="mlstm",
    )(qk, plain, gates, gate_bias, sig, g_mlstm)


def _attn_kernel(q_ref, k_ref, v_ref, lam_ref, g_ref, o_ref, vt_ref, acc_ref, m_ref, sa_ref, sb_ref,
                 mxa_ref, mxb_ref, *, lam_init, seq):
    t = ATT_T
    qi = pl.program_id(1)

    @pl.when(qi == 0)
    def _():
        ones_row = jnp.where(lax.broadcasted_iota(jnp.int32, (ATT_VROWS - B_V_DIM, t), 0) == 0, 1.0, 0.0)
        for kb in range(seq // t):
            v_t = v_ref[kb * t:(kb + 1) * t, :].astype(F32).T
            vt_ref[kb] = jnp.concatenate([v_t, ones_row], axis=0).astype(BF16)

    q = q_ref[...]
    qlane = lax.broadcasted_iota(jnp.int32, q.shape, 1)
    zero = jnp.zeros_like(q)
    qcat = jnp.concatenate([jnp.where(qlane < B_HEAD_DIM, q, zero),
                            jnp.where(qlane >= B_HEAD_DIM, q, zero)], axis=0)

    shift = CHUNK.bit_length() - 1
    kchunk = lax.broadcasted_iota(jnp.int32, (t, 2 * t), 0) >> shift
    qchunk = (lax.broadcasted_iota(jnp.int32, (t, 2 * t), 1) & (t - 1)) >> shift
    diag_mask = kchunk <= qchunk

    def scores(kb, s_ref, mx_ref, mask=None):
        kblk = k_ref[pl.ds(pl.multiple_of(kb * t, t), t), :]
        s_t = _dot_nt(kblk, qcat)
        if mask is not None:
            s_t = jnp.where(mask, s_t, NEG)
        s_ref[...] = s_t
        mx_ref[...] = jnp.max(s_t, axis=0, keepdims=True)

    def absorb(kb, s_t, mx):
        m_old = m_ref[...]
        m_new = jnp.maximum(m_old, mx)
        p = jnp.exp2(s_t - m_new).astype(BF16)
        acc_ref[...] = jnp.exp2(m_old - m_new) * acc_ref[...] + _dot(vt_ref[kb], p)
        m_ref[...] = m_new

    acc_ref[...] = jnp.zeros(acc_ref.shape, F32)
    m_ref[...] = jnp.full(m_ref.shape, NEG, F32)
    scores(0, sa_ref, mxa_ref)

    def body(it, carry):
        scores(2 * it + 1, sb_ref, mxb_ref)
        absorb(2 * it, sa_ref[...], mxa_ref[...])
        scores(2 * it + 2, sa_ref, mxa_ref)
        absorb(2 * it + 1, sb_ref[...], mxb_ref[...])
        return carry

    lax.fori_loop(0, qi >> 1, body, 0)
    odd = (qi & 1) == 1

    @pl.when(odd)
    def _():
        scores(qi, sb_ref, mxb_ref, diag_mask)
        absorb(qi - 1, sa_ref[...], mxa_ref[...])
        absorb(qi, sb_ref[...], mxb_ref[...])

    @pl.when(jnp.logical_not(odd))
    def _():
        s_d = jnp.where(diag_mask, sa_ref[...], NEG)
        absorb(qi, s_d, jnp.max(s_d, axis=0, keepdims=True))

    lf = lam_ref[...]
    lam = (jnp.exp(jnp.sum(lf[0:1] * lf[1:2], axis=1, keepdims=True))
           - jnp.exp(jnp.sum(lf[2:3] * lf[3:4], axis=1, keepdims=True)) + lam_init)
    acc = acc_ref[...]
    o1 = acc[:B_V_DIM, :t] / acc[B_V_DIM:B_V_DIM + 1, :t]
    o2 = acc[:B_V_DIM, t:] / acc[B_V_DIM:B_V_DIM + 1, t:]
    o = (o1 - lam * o2).T
    ms = jnp.mean(o * o, axis=-1, keepdims=True)
    o_ref[...] = (o * lax.rsqrt(ms + EPS) * (g_ref[...] * (1.0 - lam_init))).astype(o_ref.dtype)


def _attention(qkb, plain, lambdas, g_diff, l, lam_init):
    s = qkb.shape[0]
    t = ATT_T
    kern = functools.partial(_attn_kernel, lam_init=lam_init, seq=s)
    return pl.pallas_call(
        kern,
        out_shape=jax.ShapeDtypeStruct((s, B_V), BF16),
        grid=(B_HEADS, s // t),
        in_specs=[pl.BlockSpec((t, LANES), lambda h, i: (i, h)),
                  pl.BlockSpec((s, LANES), lambda h, i: (0, B_HEADS + h)),
                  pl.BlockSpec((s, LANES), lambda h, i: (0, B_HEADS + h)),
                  pl.BlockSpec((None, 4, B_HEAD_DIM), lambda h, i: (l, 0, 0)),
                  pl.BlockSpec((None, 1, LANES), lambda h, i: (l, 0, h))],
        out_specs=pl.BlockSpec((t, LANES), lambda h, i: (i, h)),
        scratch_shapes=[pltpu.VMEM((s // t, ATT_VROWS, t), BF16),
                        pltpu.VMEM((ATT_VROWS, 2 * t), F32),
                        pltpu.VMEM((1, 2 * t), F32),
                        pltpu.VMEM((t, 2 * t), F32),
                        pltpu.VMEM((t, 2 * t), F32),
                        pltpu.VMEM((1, 2 * t), F32),
                        pltpu.VMEM((1, 2 * t), F32)],
        compiler_params=_params(("arbitrary", "arbitrary"), _vmem_limit(
            ((s, LANES), BF16, 4), ((s // t, ATT_VROWS, t), BF16, 1), ((t, LANES), BF16, 6),
            ((t, 2 * t), F32, 8), ((ATT_VROWS, 2 * t), F32, 4))),
        name="diff_attn",
    )(qkb, qkb, plain, lambdas, g_diff)


def _merge_kernel(a_ref, b_ref, wa_ref, wb_ref, ga_ref, gb_ref, o_ref):
    ya = _dot(a_ref[...], wa_ref[...])
    yb = _dot(b_ref[...], wb_ref[...])
    o_ref[...] = (ga_ref[...].astype(F32) * ya + gb_ref[...].astype(F32) * yb).astype(o_ref.dtype)


def _merge(ya_in, yb_in, w_a, w_b, sig):
    s, d = ya_in.shape
    n = w_a.shape[1]
    tm, tn = min(512, s), 1024
    ga0 = M_V // tn
    gb0 = ga0 + n // tn
    return pl.pallas_call(
        _merge_kernel,
        out_shape=jax.ShapeDtypeStruct((s, n), BF16),
        grid=(n // tn, s // tm),
        in_specs=[pl.BlockSpec((tm, d), lambda j, i: (i, 0)),
                  pl.BlockSpec((tm, d), lambda j, i: (i, 0)),
                  pl.BlockSpec((d, tn), lambda j, i: (0, j)),
                  pl.BlockSpec((d, tn), lambda j, i: (0, j)),
                  pl.BlockSpec((tm, tn), lambda j, i: (i, ga0 + j)),
                  pl.BlockSpec((tm, tn), lambda j, i: (i, gb0 + j))],
        out_specs=pl.BlockSpec((tm, tn), lambda j, i: (i, j)),
        compiler_params=_params(("parallel", "parallel"), _vmem_limit(
            ((tm, d), BF16, 4), ((d, tn), BF16, 4), ((tm, tn), BF16, 6), ((tm, tn), F32, 4))),
        name="merge",
    )(ya_in, yb_in, w_a, w_b, sig, sig)


def _proj_res_kernel(a_ref, w_ref, x_ref, g_ref, gt_ref, o_ref):
    y = _dot(a_ref[...], w_ref[...])
    ms = jnp.mean(y * y, axis=-1, keepdims=True)
    o_ref[...] = x_ref[...] + gt_ref[...] * (y * lax.rsqrt(ms + EPS) * g_ref[...])


def _proj_res(a, w, x, gains, mod, l, gt_blk, tm, name):
    s, k = a.shape
    d = w.shape[1]
    tm = min(tm, s)
    return pl.pallas_call(
        _proj_res_kernel,
        out_shape=jax.ShapeDtypeStruct((s, d), F32),
        grid=(s // tm,),
        in_specs=[pl.BlockSpec((tm, k), lambda i: (i, 0)),
                  pl.BlockSpec((k, d), lambda i: (0, 0), pipeline_mode=pl.Buffered(1)),
                  pl.BlockSpec((tm, d), lambda i: (i, 0)),
                  pl.BlockSpec((None, 1, d), lambda i: (l, 0, 0)),
                  pl.BlockSpec((None, 1, d), lambda i: (l, 0, gt_blk))],
        out_specs=pl.BlockSpec((tm, d), lambda i: (i, 0)),
        compiler_params=_params(("parallel",), _vmem_limit(
            ((tm, k), BF16, 2), ((k, d), BF16, 1), ((tm, d), F32, 7))),
        name=name,
    )(a, w, x, gains, mod)


def _ffn_up_kernel(h_ref, wg_ref, wu_ref, cw_ref, cb_ref, o_ref, buf_ref, *, tm):
    h = h_ref[...]
    g = _causal_conv(_dot(h, wg_ref[...]), buf_ref, cw_ref, cb_ref, pl.program_id(1) == 0, tm)
    o_ref[...] = (_silu(g) * _dot(h, wu_ref[...])).astype(o_ref.dtype)


def _ffn_up(h, w_gate, w_up, conv_w, conv_b, l):
    s, d = h.shape
    n = w_gate.shape[1]
    tm, tn = min(1024, s), 512
    kconv = conv_w.shape[1]
    kern = functools.partial(_ffn_up_kernel, tm=tm)
    return pl.pallas_call(
        kern,
        out_shape=jax.ShapeDtypeStruct((s, n), BF16),
        grid=(n // tn, s // tm),
        in_specs=[pl.BlockSpec((tm, d), lambda j, i: (i, 0)),
                  pl.BlockSpec((d, tn), lambda j, i: (0, j)),
                  pl.BlockSpec((d, tn), lambda j, i: (0, j)),
                  pl.BlockSpec((None, kconv, tn), lambda j, i: (l, 0, j)),
                  pl.BlockSpec((None, 1, tn), lambda j, i: (l, 0, j))],
        out_specs=pl.BlockSpec((tm, tn), lambda j, i: (i, j)),
        scratch_shapes=[pltpu.VMEM((tm + SUBLANES, tn), F32)],
        compiler_params=_params(("arbitrary", "arbitrary"), _vmem_limit(
            ((tm, d), BF16, 2), ((d, tn), BF16, 4), ((tm, tn), BF16, 2), ((tm, tn), F32, 6))),
        name="ffn_up",
    )(h, w_gate, w_up, conv_w, conv_b)


def _rope_tables(positions):
    half = ROPE_DIM // 2
    inv = jnp.power(ROPE_THETA, -(jnp.arange(half, dtype=F32) * 2.0 / ROPE_DIM))
    ang = positions.astype(F32)[:, None] * inv
    cos, sin = jnp.cos(ang), jnp.sin(ang)
    s = positions.shape[0]
    rest = B_HEAD_DIM - ROPE_DIM
    reps = LANES // B_HEAD_DIM
    ca = jnp.tile(jnp.concatenate([cos, cos, jnp.ones((s, rest), F32)], axis=1), (1, reps))
    cb = jnp.tile(jnp.concatenate([-sin, jnp.zeros((s, B_HEAD_DIM - half), F32)], axis=1), (1, reps))
    cc = jnp.tile(jnp.concatenate([jnp.zeros((s, half), F32), sin, jnp.zeros((s, rest), F32)], axis=1),
                  (1, reps))
    return ca, cb, cc


def kernel(x, c, positions, w_ada, b_ada, g_pre_mix, g_post_mix, g_pre_ffn, g_post_ffn, w_in, conv_qk_w, conv_qk_b, b_igate, b_fgate, g_mlstm, lambdas, g_diff, w_a, w_b, w_out, w_gate, w_up, conv_ffn_w, conv_ffn_b, w_down):
    batch, s, d = x.shape
    depth = w_in.shape[0]
    assert batch == 1 and s % 1024 == 0 and d == M_V == B_V
    xs = x.reshape(s, d)
    tm_proj = min(1024, s)

    mod = _adaln(c.reshape(d, 1), w_ada, b_ada)
    ca, cb, cc = _rope_tables(positions[0])
    rope_specs = [pl.BlockSpec((tm_proj, LANES), lambda j, i: (i, 0))] * 3

    def gains(g):
        return g.reshape(depth, 1, g.shape[1])

    g_pre_mix, g_post_mix, g_pre_ffn, g_post_ffn = map(gains, (g_pre_mix, g_post_mix, g_pre_ffn, g_post_ffn))
    g_mlstm3, g_diff3 = gains(g_mlstm), gains(g_diff)
    conv_qk_b3, conv_ffn_b3 = gains(conv_qk_b), gains(conv_ffn_b)
    o_qm, o_km, o_vm, o_om = 0, M_QK, 2 * M_QK, 2 * M_QK + M_V
    o_i = o_om + M_V
    o_f = o_i + M_HEADS
    o_qb = o_f + M_HEADS
    o_kb, o_vb = o_qb + B_QK, o_qb + 2 * B_QK
    o_ga = o_vb + B_V
    o_gb = o_ga + d
    gate_pad = jnp.zeros((d, LANES - M_HEADS), F32)
    bias_pad = jnp.zeros((depth, LANES - M_HEADS), F32)
    gate_bias = jnp.concatenate([b_igate, bias_pad, b_fgate, bias_pad], axis=1).reshape(depth, 1, 2 * LANES)
    q_scale = jnp.concatenate([jnp.full((1, M_QK), M_QK_DIM ** -0.5, F32), jnp.ones((1, M_QK), F32)], axis=1)

    for l in range(depth):
        lam_init = 0.8 - 0.6 * math.exp(-0.3 * l)
        wl = w_in[l]
        w_conv = wl[:, o_qm:o_vm].astype(BF16)
        w_plain = jnp.concatenate([wl[:, o_vm:o_om], wl[:, o_vb:o_ga]], axis=1).astype(BF16)
        w_sig = jnp.concatenate([wl[:, o_om:o_i], wl[:, o_ga:o_gb + d]], axis=1).astype(BF16)
        w_rope = wl[:, o_qb:o_vb].astype(BF16)
        w_gates = jnp.concatenate([wl[:, o_i:o_f], gate_pad, wl[:, o_f:o_qb], gate_pad], axis=1).astype(BF16)

        h = _prenorm(xs, g_pre_mix, mod, l, 1, 0)
        tn = 1024
        qk = _proj(functools.partial(_proj_conv_kernel, tm=tm_proj), h, w_conv, BF16, tn=tn,
                   extra=(conv_qk_w, conv_qk_b3, q_scale),
                   extra_specs=[pl.BlockSpec((None, conv_qk_w.shape[1], tn), lambda j, i: (l, 0, j)),
                                pl.BlockSpec((None, 1, tn), lambda j, i: (l, 0, j)),
                                pl.BlockSpec((1, tn), lambda j, i: (0, j))],
                   scratch=[pltpu.VMEM((tm_proj + SUBLANES, tn), F32)], sequential=True,
                   extra_vmem=(((tm_proj, tn), F32, 3),), name="proj_conv")
        plain = _proj(_proj_plain_kernel, h, w_plain, BF16, tn=tn, name="proj_plain")
        sig = _proj(_proj_sigmoid_kernel, h, w_sig, BF16, tn=tn, name="proj_sigmoid")
        qkb = _proj(functools.partial(_proj_rope_kernel, n_qtiles=B_QK // tn, qscale=B_HEAD_DIM ** -0.5 * LOG2E),
                    h, w_rope, BF16, tn=tn, extra=(ca, cb, cc), extra_specs=rope_specs,
                    extra_vmem=(((tm_proj, LANES), F32, 9),), name="proj_rope")
        gates = _proj(_proj_plain_kernel, h, w_gates, F32, tn=2 * LANES, name="proj_gates")

        y_a_in = _mlstm(qk, plain, gates, gate_bias, sig, g_mlstm3, l)
        y_b_in = _attention(qkb, plain, lambdas, g_diff3, l, lam_init)
        merged = _merge(y_a_in, y_b_in, w_a[l].astype(BF16), w_b[l].astype(BF16), sig)
        xs = _proj_res(merged, w_out[l].astype(BF16), xs, g_post_mix, mod, l, 2, 512, "out_proj")

        h = _prenorm(xs, g_pre_ffn, mod, l, 4, 3)
        act = _ffn_up(h, w_gate[l].astype(BF16), w_up[l].astype(BF16), conv_ffn_w, conv_ffn_b3, l)
        xs = _proj_res(act, w_down[l].astype(BF16), xs, g_post_ffn, mod, l, 5, 256, "ffn_down")

    return xs.reshape(batch, s, d)
```

```python
import functools
import math

import jax
import jax.numpy as jnp
from jax import lax
from jax.experimental import pallas as pl
from jax.experimental.pallas import tpu as pltpu

F32 = jnp.float32
BF16 = jnp.bfloat16

M_HEADS = 4
M_QK_DIM = 256
M_V_DIM = 512
M_QK = M_HEADS * M_QK_DIM
M_V = M_HEADS * M_V_DIM
B_HEADS = 16
B_HEAD_DIM = 64
B_V_DIM = 2 * B_HEAD_DIM
B_QK = B_HEADS * 2 * B_HEAD_DIM
B_V = B_HEADS * B_V_DIM
CHUNK = 64
ROPE_THETA = 500000.0
ROPE_DIM = B_HEAD_DIM // 4
EPS = 1e-6

LANES = 128
SUBLANES = 8
V7X_VMEM_REQUEST_CAP = 60000 * 1024
V7X_COMPILER_SCRATCH = 8 * 1024 * 1024

MLSTM_L = 256
ATT_T = 512
ATT_VROWS = B_V_DIM + 16
LOG2E = math.log2(math.e)
NEG = -1e30


def _nbytes(shape, dtype):
    return math.prod(shape) * jnp.dtype(dtype).itemsize


def _vmem_limit(*terms):
    total = sum(_nbytes(s, d) * n for s, d, n in terms) + V7X_COMPILER_SCRATCH
    return int(min(V7X_VMEM_REQUEST_CAP, total))


def _params(sem, limit):
    return pltpu.CompilerParams(dimension_semantics=sem, vmem_limit_bytes=limit)


def _silu(x):
    return x * jax.nn.sigmoid(x)


def _dot(a, b):
    return jnp.dot(a, b, preferred_element_type=F32)


def _dot_nt(a, b):
    return lax.dot_general(a, b, (((1,), (1,)), ((), ())), preferred_element_type=F32)


def _dot_tn(a, b):
    return lax.dot_general(a, b, (((0,), (0,)), ((), ())), preferred_element_type=F32)


def _adaln_kernel(c_ref, w_ref, b_ref, o_ref):
    c = c_ref[...]
    o_ref[...] = jnp.sum(_silu(c) * w_ref[...], axis=0, keepdims=True) + b_ref[...]


def _adaln(c_col, w_ada, b_ada):
    depth, d, n = w_ada.shape
    tn = 1024
    return pl.pallas_call(
        _adaln_kernel,
        out_shape=jax.ShapeDtypeStruct((depth, 1, n), F32),
        grid=(depth, n // tn),
        in_specs=[pl.BlockSpec((d, 1), lambda l, j: (0, 0)),
                  pl.BlockSpec((None, d, tn), lambda l, j: (l, 0, j)),
                  pl.BlockSpec((None, 1, tn), lambda l, j: (l, 0, j))],
        out_specs=pl.BlockSpec((None, 1, tn), lambda l, j: (l, 0, j)),
        compiler_params=_params(("parallel", "parallel"),
                                _vmem_limit(((d, LANES), F32, 2), ((d, tn), F32, 3))),
        name="adaln",
    )(c_col, w_ada, b_ada.reshape(depth, 1, n))


def _prenorm_kernel(x_ref, g_ref, sc_ref, sh_ref, o_ref):
    x = x_ref[...]
    ms = jnp.mean(x * x, axis=-1, keepdims=True)
    y = x * lax.rsqrt(ms + EPS) * g_ref[...]
    o_ref[...] = (y * (1.0 + sc_ref[...]) + sh_ref[...]).astype(o_ref.dtype)


def _prenorm(x, gains, mod, l, sc_blk, sh_blk):
    s, d = x.shape
    tm = min(512, s)
    return pl.pallas_call(
        _prenorm_kernel,
        out_shape=jax.ShapeDtypeStruct((s, d), BF16),
        grid=(s // tm,),
        in_specs=[pl.BlockSpec((tm, d), lambda i: (i, 0)),
                  pl.BlockSpec((None, 1, d), lambda i: (l, 0, 0)),
                  pl.BlockSpec((None, 1, d), lambda i: (l, 0, sc_blk)),
                  pl.BlockSpec((None, 1, d), lambda i: (l, 0, sh_blk))],
        out_specs=pl.BlockSpec((tm, d), lambda i: (i, 0)),
        compiler_params=_params(("parallel",), _vmem_limit(((tm, d), F32, 4), ((tm, d), BF16, 2))),
        name="prenorm",
    )(x, gains, mod, mod)


def _first_row_tile():
    return pl.program_id(1) == 0


def _cast_weight(w_ref, wb_ref):
    @pl.when(_first_row_tile())
    def _():
        wb_ref[...] = w_ref[...].astype(BF16)


def _causal_conv(acc, buf_ref, cw_ref, cb_ref, cs, tm):
    kconv = cw_ref.shape[0]
    buf_ref[SUBLANES:2 * SUBLANES, cs] = acc[0:SUBLANES, :]
    out = cb_ref[:, cs] + cw_ref[kconv - 1:kconv, cs] * acc
    for j in range(kconv - 1):
        dist = kconv - 1 - j
        top = buf_ref[SUBLANES - dist:2 * SUBLANES - dist, cs]
        shifted = jnp.concatenate([top, pltpu.roll(acc, dist, 0)[SUBLANES:, :]], axis=0)
        out = out + cw_ref[j:j + 1, cs] * shifted
    buf_ref[0:SUBLANES, cs] = acc[tm - SUBLANES:tm, :]
    return out


def _proj_conv_kernel(h_ref, w_ref, cw_ref, cb_ref, sc_ref, o_ref, wb_ref, buf_ref, *, tm):
    _cast_weight(w_ref, wb_ref)

    @pl.when(_first_row_tile())
    def _():
        buf_ref[0:SUBLANES, :] = jnp.zeros((SUBLANES, buf_ref.shape[1]), F32)

    acc = _dot(h_ref[...], wb_ref[...])
    out = _causal_conv(acc, buf_ref, cw_ref, cb_ref, slice(None), tm)
    o_ref[...] = (_silu(out) * sc_ref[...]).astype(o_ref.dtype)


def _proj_plain_kernel(h_ref, w_ref, o_ref, wb_ref):
    _cast_weight(w_ref, wb_ref)
    o_ref[...] = _dot(h_ref[...], wb_ref[...]).astype(o_ref.dtype)


def _proj_mixed_kernel(h_ref, w_ref, o_ref, wb_ref, *, n_plain):
    _cast_weight(w_ref, wb_ref)
    acc = _dot(h_ref[...], wb_ref[...])
    j = pl.program_id(0)

    @pl.when(j < n_plain)
    def _():
        o_ref[...] = acc.astype(o_ref.dtype)

    @pl.when(j >= n_plain)
    def _():
        o_ref[...] = jax.nn.sigmoid(acc).astype(o_ref.dtype)


def _proj_rope_kernel(h_ref, w_ref, ca_ref, cb_ref, cc_ref, o_ref, wb_ref, *, n_qtiles, qscale):
    _cast_weight(w_ref, wb_ref)
    acc = _dot(h_ref[...], wb_ref[...])
    scale = jnp.where(pl.program_id(0) < n_qtiles, qscale, 1.0).astype(F32)
    ca = ca_ref[...] * scale
    cb = cb_ref[...] * scale
    cc = cc_ref[...] * scale
    for s in range(acc.shape[1] // LANES):
        xs = acc[:, s * LANES:(s + 1) * LANES]
        r = xs * ca + pltpu.roll(xs, LANES - ROPE_DIM // 2, 1) * cb + pltpu.roll(xs, ROPE_DIM // 2, 1) * cc
        o_ref[:, s * LANES:(s + 1) * LANES] = r.astype(o_ref.dtype)


def _proj(kernel, h, w, l, col0, n, out_dtype, *, tn, extra=(), extra_specs=(), scratch=(), extra_vmem=(),
          name):
    s, d = h.shape
    tm = min(1024, s)
    blk0 = col0 // tn
    assert blk0 * tn == col0 and n % tn == 0
    return pl.pallas_call(
        kernel,
        out_shape=jax.ShapeDtypeStruct((s, n), out_dtype),
        grid=(n // tn, s // tm),
        in_specs=[pl.BlockSpec((tm, d), lambda j, i: (i, 0)),
                  pl.BlockSpec((None, d, tn), lambda j, i: (l, 0, blk0 + j))] + list(extra_specs),
        out_specs=pl.BlockSpec((tm, tn), lambda j, i: (i, j)),
        scratch_shapes=[pltpu.VMEM((d, tn), BF16)] + list(scratch),
        compiler_params=_params(("parallel", "arbitrary"), _vmem_limit(
            ((tm, d), BF16, 2), ((d, tn), F32, 2), ((d, tn), BF16, 1), ((tm, tn), out_dtype, 2),
            ((tm, tn), F32, 3), *extra_vmem)),
        name=name,
    )(h, w, *extra)


def _mlstm_kernel(qk_ref, v_ref, gate_ref, bias_ref, so_ref, g_ref, o_ref, st_ref, m_ref):
    L = MLSTM_L

    @pl.when(pl.program_id(0) == 0)
    def _():
        st_ref[...] = jnp.zeros(st_ref.shape, F32)
        m_ref[...] = jnp.zeros(m_ref.shape, F32)

    ipre = gate_ref[...] + bias_ref[...]
    fpre = pltpu.roll(ipre, LANES - M_HEADS, 1)
    logf = jnp.minimum(fpre, 0.0) - jnp.log1p(jnp.exp(-jnp.abs(fpre)))

    row = lax.broadcasted_iota(jnp.int32, (L, L), 0)
    col = lax.broadcasted_iota(jnp.int32, (L, L), 1)
    tri = col <= row
    tri_b = jnp.where(tri, 1.0, 0.0).astype(BF16)
    hi = logf.astype(BF16)
    r1 = logf - hi.astype(F32)
    mid = r1.astype(BF16)
    lo = (r1 - mid.astype(F32)).astype(BF16)
    b = _dot(tri_b, hi) + _dot(tri_b, mid) + _dot(tri_b, lo)
    a = ipre - b
    a_t = a.T
    m_all = m_ref[...]
    lane = lax.broadcasted_iota(jnp.int32, (1, LANES), 1)
    ones_col = jnp.where(lax.broadcasted_iota(jnp.int32, (L, LANES), 1) == 0, 1.0, 0.0).astype(BF16)
    m_next = m_all

    for h in range(M_HEADS):
        a_row = a_t[h:h + 1, :]
        a_col = a[:, h:h + 1]
        b_col = b[:, h:h + 1]
        m_prev = m_all[:, h:h + 1]
        amask = jnp.where(tri, a_row, NEG)
        m_run = jnp.maximum(jnp.max(amask, axis=1, keepdims=True), m_prev)
        dmat = jnp.exp(amask - m_run)
        w_inter = jnp.exp(m_prev - m_run)
        q = qk_ref[:, h * M_QK_DIM:(h + 1) * M_QK_DIM]
        k = qk_ref[:, M_QK + h * M_QK_DIM:M_QK + (h + 1) * M_QK_DIM]
        v_ext = jnp.concatenate([v_ref[:, h * M_V_DIM:(h + 1) * M_V_DIM], ones_col], axis=1)
        sd = (_dot_nt(q, k) * dmat).astype(BF16)
        st = st_ref[h]
        tot = w_inter * _dot(q, st.astype(BF16)) + _dot(sd, v_ext)
        num = tot[:, :M_V_DIM]
        den = tot[:, M_V_DIM:M_V_DIM + 1]
        hval = num / jnp.maximum(jnp.abs(den), jnp.exp(-(b_col + m_run)))
        m_last = m_run[L - 1:L, :]
        w_s = jnp.exp(a_col - m_last)
        decay = jnp.exp(m_prev - m_last)
        rhs = (w_s * v_ext.astype(F32)).astype(BF16)
        st_ref[h] = decay * st + _dot_tn(k, rhs)
        m_next = jnp.where(lane == h, b_col[L - 1:L, :] + m_last, m_next)
        ms = jnp.mean(hval * hval, axis=-1, keepdims=True)
        sl = slice(h * M_V_DIM, (h + 1) * M_V_DIM)
        hn = hval * lax.rsqrt(ms + EPS) * g_ref[:, sl]
        o_ref[:, sl] = (so_ref[:, sl].astype(F32) * hn).astype(o_ref.dtype)

    m_ref[...] = m_next


def _mlstm(qk, v_so, gates, gate_bias, g_mlstm, l):
    s = qk.shape[0]
    L = MLSTM_L
    st_shape = (M_HEADS, M_QK_DIM, M_V_DIM + LANES)
    return pl.pallas_call(
        _mlstm_kernel,
        out_shape=jax.ShapeDtypeStruct((s, M_V), BF16),
        grid=(s // L,),
        in_specs=[pl.BlockSpec((L, 2 * M_QK), lambda c: (c, 0)),
                  pl.BlockSpec((L, M_V), lambda c: (c, 0)),
                  pl.BlockSpec((L, LANES), lambda c: (c, 0)),
                  pl.BlockSpec((None, 1, LANES), lambda c: (l, 0, 0)),
                  pl.BlockSpec((L, M_V), lambda c: (c, 1)),
                  pl.BlockSpec((None, 1, M_V), lambda c: (l, 0, 0))],
        out_specs=pl.BlockSpec((L, M_V), lambda c: (c, 0)),
        scratch_shapes=[pltpu.VMEM(st_shape, F32), pltpu.VMEM((1, LANES), F32)],
        compiler_params=_params(("arbitrary",), _vmem_limit(
            ((L, 2 * M_QK), BF16, 2), ((L, M_V), BF16, 6), ((L, LANES), F32, 2),
            (st_shape, F32, 2), ((L, M_V + LANES), F32, 8), ((L, L), F32, 8))),
        name="mlstm",
    )(qk, v_so, gates, gate_bias, v_so, g_mlstm)


def _attn_kernel(q_ref, k_ref, v_ref, lam_ref, g_ref, o_ref, vt_ref, acc_ref, m_ref, sa_ref, sb_ref,
                 mxa_ref, mxb_ref, *, lam_init, seq):
    t = ATT_T
    qi = pl.program_id(1)

    @pl.when(qi == 0)
    def _():
        ones_row = jnp.where(lax.broadcasted_iota(jnp.int32, (ATT_VROWS - B_V_DIM, t), 0) == 0, 1.0, 0.0)
        for kb in range(seq // t):
            v_t = v_ref[kb * t:(kb + 1) * t, :].astype(F32).T
            vt_ref[kb] = jnp.concatenate([v_t, ones_row], axis=0).astype(BF16)

    q = q_ref[...]
    qlane = lax.broadcasted_iota(jnp.int32, q.shape, 1)
    zero = jnp.zeros_like(q)
    qcat = jnp.concatenate([jnp.where(qlane < B_HEAD_DIM, q, zero),
                            jnp.where(qlane >= B_HEAD_DIM, q, zero)], axis=0)

    shift = CHUNK.bit_length() - 1
    kchunk = lax.broadcasted_iota(jnp.int32, (t, 2 * t), 0) >> shift
    qchunk = (lax.broadcasted_iota(jnp.int32, (t, 2 * t), 1) & (t - 1)) >> shift
    diag_mask = kchunk <= qchunk

    def scores(kb, s_ref, mx_ref, mask=None):
        kblk = k_ref[pl.ds(pl.multiple_of(kb * t, t), t), :]
        s_t = _dot_nt(kblk, qcat)
        if mask is not None:
            s_t = jnp.where(mask, s_t, NEG)
        s_ref[...] = s_t
        mx_ref[...] = jnp.max(s_t, axis=0, keepdims=True)

    def absorb(kb, s_t, mx):
        m_old = m_ref[...]
        m_new = jnp.maximum(m_old, mx)
        p = jnp.exp2(s_t - m_new).astype(BF16)
        acc_ref[...] = jnp.exp2(m_old - m_new) * acc_ref[...] + _dot(vt_ref[kb], p)
        m_ref[...] = m_new

    acc_ref[...] = jnp.zeros(acc_ref.shape, F32)
    m_ref[...] = jnp.full(m_ref.shape, NEG, F32)
    scores(0, sa_ref, mxa_ref)

    def body(it, carry):
        scores(2 * it + 1, sb_ref, mxb_ref)
        absorb(2 * it, sa_ref[...], mxa_ref[...])
        scores(2 * it + 2, sa_ref, mxa_ref)
        absorb(2 * it + 1, sb_ref[...], mxb_ref[...])
        return carry

    lax.fori_loop(0, qi >> 1, body, 0)
    odd = (qi & 1) == 1

    @pl.when(odd)
    def _():
        scores(qi, sb_ref, mxb_ref, diag_mask)
        absorb(qi - 1, sa_ref[...], mxa_ref[...])
        absorb(qi, sb_ref[...], mxb_ref[...])

    @pl.when(jnp.logical_not(odd))
    def _():
        s_d = jnp.where(diag_mask, sa_ref[...], NEG)
        absorb(qi, s_d, jnp.max(s_d, axis=0, keepdims=True))

    lf = lam_ref[...]
    lam = (jnp.exp(jnp.sum(lf[0:1] * lf[1:2], axis=1, keepdims=True))
           - jnp.exp(jnp.sum(lf[2:3] * lf[3:4], axis=1, keepdims=True)) + lam_init)
    acc = acc_ref[...]
    o1 = acc[:B_V_DIM, :t] / acc[B_V_DIM:B_V_DIM + 1, :t]
    o2 = acc[:B_V_DIM, t:] / acc[B_V_DIM:B_V_DIM + 1, t:]
    o = (o1 - lam * o2).T
    ms = jnp.mean(o * o, axis=-1, keepdims=True)
    o_ref[...] = (o * lax.rsqrt(ms + EPS) * (g_ref[...] * (1.0 - lam_init))).astype(o_ref.dtype)


def _attention(qkb, v_sg, lambdas, g_diff, l, lam_init):
    s = qkb.shape[0]
    t = ATT_T
    kern = functools.partial(_attn_kernel, lam_init=lam_init, seq=s)
    return pl.pallas_call(
        kern,
        out_shape=jax.ShapeDtypeStruct((s, B_V), BF16),
        grid=(B_HEADS, s // t),
        in_specs=[pl.BlockSpec((t, LANES), lambda h, i: (i, h)),
                  pl.BlockSpec((s, LANES), lambda h, i: (0, B_HEADS + h)),
                  pl.BlockSpec((s, LANES), lambda h, i: (0, h)),
                  pl.BlockSpec((None, 4, B_HEAD_DIM), lambda h, i: (l, 0, 0)),
                  pl.BlockSpec((None, 1, LANES), lambda h, i: (l, 0, h))],
        out_specs=pl.BlockSpec((t, LANES), lambda h, i: (i, h)),
        scratch_shapes=[pltpu.VMEM((s // t, ATT_VROWS, t), BF16),
                        pltpu.VMEM((ATT_VROWS, 2 * t), F32),
                        pltpu.VMEM((1, 2 * t), F32),
                        pltpu.VMEM((t, 2 * t), F32),
                        pltpu.VMEM((t, 2 * t), F32),
                        pltpu.VMEM((1, 2 * t), F32),
                        pltpu.VMEM((1, 2 * t), F32)],
        compiler_params=_params(("arbitrary", "arbitrary"), _vmem_limit(
            ((s, LANES), BF16, 4), ((s // t, ATT_VROWS, t), BF16, 1), ((t, LANES), BF16, 6),
            ((t, 2 * t), F32, 8), ((ATT_VROWS, 2 * t), F32, 4))),
        name="diff_attn",
    )(qkb, qkb, v_sg, lambdas, g_diff)


def _merge_kernel(a_ref, b_ref, wa_ref, wb_ref, ga_ref, gb_ref, o_ref, wab_ref, wbb_ref):
    _cast_weight(wa_ref, wab_ref)
    _cast_weight(wb_ref, wbb_ref)
    ya = _dot(a_ref[...], wab_ref[...])
    yb = _dot(b_ref[...], wbb_ref[...])
    o_ref[...] = (ga_ref[...].astype(F32) * ya + gb_ref[...].astype(F32) * yb).astype(o_ref.dtype)


def _merge(ya_in, yb_in, w_a, w_b, v_sg, l):
    s, d = ya_in.shape
    n = w_a.shape[2]
    tm, tn = min(512, s), 512
    ga0 = B_V // tn
    gb0 = ga0 + n // tn
    return pl.pallas_call(
        _merge_kernel,
        out_shape=jax.ShapeDtypeStruct((s, n), BF16),
        grid=(n // tn, s // tm),
        in_specs=[pl.BlockSpec((tm, d), lambda j, i: (i, 0)),
                  pl.BlockSpec((tm, d), lambda j, i: (i, 0)),
                  pl.BlockSpec((None, d, tn), lambda j, i: (l, 0, j)),
                  pl.BlockSpec((None, d, tn), lambda j, i: (l, 0, j)),
                  pl.BlockSpec((tm, tn), lambda j, i: (i, ga0 + j)),
                  pl.BlockSpec((tm, tn), lambda j, i: (i, gb0 + j))],
        out_specs=pl.BlockSpec((tm, tn), lambda j, i: (i, j)),
        scratch_shapes=[pltpu.VMEM((d, tn), BF16), pltpu.VMEM((d, tn), BF16)],
        compiler_params=_params(("parallel", "arbitrary"), _vmem_limit(
            ((tm, d), BF16, 4), ((d, tn), F32, 4), ((d, tn), BF16, 2), ((tm, tn), BF16, 6),
            ((tm, tn), F32, 4))),
        name="merge",
    )(ya_in, yb_in, w_a, w_b, v_sg, v_sg)


def _proj_res_kernel(a_ref, w_ref, x_ref, g_ref, gt_ref, o_ref):
    y = _dot(a_ref[...], w_ref[...])
    ms = jnp.mean(y * y, axis=-1, keepdims=True)
    o_ref[...] = x_ref[...] + gt_ref[...] * (y * lax.rsqrt(ms + EPS) * g_ref[...])


def _proj_res(a, w, x, gains, mod, l, gt_blk, tm, name):
    s, k = a.shape
    d = w.shape[1]
    tm = min(tm, s)
    return pl.pallas_call(
        _proj_res_kernel,
        out_shape=jax.ShapeDtypeStruct((s, d), F32),
        grid=(s // tm,),
        in_specs=[pl.BlockSpec((tm, k), lambda i: (i, 0)),
                  pl.BlockSpec((k, d), lambda i: (0, 0), pipeline_mode=pl.Buffered(1)),
                  pl.BlockSpec((tm, d), lambda i: (i, 0)),
                  pl.BlockSpec((None, 1, d), lambda i: (l, 0, 0)),
                  pl.BlockSpec((None, 1, d), lambda i: (l, 0, gt_blk))],
        out_specs=pl.BlockSpec((tm, d), lambda i: (i, 0)),
        compiler_params=_params(("parallel",), _vmem_limit(
            ((tm, k), BF16, 2), ((k, d), BF16, 1), ((tm, d), F32, 7))),
        name=name,
    )(a, w, x, gains, mod)


FFN_SUB = 256


def _ffn_up_kernel(h_ref, wg_ref, wu_ref, cw_ref, cb_ref, o_ref, wgb_ref, wub_ref, buf_ref, *, tm):
    _cast_weight(wg_ref, wgb_ref)
    _cast_weight(wu_ref, wub_ref)

    @pl.when(_first_row_tile())
    def _():
        buf_ref[0:SUBLANES, :] = jnp.zeros((SUBLANES, buf_ref.shape[1]), F32)

    h = h_ref[...]
    for c in range(o_ref.shape[1] // FFN_SUB):
        cs = slice(c * FFN_SUB, (c + 1) * FFN_SUB)
        g = _causal_conv(_dot(h, wgb_ref[:, cs]), buf_ref, cw_ref, cb_ref, cs, tm)
        o_ref[:, cs] = (_silu(g) * _dot(h, wub_ref[:, cs])).astype(o_ref.dtype)


def _ffn_up(h, w_gate, w_up, conv_w, conv_b, l):
    s, d = h.shape
    n = w_gate.shape[2]
    tm, tn = min(1024, s), 512
    kconv = conv_w.shape[1]
    kern = functools.partial(_ffn_up_kernel, tm=tm)
    return pl.pallas_call(
        kern,
        out_shape=jax.ShapeDtypeStruct((s, n), BF16),
        grid=(n // tn, s // tm),
        in_specs=[pl.BlockSpec((tm, d), lambda j, i: (i, 0)),
                  pl.BlockSpec((None, d, tn), lambda j, i: (l, 0, j)),
                  pl.BlockSpec((None, d, tn), lambda j, i: (l, 0, j)),
                  pl.BlockSpec((None, kconv, tn), lambda j, i: (l, 0, j)),
                  pl.BlockSpec((None, 1, tn), lambda j, i: (l, 0, j))],
        out_specs=pl.BlockSpec((tm, tn), lambda j, i: (i, j)),
        scratch_shapes=[pltpu.VMEM((d, tn), BF16), pltpu.VMEM((d, tn), BF16),
                        pltpu.VMEM((2 * SUBLANES, tn), F32)],
        compiler_params=_params(("parallel", "arbitrary"), _vmem_limit(
            ((tm, d), BF16, 2), ((d, tn), F32, 4), ((d, tn), BF16, 2), ((tm, tn), BF16, 2),
            ((tm, tn), F32, 6))),
        name="ffn_up",
    )(h, w_gate, w_up, conv_w, conv_b)


def _rope_tables(positions):
    half = ROPE_DIM // 2
    inv = jnp.power(ROPE_THETA, -(jnp.arange(half, dtype=F32) * 2.0 / ROPE_DIM))
    ang = positions.astype(F32)[:, None] * inv
    cos, sin = jnp.cos(ang), jnp.sin(ang)
    s = positions.shape[0]
    rest = B_HEAD_DIM - ROPE_DIM
    reps = LANES // B_HEAD_DIM
    ca = jnp.tile(jnp.concatenate([cos, cos, jnp.ones((s, rest), F32)], axis=1), (1, reps))
    cb = jnp.tile(jnp.concatenate([-sin, jnp.zeros((s, B_HEAD_DIM - half), F32)], axis=1), (1, reps))
    cc = jnp.tile(jnp.concatenate([jnp.zeros((s, half), F32), sin, jnp.zeros((s, rest), F32)], axis=1),
                  (1, reps))
    return ca, cb, cc


def kernel(x, c, positions, w_ada, b_ada, g_pre_mix, g_post_mix, g_pre_ffn, g_post_ffn, w_in, conv_qk_w, conv_qk_b, b_igate, b_fgate, g_mlstm, lambdas, g_diff, w_a, w_b, w_out, w_gate, w_up, conv_ffn_w, conv_ffn_b, w_down):
    batch, s, d = x.shape
    depth = w_in.shape[0]
    assert batch == 1 and s % 1024 == 0 and d == M_V == B_V
    xs = x.reshape(s, d)
    tm_proj = min(1024, s)

    mod = _adaln(c.reshape(d, 1), w_ada, b_ada)
    ca, cb, cc = _rope_tables(positions[0])
    rope_specs = [pl.BlockSpec((tm_proj, LANES), lambda j, i: (i, 0))] * 3

    def gains(g):
        return g.reshape(depth, 1, g.shape[1])

    g_pre_mix, g_post_mix, g_pre_ffn, g_post_ffn = map(gains, (g_pre_mix, g_post_mix, g_pre_ffn, g_post_ffn))
    g_mlstm3, g_diff3 = gains(g_mlstm), gains(g_diff)
    conv_qk_b3, conv_ffn_b3 = gains(conv_qk_b), gains(conv_ffn_b)
    o_vm = 2 * M_QK
    o_om = o_vm + M_V
    o_gates = o_om + M_V
    o_tail = o_gates + 2 * M_HEADS
    w_tail = w_in[:, :, o_tail:]
    gate_bias = jnp.concatenate([b_igate, b_fgate, jnp.zeros((depth, LANES - 2 * M_HEADS), F32)],
                                axis=1).reshape(depth, 1, LANES)
    q_scale = jnp.concatenate([jnp.full((1, M_QK), M_QK_DIM ** -0.5, F32), jnp.ones((1, M_QK), F32)], axis=1)
    tn = 1024

    for l in range(depth):
        lam_init = 0.8 - 0.6 * math.exp(-0.3 * l)

        h = _prenorm(xs, g_pre_mix, mod, l, 1, 0)
        qk = _proj(functools.partial(_proj_conv_kernel, tm=tm_proj), h, w_in, l, 0, 2 * M_QK, BF16, tn=tn,
                   extra=(conv_qk_w, conv_qk_b3, q_scale),
                   extra_specs=[pl.BlockSpec((None, conv_qk_w.shape[1], tn), lambda j, i: (l, 0, j)),
                                pl.BlockSpec((None, 1, tn), lambda j, i: (l, 0, j)),
                                pl.BlockSpec((1, tn), lambda j, i: (0, j))],
                   scratch=[pltpu.VMEM((2 * SUBLANES, tn), F32)],
                   extra_vmem=(((tm_proj, tn), F32, 3),), name="proj_conv")
        v_so = _proj(functools.partial(_proj_mixed_kernel, n_plain=M_V // tn), h, w_in, l, o_vm, 2 * M_V,
                     BF16, tn=tn, name="proj_v_so")
        gates = _proj(_proj_plain_kernel, h, w_in, l, o_gates, LANES, F32, tn=LANES, name="proj_gates")
        qkb = _proj(functools.partial(_proj_rope_kernel, n_qtiles=B_QK // tn, qscale=B_HEAD_DIM ** -0.5 * LOG2E),
                    h, w_tail, l, 0, 2 * B_QK, BF16, tn=tn, extra=(ca, cb, cc), extra_specs=rope_specs,
                    extra_vmem=(((tm_proj, LANES), F32, 9),), name="proj_rope")
        v_sg = _proj(functools.partial(_proj_mixed_kernel, n_plain=B_V // tn), h, w_tail, l, 2 * B_QK,
                     B_V + 2 * d, BF16, tn=tn, name="proj_v_sg")

        y_a_in = _mlstm(qk, v_so, gates, gate_bias, g_mlstm3, l)
        y_b_in = _attention(qkb, v_sg, lambdas, g_diff3, l, lam_init)
        merged = _merge(y_a_in, y_b_in, w_a, w_b, v_sg, l)
        xs = _proj_res(merged, w_out[l].astype(BF16), xs, g_post_mix, mod, l, 2, 512, "out_proj")

        h = _prenorm(xs, g_pre_ffn, mod, l, 4, 3)
        act = _ffn_up(h, w_gate, w_up, conv_ffn_w, conv_ffn_b3, l)
        xs = _proj_res(act, w_down[l].astype(BF16), xs, g_post_ffn, mod, l, 5, 256, "ffn_down")

    return xs.reshape(batch, s, d)
```

```python
import functools
import math

import jax
import jax.numpy as jnp
from jax import lax
from jax.experimental import pallas as pl
from jax.experimental.pallas import tpu as pltpu

F32 = jnp.float32
BF16 = jnp.bfloat16

M_HEADS = 4
M_QK_DIM = 256
M_V_DIM = 512
M_QK = M_HEADS * M_QK_DIM
M_V = M_HEADS * M_V_DIM
B_HEADS = 16
B_HEAD_DIM = 64
B_V_DIM = 2 * B_HEAD_DIM
B_QK = B_HEADS * 2 * B_HEAD_DIM
B_V = B_HEADS * B_V_DIM
CHUNK = 64
ROPE_THETA = 500000.0
ROPE_DIM = B_HEAD_DIM // 4
EPS = 1e-6

LANES = 128
SUBLANES = 8
MXU_COLS = 256
V7X_VMEM_REQUEST_CAP = 60000 * 1024
V7X_COMPILER_SCRATCH = 8 * 1024 * 1024

MLSTM_L = 256
ATT_T = 512
ATT_VROWS = B_V_DIM + 16
LOG2E = math.log2(math.e)
NEG = -1e30


def _nbytes(shape, dtype):
    return math.prod(shape) * jnp.dtype(dtype).itemsize


def _vmem_limit(*terms):
    total = sum(_nbytes(s, d) * n for s, d, n in terms) + V7X_COMPILER_SCRATCH
    return int(min(V7X_VMEM_REQUEST_CAP, total))


def _params(sem, limit):
    return pltpu.CompilerParams(dimension_semantics=sem, vmem_limit_bytes=limit)


def _silu(x):
    return x * jax.nn.sigmoid(x)


def _dot(a, b):
    return jnp.dot(a, b, preferred_element_type=F32)


def _dot_nt(a, b):
    return lax.dot_general(a, b, (((1,), (1,)), ((), ())), preferred_element_type=F32)


def _dot_tn(a, b):
    return lax.dot_general(a, b, (((0,), (0,)), ((), ())), preferred_element_type=F32)


def _adaln_kernel(c_ref, w_ref, b_ref, o_ref):
    c = c_ref[...]
    o_ref[...] = jnp.sum(_silu(c) * w_ref[...], axis=0, keepdims=True) + b_ref[...]


def _adaln(c_col, w_ada, b_ada):
    depth, d, n = w_ada.shape
    tn = 1024
    return pl.pallas_call(
        _adaln_kernel,
        out_shape=jax.ShapeDtypeStruct((depth, 1, n), F32),
        grid=(depth, n // tn),
        in_specs=[pl.BlockSpec((d, 1), lambda l, j: (0, 0)),
                  pl.BlockSpec((None, d, tn), lambda l, j: (l, 0, j)),
                  pl.BlockSpec((None, 1, tn), lambda l, j: (l, 0, j))],
        out_specs=pl.BlockSpec((None, 1, tn), lambda l, j: (l, 0, j)),
        compiler_params=_params(("parallel", "parallel"),
                                _vmem_limit(((d, LANES), F32, 2), ((d, tn), F32, 3))),
        name="adaln",
    )(c_col, w_ada, b_ada.reshape(depth, 1, n))


def _modulated_norm(x, g, sc, sh):
    ms = jnp.mean(x * x, axis=-1, keepdims=True)
    return (x * lax.rsqrt(ms + EPS) * g) * (1.0 + sc) + sh


def _prenorm_kernel(x_ref, g_ref, sc_ref, sh_ref, o_ref):
    o_ref[...] = _modulated_norm(x_ref[...], g_ref[...], sc_ref[...], sh_ref[...]).astype(o_ref.dtype)


def _prenorm(x, gains, mod, l, sc_blk, sh_blk):
    s, d = x.shape
    tm = min(512, s)
    return pl.pallas_call(
        _prenorm_kernel,
        out_shape=jax.ShapeDtypeStruct((s, d), BF16),
        grid=(s // tm,),
        in_specs=[pl.BlockSpec((tm, d), lambda i: (i, 0)),
                  pl.BlockSpec((None, 1, d), lambda i: (l, 0, 0)),
                  pl.BlockSpec((None, 1, d), lambda i: (l, 0, sc_blk)),
                  pl.BlockSpec((None, 1, d), lambda i: (l, 0, sh_blk))],
        out_specs=pl.BlockSpec((tm, d), lambda i: (i, 0)),
        compiler_params=_params(("parallel",), _vmem_limit(((tm, d), F32, 4), ((tm, d), BF16, 2))),
        name="prenorm",
    )(x, gains, mod, mod)


def _first_row_tile():
    return pl.program_id(1) == 0


def _cast_weight(w_ref, wb_ref):
    @pl.when(_first_row_tile())
    def _():
        wb_ref[...] = w_ref[...].astype(BF16)


def _causal_conv(acc, buf_ref, cw_ref, cb_ref, cs, tm):
    kconv = cw_ref.shape[0]
    buf_ref[SUBLANES:2 * SUBLANES, cs] = acc[0:SUBLANES, :]
    out = cb_ref[:, cs] + cw_ref[kconv - 1:kconv, cs] * acc
    for j in range(kconv - 1):
        dist = kconv - 1 - j
        top = buf_ref[SUBLANES - dist:2 * SUBLANES - dist, cs]
        shifted = jnp.concatenate([top, pltpu.roll(acc, dist, 0)[SUBLANES:, :]], axis=0)
        out = out + cw_ref[j:j + 1, cs] * shifted
    buf_ref[0:SUBLANES, cs] = acc[tm - SUBLANES:tm, :]
    return out


def _col_subtiles(n):
    sub = min(MXU_COLS, n)
    return [slice(c * sub, (c + 1) * sub) for c in range(n // sub)]


def _proj_conv_kernel(h_ref, w_ref, cw_ref, cb_ref, o_ref, buf_ref, *, tm):
    @pl.when(_first_row_tile())
    def _():
        buf_ref[0:SUBLANES, :] = jnp.zeros((SUBLANES, buf_ref.shape[1]), F32)

    h = h_ref[...]
    for cs in _col_subtiles(o_ref.shape[1]):
        out = _causal_conv(_dot_nt(h, w_ref[cs, :]), buf_ref, cw_ref, cb_ref, cs, tm)
        o_ref[:, cs] = _silu(out).astype(o_ref.dtype)


def _proj_plain_kernel(h_ref, w_ref, o_ref):
    h = h_ref[...]
    for cs in _col_subtiles(o_ref.shape[1]):
        o_ref[:, cs] = _dot_nt(h, w_ref[cs, :]).astype(o_ref.dtype)


def _proj_mixed_kernel(h_ref, w_ref, o_ref, *, n_plain):
    j = pl.program_id(0)

    @pl.when(j < n_plain)
    def _():
        _proj_plain_kernel(h_ref, w_ref, o_ref)

    @pl.when(j >= n_plain)
    def _():
        h = h_ref[...]
        for cs in _col_subtiles(o_ref.shape[1]):
            o_ref[:, cs] = jax.nn.sigmoid(_dot_nt(h, w_ref[cs, :])).astype(o_ref.dtype)


def _proj_rope_kernel(h_ref, w_ref, ca_ref, cb_ref, cc_ref, o_ref, *, n_qtiles, qscale):
    scale = jnp.where(pl.program_id(0) < n_qtiles, qscale, 1.0).astype(F32)
    ca = ca_ref[...] * scale
    cb = cb_ref[...] * scale
    cc = cc_ref[...] * scale
    h = h_ref[...]
    for cs in _col_subtiles(o_ref.shape[1]):
        acc = _dot_nt(h, w_ref[cs, :])
        for s in range(acc.shape[1] // LANES):
            xs = acc[:, s * LANES:(s + 1) * LANES]
            r = (xs * ca + pltpu.roll(xs, LANES - ROPE_DIM // 2, 1) * cb
                 + pltpu.roll(xs, ROPE_DIM // 2, 1) * cc)
            o_ref[:, cs.start + s * LANES:cs.start + (s + 1) * LANES] = r.astype(o_ref.dtype)


def _proj(kernel, h, w, col0, n, out_dtype, *, tn, extra=(), extra_specs=(), scratch=(), extra_vmem=(),
          name):
    s, d = h.shape
    tm = min(1024, s)
    blk0 = col0 // tn
    assert blk0 * tn == col0 and n % tn == 0
    return pl.pallas_call(
        kernel,
        out_shape=jax.ShapeDtypeStruct((s, n), out_dtype),
        grid=(n // tn, s // tm),
        in_specs=[pl.BlockSpec((tm, d), lambda j, i: (i, 0)),
                  pl.BlockSpec((tn, d), lambda j, i: (blk0 + j, 0))] + list(extra_specs),
        out_specs=pl.BlockSpec((tm, tn), lambda j, i: (i, j)),
        scratch_shapes=list(scratch),
        compiler_params=_params(("parallel", "arbitrary"), _vmem_limit(
            ((tm, d), BF16, 2), ((d, tn), BF16, 2), ((tm, tn), out_dtype, 2),
            ((tm, MXU_COLS), F32, 8), *extra_vmem)),
        name=name,
    )(h, w, *extra)


def _mlstm_kernel(qk_ref, v_ref, gate_ref, bias_ref, so_ref, g_ref, o_ref, st_ref, m_ref):
    L = MLSTM_L

    @pl.when(pl.program_id(0) == 0)
    def _():
        st_ref[...] = jnp.zeros(st_ref.shape, F32)
        m_ref[...] = jnp.zeros(m_ref.shape, F32)

    ipre = gate_ref[...] + bias_ref[...]
    fpre = pltpu.roll(ipre, LANES - M_HEADS, 1)
    logf = jnp.minimum(fpre, 0.0) - jnp.log1p(jnp.exp(-jnp.abs(fpre)))

    row = lax.broadcasted_iota(jnp.int32, (L, L), 0)
    col = lax.broadcasted_iota(jnp.int32, (L, L), 1)
    tri = col <= row
    tri_b = jnp.where(tri, 1.0, 0.0).astype(BF16)
    hi = logf.astype(BF16)
    r1 = logf - hi.astype(F32)
    mid = r1.astype(BF16)
    lo = (r1 - mid.astype(F32)).astype(BF16)
    b = _dot(tri_b, hi) + _dot(tri_b, mid) + _dot(tri_b, lo)
    a = ipre - b
    a_t = a.T
    m_all = m_ref[...]
    lane = lax.broadcasted_iota(jnp.int32, (1, LANES), 1)
    ones_col = jnp.where(lax.broadcasted_iota(jnp.int32, (L, LANES), 1) == 0, 1.0, 0.0).astype(BF16)
    m_next = m_all

    for h in range(M_HEADS):
        a_row = a_t[h:h + 1, :]
        a_col = a[:, h:h + 1]
        b_col = b[:, h:h + 1]
        m_prev = m_all[:, h:h + 1]
        amask = jnp.where(tri, a_row, NEG)
        m_run = jnp.maximum(jnp.max(amask, axis=1, keepdims=True), m_prev)
        dmat = jnp.exp(amask - m_run)
        w_inter = jnp.exp(m_prev - m_run)
        q = qk_ref[:, h * M_QK_DIM:(h + 1) * M_QK_DIM] * jnp.asarray(M_QK_DIM ** -0.5, BF16)
        k = qk_ref[:, M_QK + h * M_QK_DIM:M_QK + (h + 1) * M_QK_DIM]
        v_ext = jnp.concatenate([v_ref[:, h * M_V_DIM:(h + 1) * M_V_DIM], ones_col], axis=1)
        sd = (_dot_nt(q, k) * dmat).astype(BF16)
        st = st_ref[h]
        tot = w_inter * _dot(q, st.astype(BF16)) + _dot(sd, v_ext)
        num = tot[:, :M_V_DIM]
        den = tot[:, M_V_DIM:M_V_DIM + 1]
        hval = num / jnp.maximum(jnp.abs(den), jnp.exp(-(b_col + m_run)))
        m_last = m_run[L - 1:L, :]
        w_s = jnp.exp(a_col - m_last)
        decay = jnp.exp(m_prev - m_last)
        rhs = (w_s * v_ext.astype(F32)).astype(BF16)
        st_ref[h] = decay * st + _dot_tn(k, rhs)
        m_next = jnp.where(lane == h, b_col[L - 1:L, :] + m_last, m_next)
        ms = jnp.mean(hval * hval, axis=-1, keepdims=True)
        sl = slice(h * M_V_DIM, (h + 1) * M_V_DIM)
        hn = hval * lax.rsqrt(ms + EPS) * g_ref[:, sl]
        o_ref[:, sl] = (so_ref[:, sl].astype(F32) * hn).astype(o_ref.dtype)

    m_ref[...] = m_next


def _mlstm(qk, v_so, gates, gate_bias, g_mlstm, l):
    s = qk.shape[0]
    L = MLSTM_L
    st_shape = (M_HEADS, M_QK_DIM, M_V_DIM + LANES)
    return pl.pallas_call(
        _mlstm_kernel,
        out_shape=jax.ShapeDtypeStruct((s, M_V), BF16),
        grid=(s // L,),
        in_specs=[pl.BlockSpec((L, 2 * M_QK), lambda c: (c, 0)),
                  pl.BlockSpec((L, M_V), lambda c: (c, 0)),
                  pl.BlockSpec((L, LANES), lambda c: (c, 0)),
                  pl.BlockSpec((None, 1, LANES), lambda c: (l, 0, 0)),
                  pl.BlockSpec((L, M_V), lambda c: (c, 1)),
                  pl.BlockSpec((None, 1, M_V), lambda c: (l, 0, 0))],
        out_specs=pl.BlockSpec((L, M_V), lambda c: (c, 0)),
        scratch_shapes=[pltpu.VMEM(st_shape, F32), pltpu.VMEM((1, LANES), F32)],
        compiler_params=_params(("arbitrary",), _vmem_limit(
            ((L, 2 * M_QK), BF16, 2), ((L, M_V), BF16, 6), ((L, LANES), F32, 2),
            (st_shape, F32, 2), ((L, M_V + LANES), F32, 8), ((L, L), F32, 8))),
        name="mlstm",
    )(qk, v_so, gates, gate_bias, v_so, g_mlstm)


def _attn_kernel(q_ref, k_ref, v_ref, lam_ref, g_ref, o_ref, vt_ref, acc_ref, m_ref, sa_ref, sb_ref,
                 mxa_ref, mxb_ref, *, lam_init, seq):
    t = ATT_T
    qi = pl.program_id(1)

    @pl.when(qi == 0)
    def _():
        ones_row = jnp.where(lax.broadcasted_iota(jnp.int32, (ATT_VROWS - B_V_DIM, t), 0) == 0, 1.0, 0.0)
        for kb in range(seq // t):
            v_t = v_ref[kb * t:(kb + 1) * t, :].astype(F32).T
            vt_ref[kb] = jnp.concatenate([v_t, ones_row], axis=0).astype(BF16)

    def query_block(qb):
        q = q_ref[pl.ds(pl.multiple_of(qb * t, t), t), :]
        qlane = lax.broadcasted_iota(jnp.int32, q.shape, 1)
        zero = jnp.zeros_like(q)
        return jnp.concatenate([jnp.where(qlane < B_HEAD_DIM, q, zero),
                                jnp.where(qlane >= B_HEAD_DIM, q, zero)], axis=0)

    qcat = query_block(qi)
    shift = CHUNK.bit_length() - 1
    kchunk = lax.broadcasted_iota(jnp.int32, (t, 2 * t), 0) >> shift
    qchunk = (lax.broadcasted_iota(jnp.int32, (t, 2 * t), 1) & (t - 1)) >> shift
    diag_mask = kchunk <= qchunk

    def scores(kb, s_ref, mx_ref, mask=None, qc=qcat):
        kblk = k_ref[pl.ds(pl.multiple_of(kb * t, t), t), :]
        s_t = _dot_nt(kblk, qc)
        if mask is not None:
            s_t = jnp.where(mask, s_t, NEG)
        s_ref[...] = s_t
        mx_ref[...] = jnp.max(s_t, axis=0, keepdims=True)

    def absorb(kb, s_t, mx):
        m_old = m_ref[...]
        m_new = jnp.maximum(m_old, mx)
        p = jnp.exp2(s_t - m_new).astype(BF16)
        acc_ref[...] = jnp.exp2(m_old - m_new) * acc_ref[...] + _dot(vt_ref[kb], p)
        m_ref[...] = m_new

    acc_ref[...] = jnp.zeros(acc_ref.shape, F32)
    m_ref[...] = jnp.full(m_ref.shape, NEG, F32)

    @pl.when(qi == 0)
    def _():
        scores(0, sa_ref, mxa_ref)

    def body(it, carry):
        scores(2 * it + 1, sb_ref, mxb_ref)
        absorb(2 * it, sa_ref[...], mxa_ref[...])
        scores(2 * it + 2, sa_ref, mxa_ref)
        absorb(2 * it + 1, sb_ref[...], mxb_ref[...])
        return carry

    lax.fori_loop(0, qi >> 1, body, 0)
    odd = (qi & 1) == 1

    @pl.when(odd)
    def _():
        scores(qi, sb_ref, mxb_ref, diag_mask)
        absorb(qi - 1, sa_ref[...], mxa_ref[...])
        absorb(qi, sb_ref[...], mxb_ref[...])

    @pl.when(jnp.logical_not(odd))
    def _():
        s_d = jnp.where(diag_mask, sa_ref[...], NEG)
        absorb(qi, s_d, jnp.max(s_d, axis=0, keepdims=True))

    lf = lam_ref[...]
    lam = (jnp.exp(jnp.sum(lf[0:1] * lf[1:2], axis=1, keepdims=True))
           - jnp.exp(jnp.sum(lf[2:3] * lf[3:4], axis=1, keepdims=True)) + lam_init)
    acc = acc_ref[...]
    o1 = acc[:B_V_DIM, :t] / acc[B_V_DIM:B_V_DIM + 1, :t]
    o2 = acc[:B_V_DIM, t:] / acc[B_V_DIM:B_V_DIM + 1, t:]
    o_t = o1 - lam * o2
    ms = jnp.mean(o_t * o_t, axis=0, keepdims=True)
    o = (o_t * lax.rsqrt(ms + EPS)).T
    o_ref[...] = (o * (g_ref[...] * (1.0 - lam_init))).astype(o_ref.dtype)

    q_next = jnp.minimum(qi + 1, pl.num_programs(1) - 1)
    scores(0, sa_ref, mxa_ref, qc=query_block(q_next))


def _attention(qkb, v_sg, lambdas, g_diff, l, lam_init):
    s = qkb.shape[0]
    t = ATT_T
    kern = functools.partial(_attn_kernel, lam_init=lam_init, seq=s)
    return pl.pallas_call(
        kern,
        out_shape=jax.ShapeDtypeStruct((s, B_V), BF16),
        grid=(B_HEADS, s // t),
        in_specs=[pl.BlockSpec((s, LANES), lambda h, i: (0, h)),
                  pl.BlockSpec((s, LANES), lambda h, i: (0, B_HEADS + h)),
                  pl.BlockSpec((s, LANES), lambda h, i: (0, h)),
                  pl.BlockSpec((None, 4, B_HEAD_DIM), lambda h, i: (l, 0, 0)),
                  pl.BlockSpec((None, 1, LANES), lambda h, i: (l, 0, h))],
        out_specs=pl.BlockSpec((t, LANES), lambda h, i: (i, h)),
        scratch_shapes=[pltpu.VMEM((s // t, ATT_VROWS, t), BF16),
                        pltpu.VMEM((ATT_VROWS, 2 * t), F32),
                        pltpu.VMEM((1, 2 * t), F32),
                        pltpu.VMEM((t, 2 * t), F32),
                        pltpu.VMEM((t, 2 * t), F32),
                        pltpu.VMEM((1, 2 * t), F32),
                        pltpu.VMEM((1, 2 * t), F32)],
        compiler_params=_params(("arbitrary", "arbitrary"), _vmem_limit(
            ((s, LANES), BF16, 6), ((s // t, ATT_VROWS, t), BF16, 1), ((t, LANES), BF16, 6),
            ((t, 2 * t), F32, 8), ((ATT_VROWS, 2 * t), F32, 4))),
        name="diff_attn",
    )(qkb, qkb, v_sg, lambdas, g_diff)


def _merge_kernel(a_ref, b_ref, wa_ref, wb_ref, ga_ref, gb_ref, o_ref, wab_ref, wbb_ref):
    _cast_weight(wa_ref, wab_ref)
    _cast_weight(wb_ref, wbb_ref)
    a = a_ref[...]
    b = b_ref[...]
    for cs in _col_subtiles(o_ref.shape[1]):
        ya = _dot(a, wab_ref[:, cs])
        yb = _dot(b, wbb_ref[:, cs])
        o_ref[:, cs] = (ga_ref[:, cs].astype(F32) * ya + gb_ref[:, cs].astype(F32) * yb).astype(o_ref.dtype)


def _merge(ya_in, yb_in, w_a, w_b, v_sg, l):
    s, d = ya_in.shape
    n = w_a.shape[2]
    tm, tn = min(512, s), 512
    ga0 = B_V // tn
    gb0 = ga0 + n // tn
    return pl.pallas_call(
        _merge_kernel,
        out_shape=jax.ShapeDtypeStruct((s, n), BF16),
        grid=(n // tn, s // tm),
        in_specs=[pl.BlockSpec((tm, d), lambda j, i: (i, 0)),
                  pl.BlockSpec((tm, d), lambda j, i: (i, 0)),
                  pl.BlockSpec((None, d, tn), lambda j, i: (l, 0, j)),
                  pl.BlockSpec((None, d, tn), lambda j, i: (l, 0, j)),
                  pl.BlockSpec((tm, tn), lambda j, i: (i, ga0 + j)),
                  pl.BlockSpec((tm, tn), lambda j, i: (i, gb0 + j))],
        out_specs=pl.BlockSpec((tm, tn), lambda j, i: (i, j)),
        scratch_shapes=[pltpu.VMEM((d, tn), BF16), pltpu.VMEM((d, tn), BF16)],
        compiler_params=_params(("parallel", "arbitrary"), _vmem_limit(
            ((tm, d), BF16, 4), ((d, tn), F32, 4), ((d, tn), BF16, 2), ((tm, tn), BF16, 6),
            ((tm, tn), F32, 4))),
        name="merge",
    )(ya_in, yb_in, w_a, w_b, v_sg, v_sg)


def _proj_res_kernel(a_ref, w_ref, x_ref, g_ref, gt_ref, *rest, sub, with_next):
    if with_next:
        gn_ref, sc_ref, sh_ref, o_ref, hn_ref = rest
    else:
        (o_ref,) = rest
    for r in range(a_ref.shape[0] // sub):
        rs = slice(r * sub, (r + 1) * sub)
        y = _dot(a_ref[rs, :], w_ref[...])
        ms = jnp.mean(y * y, axis=-1, keepdims=True)
        x_new = x_ref[rs, :] + gt_ref[...] * (y * lax.rsqrt(ms + EPS) * g_ref[...])
        o_ref[rs, :] = x_new
        if with_next:
            hn_ref[rs, :] = _modulated_norm(x_new, gn_ref[...], sc_ref[...], sh_ref[...]).astype(hn_ref.dtype)


def _proj_res(a, w, x, gains, mod, l, gt_blk, tm, name, nxt=None):
    s, k = a.shape
    d = w.shape[1]
    tm = min(tm, s)
    vec = lambda arr, lay, blk: (arr, pl.BlockSpec((None, 1, d), lambda i: (lay, 0, blk)))
    operands = [(a, pl.BlockSpec((tm, k), lambda i: (i, 0))),
                (w, pl.BlockSpec((k, d), lambda i: (0, 0), pipeline_mode=pl.Buffered(1))),
                (x, pl.BlockSpec((tm, d), lambda i: (i, 0))),
                vec(gains, l, 0), vec(mod, l, gt_blk)]
    out_shape = [jax.ShapeDtypeStruct((s, d), F32)]
    out_specs = [pl.BlockSpec((tm, d), lambda i: (i, 0))]
    if nxt is not None:
        n_gains, n_l, n_sc, n_sh = nxt
        operands += [vec(n_gains, n_l, 0), vec(mod, n_l, n_sc), vec(mod, n_l, n_sh)]
        out_shape.append(jax.ShapeDtypeStruct((s, d), BF16))
        out_specs.append(pl.BlockSpec((tm, d), lambda i: (i, 0)))
    kern = functools.partial(_proj_res_kernel, sub=min(tm, MXU_COLS), with_next=nxt is not None)
    outs = pl.pallas_call(
        kern,
        out_shape=out_shape,
        grid=(s // tm,),
        in_specs=[spec for _, spec in operands],
        out_specs=out_specs,
        compiler_params=_params(("parallel",), _vmem_limit(
            ((tm, k), BF16, 2), ((k, d), BF16, 1), ((tm, d), F32, 4), ((tm, d), BF16, 2),
            ((MXU_COLS, d), F32, 6))),
        name=name,
    )(*[arr for arr, _ in operands])
    return outs if nxt is not None else (outs[0], None)


def _ffn_up_kernel(h_ref, wg_ref, wu_ref, cw_ref, cb_ref, o_ref, wgb_ref, wub_ref, buf_ref, *, tm):
    _cast_weight(wg_ref, wgb_ref)
    _cast_weight(wu_ref, wub_ref)

    @pl.when(_first_row_tile())
    def _():
        buf_ref[0:SUBLANES, :] = jnp.zeros((SUBLANES, buf_ref.shape[1]), F32)

    h = h_ref[...]
    for cs in _col_subtiles(o_ref.shape[1]):
        g = _causal_conv(_dot(h, wgb_ref[:, cs]), buf_ref, cw_ref, cb_ref, cs, tm)
        o_ref[:, cs] = (_silu(g) * _dot(h, wub_ref[:, cs])).astype(o_ref.dtype)


def _ffn_up(h, w_gate, w_up, conv_w, conv_b, l):
    s, d = h.shape
    n = w_gate.shape[2]
    tm, tn = min(1024, s), 512
    kconv = conv_w.shape[1]
    kern = functools.partial(_ffn_up_kernel, tm=tm)
    return pl.pallas_call(
        kern,
        out_shape=jax.ShapeDtypeStruct((s, n), BF16),
        grid=(n // tn, s // tm),
        in_specs=[pl.BlockSpec((tm, d), lambda j, i: (i, 0)),
                  pl.BlockSpec((None, d, tn), lambda j, i: (l, 0, j)),
                  pl.BlockSpec((None, d, tn), lambda j, i: (l, 0, j)),
                  pl.BlockSpec((None, kconv, tn), lambda j, i: (l, 0, j)),
                  pl.BlockSpec((None, 1, tn), lambda j, i: (l, 0, j))],
        out_specs=pl.BlockSpec((tm, tn), lambda j, i: (i, j)),
        scratch_shapes=[pltpu.VMEM((d, tn), BF16), pltpu.VMEM((d, tn), BF16),
                        pltpu.VMEM((2 * SUBLANES, tn), F32)],
        compiler_params=_params(("parallel", "arbitrary"), _vmem_limit(
            ((tm, d), BF16, 2), ((d, tn), F32, 4), ((d, tn), BF16, 2), ((tm, tn), BF16, 2),
            ((tm, tn), F32, 6))),
        name="ffn_up",
    )(h, w_gate, w_up, conv_w, conv_b)


def _rope_tables(positions):
    half = ROPE_DIM // 2
    inv = jnp.power(ROPE_THETA, -(jnp.arange(half, dtype=F32) * 2.0 / ROPE_DIM))
    ang = positions.astype(F32)[:, None] * inv
    cos, sin = jnp.cos(ang), jnp.sin(ang)
    s = positions.shape[0]
    rest = B_HEAD_DIM - ROPE_DIM
    reps = LANES // B_HEAD_DIM
    ca = jnp.tile(jnp.concatenate([cos, cos, jnp.ones((s, rest), F32)], axis=1), (1, reps))
    cb = jnp.tile(jnp.concatenate([-sin, jnp.zeros((s, B_HEAD_DIM - half), F32)], axis=1), (1, reps))
    cc = jnp.tile(jnp.concatenate([jnp.zeros((s, half), F32), sin, jnp.zeros((s, rest), F32)], axis=1),
                  (1, reps))
    return ca, cb, cc


def kernel(x, c, positions, w_ada, b_ada, g_pre_mix, g_post_mix, g_pre_ffn, g_post_ffn, w_in, conv_qk_w, conv_qk_b, b_igate, b_fgate, g_mlstm, lambdas, g_diff, w_a, w_b, w_out, w_gate, w_up, conv_ffn_w, conv_ffn_b, w_down):
    batch, s, d = x.shape
    depth = w_in.shape[0]
    assert batch == 1 and s % 1024 == 0 and d == M_V == B_V
    xs = x.reshape(s, d)
    tm_proj = min(1024, s)

    mod = _adaln(c.reshape(d, 1), w_ada, b_ada)
    ca, cb, cc = _rope_tables(positions[0])
    rope_specs = [pl.BlockSpec((tm_proj, LANES), lambda j, i: (i, 0))] * 3

    def gains(g):
        return g.reshape(depth, 1, g.shape[1])

    g_pre_mix, g_post_mix, g_pre_ffn, g_post_ffn = map(gains, (g_pre_mix, g_post_mix, g_pre_ffn, g_post_ffn))
    g_mlstm3, g_diff3 = gains(g_mlstm), gains(g_diff)
    conv_qk_b3, conv_ffn_b3 = gains(conv_qk_b), gains(conv_ffn_b)
    o_vm = 2 * M_QK
    o_om = o_vm + M_V
    o_gates = o_om + M_V
    o_tail = o_gates + 2 * M_HEADS
    gate_bias = jnp.concatenate([b_igate, b_fgate, jnp.zeros((depth, LANES - 2 * M_HEADS), F32)],
                                axis=1).reshape(depth, 1, LANES)
    assert math.log2(M_QK_DIM) % 2 == 0
    w_in_t = jnp.swapaxes(w_in, 1, 2)
    tn, tw = 1024, 2048

    h = _prenorm(xs, g_pre_mix, mod, 0, 1, 0)
    for l in range(depth):
        lam_init = 0.8 - 0.6 * math.exp(-0.3 * l)
        w_head = w_in_t[l, :o_gates + LANES, :].astype(BF16)
        w_tail = w_in_t[l, o_tail:, :].astype(BF16)

        qk = _proj(functools.partial(_proj_conv_kernel, tm=tm_proj), h, w_head, 0, 2 * M_QK, BF16, tn=tn,
                   extra=(conv_qk_w, conv_qk_b3),
                   extra_specs=[pl.BlockSpec((None, conv_qk_w.shape[1], tn), lambda j, i: (l, 0, j)),
                                pl.BlockSpec((None, 1, tn), lambda j, i: (l, 0, j))],
                   scratch=[pltpu.VMEM((2 * SUBLANES, tn), F32)],
                   extra_vmem=(((tm_proj, tn), F32, 3),), name="proj_conv")
        v_so = _proj(functools.partial(_proj_mixed_kernel, n_plain=M_V // tw), h, w_head, o_vm, 2 * M_V,
                     BF16, tn=tw, name="proj_v_so")
        gates = _proj(_proj_plain_kernel, h, w_head, o_gates, LANES, F32, tn=LANES, name="proj_gates")
        qkb = _proj(functools.partial(_proj_rope_kernel, n_qtiles=B_QK // tw, qscale=B_HEAD_DIM ** -0.5 * LOG2E),
                    h, w_tail, 0, 2 * B_QK, BF16, tn=tw, extra=(ca, cb, cc), extra_specs=rope_specs,
                    extra_vmem=(((tm_proj, LANES), F32, 9),), name="proj_rope")
        v_sg = _proj(functools.partial(_proj_mixed_kernel, n_plain=B_V // tw), h, w_tail, 2 * B_QK,
                     B_V + 2 * d, BF16, tn=tw, name="proj_v_sg")

        y_a_in = _mlstm(qk, v_so, gates, gate_bias, g_mlstm3, l)
        y_b_in = _attention(qkb, v_sg, lambdas, g_diff3, l, lam_init)
        merged = _merge(y_a_in, y_b_in, w_a, w_b, v_sg, l)
        xs, h = _proj_res(merged, w_out[l].astype(BF16), xs, g_post_mix, mod, l, 2, 512, "out_proj",
                          nxt=(g_pre_ffn, l, 4, 3))

        act = _ffn_up(h, w_gate, w_up, conv_ffn_w, conv_ffn_b3, l)
        nxt = (g_pre_mix, l + 1, 1, 0) if l + 1 < depth else None
        xs, h = _proj_res(act, w_down[l].astype(BF16), xs, g_post_ffn, mod, l, 5, 256, "ffn_down", nxt=nxt)

    return xs.reshape(batch, s, d)
```

```python
import functools
import math

import jax
import jax.numpy as jnp
from jax import lax
from jax.experimental import pallas as pl
from jax.experimental.pallas import tpu as pltpu

F32 = jnp.float32
BF16 = jnp.bfloat16

M_HEADS = 4
M_QK_DIM = 256
M_V_DIM = 512
M_QK = M_HEADS * M_QK_DIM
M_V = M_HEADS * M_V_DIM
B_HEADS = 16
B_HEAD_DIM = 64
B_V_DIM = 2 * B_HEAD_DIM
B_QK = B_HEADS * 2 * B_HEAD_DIM
B_V = B_HEADS * B_V_DIM
CHUNK = 64
ROPE_THETA = 500000.0
ROPE_DIM = B_HEAD_DIM // 4
EPS = 1e-6

LANES = 128
SUBLANES = 8
MXU_COLS = 256
V7X_VMEM_REQUEST_CAP = 60000 * 1024
V7X_COMPILER_SCRATCH = 8 * 1024 * 1024

MLSTM_L = 256
ATT_T = 512
ATT_VROWS = B_V_DIM + 16
LOG2E = math.log2(math.e)
NEG = -1e30


def _nbytes(shape, dtype):
    return math.prod(shape) * jnp.dtype(dtype).itemsize


def _vmem_limit(*terms):
    total = sum(_nbytes(s, d) * n for s, d, n in terms) + V7X_COMPILER_SCRATCH
    return int(min(V7X_VMEM_REQUEST_CAP, total))


def _params(sem, limit):
    return pltpu.CompilerParams(dimension_semantics=sem, vmem_limit_bytes=limit)


def _silu(x):
    return x * jax.nn.sigmoid(x)


def _dot(a, b):
    return jnp.dot(a, b, preferred_element_type=F32)


def _dot_nt(a, b):
    return lax.dot_general(a, b, (((1,), (1,)), ((), ())), preferred_element_type=F32)


def _dot_tn(a, b):
    return lax.dot_general(a, b, (((0,), (0,)), ((), ())), preferred_element_type=F32)


def _adaln_kernel(c_ref, w_ref, b_ref, o_ref):
    c = c_ref[...]
    o_ref[...] = jnp.sum(_silu(c) * w_ref[...], axis=0, keepdims=True) + b_ref[...]


def _adaln(c_col, w_ada, b_ada):
    depth, d, n = w_ada.shape
    tn = 1024
    return pl.pallas_call(
        _adaln_kernel,
        out_shape=jax.ShapeDtypeStruct((depth, 1, n), F32),
        grid=(depth, n // tn),
        in_specs=[pl.BlockSpec((d, 1), lambda l, j: (0, 0)),
                  pl.BlockSpec((None, d, tn), lambda l, j: (l, 0, j)),
                  pl.BlockSpec((None, 1, tn), lambda l, j: (l, 0, j))],
        out_specs=pl.BlockSpec((None, 1, tn), lambda l, j: (l, 0, j)),
        compiler_params=_params(("parallel", "parallel"),
                                _vmem_limit(((d, LANES), F32, 2), ((d, tn), F32, 3))),
        name="adaln",
    )(c_col, w_ada, b_ada.reshape(depth, 1, n))


def _modulated_norm(x, g, sc, sh):
    ms = jnp.mean(x * x, axis=-1, keepdims=True)
    return (x * lax.rsqrt(ms + EPS) * g) * (1.0 + sc) + sh


def _prenorm_kernel(x_ref, g_ref, sc_ref, sh_ref, o_ref):
    o_ref[...] = _modulated_norm(x_ref[...], g_ref[...], sc_ref[...], sh_ref[...]).astype(o_ref.dtype)


def _prenorm(x, gains, mod, l, sc_blk, sh_blk):
    s, d = x.shape
    tm = min(512, s)
    return pl.pallas_call(
        _prenorm_kernel,
        out_shape=jax.ShapeDtypeStruct((s, d), BF16),
        grid=(s // tm,),
        in_specs=[pl.BlockSpec((tm, d), lambda i: (i, 0)),
                  pl.BlockSpec((None, 1, d), lambda i: (l, 0, 0)),
                  pl.BlockSpec((None, 1, d), lambda i: (l, 0, sc_blk)),
                  pl.BlockSpec((None, 1, d), lambda i: (l, 0, sh_blk))],
        out_specs=pl.BlockSpec((tm, d), lambda i: (i, 0)),
        compiler_params=_params(("parallel",), _vmem_limit(((tm, d), F32, 4), ((tm, d), BF16, 2))),
        name="prenorm",
    )(x, gains, mod, mod)


REBASE_ROWS = 512


def _rebase_kernel(cur_ref, nxt_ref, o_ref):
    skip = nxt_ref.shape[0]
    o_ref[...] = jnp.concatenate([cur_ref[skip:, :], nxt_ref[...]], axis=0).astype(o_ref.dtype)


def _rebase_rows(w_t, row0):
    depth, rows, d = w_t.shape
    r = REBASE_ROWS
    base = row0 - SUBLANES
    n = rows - row0
    assert base % r == 0 and n % r == 0
    blk0, per = base // r, r // SUBLANES
    return pl.pallas_call(
        _rebase_kernel,
        out_shape=jax.ShapeDtypeStruct((depth, n, d), BF16),
        grid=(depth, n // r),
        in_specs=[pl.BlockSpec((None, r, d), lambda l, j: (l, blk0 + j, 0)),
                  pl.BlockSpec((None, SUBLANES, d), lambda l, j: (l, (blk0 + j + 1) * per, 0))],
        out_specs=pl.BlockSpec((None, r, d), lambda l, j: (l, j, 0)),
        compiler_params=_params(("parallel", "parallel"), _vmem_limit(
            ((r, d), F32, 3), ((SUBLANES, d), F32, 2), ((r, d), BF16, 3))),
        name="rebase_w_in",
    )(w_t, w_t)


def _first_row_tile():
    return pl.program_id(1) == 0


def _cast_weight(w_ref, wb_ref):
    @pl.when(_first_row_tile())
    def _():
        wb_ref[...] = w_ref[...].astype(BF16)


def _causal_conv(acc, buf_ref, cw_ref, cb_ref, cs, tm):
    kconv = cw_ref.shape[0]
    buf_ref[SUBLANES:2 * SUBLANES, cs] = acc[0:SUBLANES, :]
    out = cb_ref[:, cs] + cw_ref[kconv - 1:kconv, cs] * acc
    for j in range(kconv - 1):
        dist = kconv - 1 - j
        top = buf_ref[SUBLANES - dist:2 * SUBLANES - dist, cs]
        shifted = jnp.concatenate([top, pltpu.roll(acc, dist, 0)[SUBLANES:, :]], axis=0)
        out = out + cw_ref[j:j + 1, cs] * shifted
    buf_ref[0:SUBLANES, cs] = acc[tm - SUBLANES:tm, :]
    return out


def _col_subtiles(n):
    sub = min(MXU_COLS, n)
    return [slice(c * sub, (c + 1) * sub) for c in range(n // sub)]


def _proj_conv_kernel(h_ref, w_ref, cw_ref, cb_ref, o_ref, buf_ref, *, tm):
    @pl.when(_first_row_tile())
    def _():
        buf_ref[0:SUBLANES, :] = jnp.zeros((SUBLANES, buf_ref.shape[1]), F32)

    h = h_ref[...]
    for cs in _col_subtiles(o_ref.shape[1]):
        out = _causal_conv(_dot_nt(h, w_ref[cs, :]), buf_ref, cw_ref, cb_ref, cs, tm)
        o_ref[:, cs] = _silu(out).astype(o_ref.dtype)


def _proj_plain_kernel(h_ref, w_ref, o_ref):
    h = h_ref[...]
    for cs in _col_subtiles(o_ref.shape[1]):
        o_ref[:, cs] = _dot_nt(h, w_ref[cs, :]).astype(o_ref.dtype)


def _proj_mixed_kernel(h_ref, w_ref, o_ref, *, n_plain):
    j = pl.program_id(0)

    @pl.when(j < n_plain)
    def _():
        _proj_plain_kernel(h_ref, w_ref, o_ref)

    @pl.when(j >= n_plain)
    def _():
        h = h_ref[...]
        for cs in _col_subtiles(o_ref.shape[1]):
            o_ref[:, cs] = jax.nn.sigmoid(_dot_nt(h, w_ref[cs, :])).astype(o_ref.dtype)


def _proj_rope_kernel(h_ref, w_ref, ca_ref, cb_ref, cc_ref, o_ref, *, n_qtiles, qscale):
    scale = jnp.where(pl.program_id(0) < n_qtiles, qscale, 1.0).astype(F32)
    ca = ca_ref[...] * scale
    cb = cb_ref[...] * scale
    cc = cc_ref[...] * scale
    h = h_ref[...]
    for cs in _col_subtiles(o_ref.shape[1]):
        acc = _dot_nt(h, w_ref[cs, :])
        for s in range(acc.shape[1] // LANES):
            xs = acc[:, s * LANES:(s + 1) * LANES]
            r = (xs * ca + pltpu.roll(xs, LANES - ROPE_DIM // 2, 1) * cb
                 + pltpu.roll(xs, ROPE_DIM // 2, 1) * cc)
            o_ref[:, cs.start + s * LANES:cs.start + (s + 1) * LANES] = r.astype(o_ref.dtype)


def _proj(kernel, h, w, l, col0, n, out_dtype, *, tn, extra=(), extra_specs=(), scratch=(), extra_vmem=(),
          name):
    s, d = h.shape
    tm = min(1024, s)
    blk0 = col0 // tn
    assert blk0 * tn == col0 and n % tn == 0
    return pl.pallas_call(
        kernel,
        out_shape=jax.ShapeDtypeStruct((s, n), out_dtype),
        grid=(n // tn, s // tm),
        in_specs=[pl.BlockSpec((tm, d), lambda j, i: (i, 0)),
                  pl.BlockSpec((None, tn, d), lambda j, i: (l, blk0 + j, 0))] + list(extra_specs),
        out_specs=pl.BlockSpec((tm, tn), lambda j, i: (i, j)),
        scratch_shapes=list(scratch),
        compiler_params=_params(("parallel", "arbitrary"), _vmem_limit(
            ((tm, d), BF16, 2), ((d, tn), BF16, 2), ((tm, tn), out_dtype, 2),
            ((tm, MXU_COLS), F32, 8), *extra_vmem)),
        name=name,
    )(h, w, *extra)


def _mlstm_kernel(qk_ref, v_ref, gate_ref, bias_ref, so_ref, g_ref, o_ref, st_ref, m_ref):
    L = MLSTM_L

    @pl.when(pl.program_id(0) == 0)
    def _():
        st_ref[...] = jnp.zeros(st_ref.shape, F32)
        m_ref[...] = jnp.zeros(m_ref.shape, F32)

    ipre = gate_ref[...] + bias_ref[...]
    fpre = pltpu.roll(ipre, LANES - M_HEADS, 1)
    logf = jnp.minimum(fpre, 0.0) - jnp.log1p(jnp.exp(-jnp.abs(fpre)))

    row = lax.broadcasted_iota(jnp.int32, (L, L), 0)
    col = lax.broadcasted_iota(jnp.int32, (L, L), 1)
    tri = col <= row
    tri_b = jnp.where(tri, 1.0, 0.0).astype(BF16)
    hi = logf.astype(BF16)
    r1 = logf - hi.astype(F32)
    mid = r1.astype(BF16)
    lo = (r1 - mid.astype(F32)).astype(BF16)
    b = _dot(tri_b, hi) + _dot(tri_b, mid) + _dot(tri_b, lo)
    a = ipre - b
    a_t = a.T
    m_all = m_ref[...]
    lane = lax.broadcasted_iota(jnp.int32, (1, LANES), 1)
    ones_col = jnp.where(lax.broadcasted_iota(jnp.int32, (L, LANES), 1) == 0, 1.0, 0.0).astype(BF16)
    m_next = m_all

    for h in range(M_HEADS):
        a_row = a_t[h:h + 1, :]
        a_col = a[:, h:h + 1]
        b_col = b[:, h:h + 1]
        m_prev = m_all[:, h:h + 1]
        amask = jnp.where(tri, a_row, NEG)
        m_run = jnp.maximum(jnp.max(amask, axis=1, keepdims=True), m_prev)
        dmat = jnp.exp(amask - m_run)
        w_inter = jnp.exp(m_prev - m_run)
        q = qk_ref[:, h * M_QK_DIM:(h + 1) * M_QK_DIM] * jnp.asarray(M_QK_DIM ** -0.5, BF16)
        k = qk_ref[:, M_QK + h * M_QK_DIM:M_QK + (h + 1) * M_QK_DIM]
        v_ext = jnp.concatenate([v_ref[:, h * M_V_DIM:(h + 1) * M_V_DIM], ones_col], axis=1)
        sd = (_dot_nt(q, k) * dmat).astype(BF16)
        st = st_ref[h]
        tot = w_inter * _dot(q, st.astype(BF16)) + _dot(sd, v_ext)
        num = tot[:, :M_V_DIM]
        den = tot[:, M_V_DIM:M_V_DIM + 1]
        hval = num / jnp.maximum(jnp.abs(den), jnp.exp(-(b_col + m_run)))
        m_last = m_run[L - 1:L, :]
        w_s = jnp.exp(a_col - m_last)
        decay = jnp.exp(m_prev - m_last)
        rhs = (w_s * v_ext.astype(F32)).astype(BF16)
        st_ref[h] = decay * st + _dot_tn(k, rhs)
        m_next = jnp.where(lane == h, b_col[L - 1:L, :] + m_last, m_next)
        ms = jnp.mean(hval * hval, axis=-1, keepdims=True)
        sl = slice(h * M_V_DIM, (h + 1) * M_V_DIM)
        hn = hval * lax.rsqrt(ms + EPS) * g_ref[:, sl]
        o_ref[:, sl] = (so_ref[:, sl].astype(F32) * hn).astype(o_ref.dtype)

    m_ref[...] = m_next


def _mlstm(qk, v_so, gates, gate_bias, g_mlstm, l):
    s = qk.shape[0]
    L = MLSTM_L
    st_shape = (M_HEADS, M_QK_DIM, M_V_DIM + LANES)
    return pl.pallas_call(
        _mlstm_kernel,
        out_shape=jax.ShapeDtypeStruct((s, M_V), BF16),
        grid=(s // L,),
        in_specs=[pl.BlockSpec((L, 2 * M_QK), lambda c: (c, 0)),
                  pl.BlockSpec((L, M_V), lambda c: (c, 0)),
                  pl.BlockSpec((L, LANES), lambda c: (c, 0)),
                  pl.BlockSpec((None, 1, LANES), lambda c: (l, 0, 0)),
                  pl.BlockSpec((L, M_V), lambda c: (c, 1)),
                  pl.BlockSpec((None, 1, M_V), lambda c: (l, 0, 0))],
        out_specs=pl.BlockSpec((L, M_V), lambda c: (c, 0)),
        scratch_shapes=[pltpu.VMEM(st_shape, F32), pltpu.VMEM((1, LANES), F32)],
        compiler_params=_params(("arbitrary",), _vmem_limit(
            ((L, 2 * M_QK), BF16, 2), ((L, M_V), BF16, 6), ((L, LANES), F32, 2),
            (st_shape, F32, 2), ((L, M_V + LANES), F32, 8), ((L, L), F32, 8))),
        name="mlstm",
    )(qk, v_so, gates, gate_bias, v_so, g_mlstm)


def _attn_kernel(q_ref, k_ref, v_ref, lam_ref, g_ref, o_ref, vt_ref, acc_ref, m_ref, sa_ref, sb_ref,
                 mxa_ref, mxb_ref, *, lam_init, seq):
    t = ATT_T
    qi = pl.program_id(1)

    @pl.when(qi == 0)
    def _():
        ones_row = jnp.where(lax.broadcasted_iota(jnp.int32, (ATT_VROWS - B_V_DIM, t), 0) == 0, 1.0, 0.0)
        for kb in range(seq // t):
            v_t = v_ref[kb * t:(kb + 1) * t, :].astype(F32).T
            vt_ref[kb] = jnp.concatenate([v_t, ones_row], axis=0).astype(BF16)

    def query_block(qb):
        q = q_ref[pl.ds(pl.multiple_of(qb * t, t), t), :]
        qlane = lax.broadcasted_iota(jnp.int32, q.shape, 1)
        zero = jnp.zeros_like(q)
        return jnp.concatenate([jnp.where(qlane < B_HEAD_DIM, q, zero),
                                jnp.where(qlane >= B_HEAD_DIM, q, zero)], axis=0)

    qcat = query_block(qi)
    shift = CHUNK.bit_length() - 1
    kchunk = lax.broadcasted_iota(jnp.int32, (t, 2 * t), 0) >> shift
    qchunk = (lax.broadcasted_iota(jnp.int32, (t, 2 * t), 1) & (t - 1)) >> shift
    diag_mask = kchunk <= qchunk

    def scores(kb, s_ref, mx_ref, mask=None, qc=qcat):
        kblk = k_ref[pl.ds(pl.multiple_of(kb * t, t), t), :]
        s_t = _dot_nt(kblk, qc)
        if mask is not None:
            s_t = jnp.where(mask, s_t, NEG)
        s_ref[...] = s_t
        mx_ref[...] = jnp.max(s_t, axis=0, keepdims=True)

    def absorb(kb, s_t, mx):
        m_old = m_ref[...]
        m_new = jnp.maximum(m_old, mx)
        p = jnp.exp2(s_t - m_new).astype(BF16)
        acc_ref[...] = jnp.exp2(m_old - m_new) * acc_ref[...] + _dot(vt_ref[kb], p)
        m_ref[...] = m_new

    acc_ref[...] = jnp.zeros(acc_ref.shape, F32)
    m_ref[...] = jnp.full(m_ref.shape, NEG, F32)

    @pl.when(qi == 0)
    def _():
        scores(0, sa_ref, mxa_ref)

    def pair(kb):
        scores(kb + 1, sb_ref, mxb_ref)
        absorb(kb, sa_ref[...], mxa_ref[...])
        scores(kb + 2, sa_ref, mxa_ref)
        absorb(kb + 1, sb_ref[...], mxb_ref[...])

    def body(it, carry):
        pair(4 * it)
        pair(4 * it + 2)
        return carry

    lax.fori_loop(0, qi >> 2, body, 0)

    @pl.when((qi & 2) == 2)
    def _():
        pair(qi & ~3)

    odd = (qi & 1) == 1

    @pl.when(odd)
    def _():
        scores(qi, sb_ref, mxb_ref, diag_mask)
        absorb(qi - 1, sa_ref[...], mxa_ref[...])
        absorb(qi, sb_ref[...], mxb_ref[...])

    @pl.when(jnp.logical_not(odd))
    def _():
        s_d = jnp.where(diag_mask, sa_ref[...], NEG)
        absorb(qi, s_d, jnp.max(s_d, axis=0, keepdims=True))

    lf = lam_ref[...]
    lam = (jnp.exp(jnp.sum(lf[0:1] * lf[1:2], axis=1, keepdims=True))
           - jnp.exp(jnp.sum(lf[2:3] * lf[3:4], axis=1, keepdims=True)) + lam_init)
    acc = acc_ref[...]
    o1 = acc[:B_V_DIM, :t] / acc[B_V_DIM:B_V_DIM + 1, :t]
    o2 = acc[:B_V_DIM, t:] / acc[B_V_DIM:B_V_DIM + 1, t:]
    o_t = o1 - lam * o2
    ms = jnp.mean(o_t * o_t, axis=0, keepdims=True)
    o = (o_t * lax.rsqrt(ms + EPS)).T
    o_ref[...] = (o * (g_ref[...] * (1.0 - lam_init))).astype(o_ref.dtype)

    q_next = jnp.minimum(qi + 1, pl.num_programs(1) - 1)
    scores(0, sa_ref, mxa_ref, qc=query_block(q_next))


def _attention(qkb, v_sg, lambdas, g_diff, l, lam_init):
    s = qkb.shape[0]
    t = ATT_T
    kern = functools.partial(_attn_kernel, lam_init=lam_init, seq=s)
    return pl.pallas_call(
        kern,
        out_shape=jax.ShapeDtypeStruct((s, B_V), BF16),
        grid=(B_HEADS, s // t),
        in_specs=[pl.BlockSpec((s, LANES), lambda h, i: (0, h)),
                  pl.BlockSpec((s, LANES), lambda h, i: (0, B_HEADS + h)),
                  pl.BlockSpec((s, LANES), lambda h, i: (0, h)),
                  pl.BlockSpec((None, 4, B_HEAD_DIM), lambda h, i: (l, 0, 0)),
                  pl.BlockSpec((None, 1, LANES), lambda h, i: (l, 0, h))],
        out_specs=pl.BlockSpec((t, LANES), lambda h, i: (i, h)),
        scratch_shapes=[pltpu.VMEM((s // t, ATT_VROWS, t), BF16),
                        pltpu.VMEM((ATT_VROWS, 2 * t), F32),
                        pltpu.VMEM((1, 2 * t), F32),
                        pltpu.VMEM((t, 2 * t), F32),
                        pltpu.VMEM((t, 2 * t), F32),
                        pltpu.VMEM((1, 2 * t), F32),
                        pltpu.VMEM((1, 2 * t), F32)],
        compiler_params=_params(("arbitrary", "arbitrary"), _vmem_limit(
            ((s, LANES), BF16, 6), ((s // t, ATT_VROWS, t), BF16, 1), ((t, LANES), BF16, 6),
            ((t, 2 * t), F32, 8), ((ATT_VROWS, 2 * t), F32, 4))),
        name="diff_attn",
    )(qkb, qkb, v_sg, lambdas, g_diff)


def _merge_kernel(a_ref, b_ref, wa_ref, wb_ref, ga_ref, gb_ref, o_ref, wab_ref, wbb_ref):
    _cast_weight(wa_ref, wab_ref)
    _cast_weight(wb_ref, wbb_ref)
    a = a_ref[...]
    b = b_ref[...]
    for cs in _col_subtiles(o_ref.shape[1]):
        ya = _dot(a, wab_ref[:, cs])
        yb = _dot(b, wbb_ref[:, cs])
        o_ref[:, cs] = (ga_ref[:, cs].astype(F32) * ya + gb_ref[:, cs].astype(F32) * yb).astype(o_ref.dtype)


def _merge(ya_in, yb_in, w_a, w_b, v_sg, l):
    s, d = ya_in.shape
    n = w_a.shape[2]
    tm, tn = min(512, s), 512
    ga0 = B_V // tn
    gb0 = ga0 + n // tn
    return pl.pallas_call(
        _merge_kernel,
        out_shape=jax.ShapeDtypeStruct((s, n), BF16),
        grid=(n // tn, s // tm),
        in_specs=[pl.BlockSpec((tm, d), lambda j, i: (i, 0)),
                  pl.BlockSpec((tm, d), lambda j, i: (i, 0)),
                  pl.BlockSpec((None, d, tn), lambda j, i: (l, 0, j)),
                  pl.BlockSpec((None, d, tn), lambda j, i: (l, 0, j)),
                  pl.BlockSpec((tm, tn), lambda j, i: (i, ga0 + j)),
                  pl.BlockSpec((tm, tn), lambda j, i: (i, gb0 + j))],
        out_specs=pl.BlockSpec((tm, tn), lambda j, i: (i, j)),
        scratch_shapes=[pltpu.VMEM((d, tn), BF16), pltpu.VMEM((d, tn), BF16)],
        compiler_params=_params(("parallel", "arbitrary"), _vmem_limit(
            ((tm, d), BF16, 4), ((d, tn), F32, 4), ((d, tn), BF16, 2), ((tm, tn), BF16, 6),
            ((tm, tn), F32, 4))),
        name="merge",
    )(ya_in, yb_in, w_a, w_b, v_sg, v_sg)


def _proj_res_kernel(a_ref, w_ref, x_ref, g_ref, gt_ref, *rest, sub, with_next):
    if with_next:
        gn_ref, sc_ref, sh_ref, o_ref, hn_ref = rest
    else:
        (o_ref,) = rest
    for r in range(a_ref.shape[0] // sub):
        rs = slice(r * sub, (r + 1) * sub)
        y = _dot(a_ref[rs, :], w_ref[...])
        ms = jnp.mean(y * y, axis=-1, keepdims=True)
        x_new = x_ref[rs, :] + gt_ref[...] * (y * lax.rsqrt(ms + EPS) * g_ref[...])
        o_ref[rs, :] = x_new
        if with_next:
            hn_ref[rs, :] = _modulated_norm(x_new, gn_ref[...], sc_ref[...], sh_ref[...]).astype(hn_ref.dtype)


def _proj_res(a, w, x, gains, mod, l, gt_blk, tm, name, nxt=None):
    s, k = a.shape
    d = w.shape[2]
    tm = min(tm, s)
    vec = lambda arr, lay, blk: (arr, pl.BlockSpec((None, 1, d), lambda i: (lay, 0, blk)))
    operands = [(a, pl.BlockSpec((tm, k), lambda i: (i, 0))),
                (w, pl.BlockSpec((None, k, d), lambda i: (l, 0, 0), pipeline_mode=pl.Buffered(1))),
                (x, pl.BlockSpec((tm, d), lambda i: (i, 0))),
                vec(gains, l, 0), vec(mod, l, gt_blk)]
    out_shape = [jax.ShapeDtypeStruct((s, d), F32)]
    out_specs = [pl.BlockSpec((tm, d), lambda i: (i, 0))]
    if nxt is not None:
        n_gains, n_l, n_sc, n_sh = nxt
        operands += [vec(n_gains, n_l, 0), vec(mod, n_l, n_sc), vec(mod, n_l, n_sh)]
        out_shape.append(jax.ShapeDtypeStruct((s, d), BF16))
        out_specs.append(pl.BlockSpec((tm, d), lambda i: (i, 0)))
    kern = functools.partial(_proj_res_kernel, sub=min(tm, MXU_COLS), with_next=nxt is not None)
    outs = pl.pallas_call(
        kern,
        out_shape=out_shape,
        grid=(s // tm,),
        in_specs=[spec for _, spec in operands],
        out_specs=out_specs,
        compiler_params=_params(("parallel",), _vmem_limit(
            ((tm, k), BF16, 2), ((k, d), BF16, 1), ((tm, d), F32, 4), ((tm, d), BF16, 2),
            ((MXU_COLS, d), F32, 6))),
        name=name,
    )(*[arr for arr, _ in operands])
    return outs if nxt is not None else (outs[0], None)


def _ffn_up_kernel(h_ref, wg_ref, wu_ref, cw_ref, cb_ref, o_ref, wgb_ref, wub_ref, buf_ref, *, tm):
    _cast_weight(wg_ref, wgb_ref)
    _cast_weight(wu_ref, wub_ref)

    @pl.when(_first_row_tile())
    def _():
        buf_ref[0:SUBLANES, :] = jnp.zeros((SUBLANES, buf_ref.shape[1]), F32)

    h = h_ref[...]
    for cs in _col_subtiles(o_ref.shape[1]):
        g = _causal_conv(_dot(h, wgb_ref[:, cs]), buf_ref, cw_ref, cb_ref, cs, tm)
        o_ref[:, cs] = (_silu(g) * _dot(h, wub_ref[:, cs])).astype(o_ref.dtype)


def _ffn_up(h, w_gate, w_up, conv_w, conv_b, l):
    s, d = h.shape
    n = w_gate.shape[2]
    tm, tn = min(1024, s), 512
    kconv = conv_w.shape[1]
    kern = functools.partial(_ffn_up_kernel, tm=tm)
    return pl.pallas_call(
        kern,
        out_shape=jax.ShapeDtypeStruct((s, n), BF16),
        grid=(n // tn, s // tm),
        in_specs=[pl.BlockSpec((tm, d), lambda j, i: (i, 0)),
                  pl.BlockSpec((None, d, tn), lambda j, i: (l, 0, j)),
                  pl.BlockSpec((None, d, tn), lambda j, i: (l, 0, j)),
                  pl.BlockSpec((None, kconv, tn), lambda j, i: (l, 0, j)),
                  pl.BlockSpec((None, 1, tn), lambda j, i: (l, 0, j))],
        out_specs=pl.BlockSpec((tm, tn), lambda j, i: (i, j)),
        scratch_shapes=[pltpu.VMEM((d, tn), BF16), pltpu.VMEM((d, tn), BF16),
                        pltpu.VMEM((2 * SUBLANES, tn), F32)],
        compiler_params=_params(("parallel", "arbitrary"), _vmem_limit(
            ((tm, d), BF16, 2), ((d, tn), F32, 4), ((d, tn), BF16, 2), ((tm, tn), BF16, 2),
            ((tm, tn), F32, 6))),
        name="ffn_up",
    )(h, w_gate, w_up, conv_w, conv_b)


def _rope_tables(positions):
    half = ROPE_DIM // 2
    inv = jnp.power(ROPE_THETA, -(jnp.arange(half, dtype=F32) * 2.0 / ROPE_DIM))
    ang = positions.astype(F32)[:, None] * inv
    cos, sin = jnp.cos(ang), jnp.sin(ang)
    s = positions.shape[0]
    rest = B_HEAD_DIM - ROPE_DIM
    reps = LANES // B_HEAD_DIM
    ca = jnp.tile(jnp.concatenate([cos, cos, jnp.ones((s, rest), F32)], axis=1), (1, reps))
    cb = jnp.tile(jnp.concatenate([-sin, jnp.zeros((s, B_HEAD_DIM - half), F32)], axis=1), (1, reps))
    cc = jnp.tile(jnp.concatenate([jnp.zeros((s, half), F32), sin, jnp.zeros((s, rest), F32)], axis=1),
                  (1, reps))
    return ca, cb, cc


def kernel(x, c, positions, w_ada, b_ada, g_pre_mix, g_post_mix, g_pre_ffn, g_post_ffn, w_in, conv_qk_w, conv_qk_b, b_igate, b_fgate, g_mlstm, lambdas, g_diff, w_a, w_b, w_out, w_gate, w_up, conv_ffn_w, conv_ffn_b, w_down):
    batch, s, d = x.shape
    depth = w_in.shape[0]
    assert batch == 1 and s % 1024 == 0 and d == M_V == B_V
    xs = x.reshape(s, d)
    tm_proj = min(1024, s)

    mod = _adaln(c.reshape(d, 1), w_ada, b_ada)
    ca, cb, cc = _rope_tables(positions[0])
    rope_specs = [pl.BlockSpec((tm_proj, LANES), lambda j, i: (i, 0))] * 3

    def gains(g):
        return g.reshape(depth, 1, g.shape[1])

    g_pre_mix, g_post_mix, g_pre_ffn, g_post_ffn = map(gains, (g_pre_mix, g_post_mix, g_pre_ffn, g_post_ffn))
    g_mlstm3, g_diff3 = gains(g_mlstm), gains(g_diff)
    conv_qk_b3, conv_ffn_b3 = gains(conv_qk_b), gains(conv_ffn_b)
    o_vm = 2 * M_QK
    o_om = o_vm + M_V
    o_gates = o_om + M_V
    o_tail = o_gates + 2 * M_HEADS
    gate_bias = jnp.concatenate([b_igate, b_fgate, jnp.zeros((depth, LANES - 2 * M_HEADS), F32)],
                                axis=1).reshape(depth, 1, LANES)
    assert math.log2(M_QK_DIM) % 2 == 0
    w_in_t = jnp.swapaxes(w_in, 1, 2)
    w_head = w_in_t[:, :o_gates + LANES, :].astype(BF16)
    w_tail = _rebase_rows(w_in_t, o_tail)
    w_out16, w_down16 = w_out.astype(BF16), w_down.astype(BF16)
    tn, tw = 1024, 2048

    h = _prenorm(xs, g_pre_mix, mod, 0, 1, 0)
    for l in range(depth):
        lam_init = 0.8 - 0.6 * math.exp(-0.3 * l)

        qk = _proj(functools.partial(_proj_conv_kernel, tm=tm_proj), h, w_head, l, 0, 2 * M_QK, BF16, tn=tn,
                   extra=(conv_qk_w, conv_qk_b3),
                   extra_specs=[pl.BlockSpec((None, conv_qk_w.shape[1], tn), lambda j, i: (l, 0, j)),
                                pl.BlockSpec((None, 1, tn), lambda j, i: (l, 0, j))],
                   scratch=[pltpu.VMEM((2 * SUBLANES, tn), F32)],
                   extra_vmem=(((tm_proj, tn), F32, 3),), name="proj_conv")
        v_so = _proj(functools.partial(_proj_mixed_kernel, n_plain=M_V // tw), h, w_head, l, o_vm, 2 * M_V,
                     BF16, tn=tw, name="proj_v_so")
        gates = _proj(_proj_plain_kernel, h, w_head, l, o_gates, LANES, F32, tn=LANES, name="proj_gates")
        qkb = _proj(functools.partial(_proj_rope_kernel, n_qtiles=B_QK // tw, qscale=B_HEAD_DIM ** -0.5 * LOG2E),
                    h, w_tail, l, 0, 2 * B_QK, BF16, tn=tw, extra=(ca, cb, cc), extra_specs=rope_specs,
                    extra_vmem=(((tm_proj, LANES), F32, 9),), name="proj_rope")
        v_sg = _proj(functools.partial(_proj_mixed_kernel, n_plain=B_V // tw), h, w_tail, l, 2 * B_QK,
                     B_V + 2 * d, BF16, tn=tw, name="proj_v_sg")

        y_a_in = _mlstm(qk, v_so, gates, gate_bias, g_mlstm3, l)
        y_b_in = _attention(qkb, v_sg, lambdas, g_diff3, l, lam_init)
        merged = _merge(y_a_in, y_b_in, w_a, w_b, v_sg, l)
        xs, h = _proj_res(merged, w_out16, xs, g_post_mix, mod, l, 2, 512, "out_proj",
                          nxt=(g_pre_ffn, l, 4, 3))

        act = _ffn_up(h, w_gate, w_up, conv_ffn_w, conv_ffn_b3, l)
        nxt = (g_pre_mix, l + 1, 1, 0) if l + 1 < depth else None
        xs, h = _proj_res(act, w_down16, xs, g_post_ffn, mod, l, 5, 256, "ffn_down", nxt=nxt)

    return xs.reshape(batch, s, d)
```

```python
import functools
import math

import jax
import jax.numpy as jnp
from jax import lax
from jax.experimental import pallas as pl
from jax.experimental.pallas import tpu as pltpu

F32 = jnp.float32
BF16 = jnp.bfloat16

M_HEADS = 4
M_QK_DIM = 256
M_V_DIM = 512
M_QK = M_HEADS * M_QK_DIM
M_V = M_HEADS * M_V_DIM
B_HEADS = 16
B_HEAD_DIM = 64
B_V_DIM = 2 * B_HEAD_DIM
B_QK = B_HEADS * 2 * B_HEAD_DIM
B_V = B_HEADS * B_V_DIM
CHUNK = 64
ROPE_THETA = 500000.0
ROPE_DIM = B_HEAD_DIM // 4
EPS = 1e-6

LANES = 128
SUBLANES = 8
MXU_COLS = 256
V7X_VMEM_REQUEST_CAP = 60000 * 1024
V7X_COMPILER_SCRATCH = 8 * 1024 * 1024

MLSTM_L = 256
ATT_T = 512
ATT_VROWS = B_V_DIM + 16
LOG2E = math.log2(math.e)
NEG = -1e30


def _nbytes(shape, dtype):
    return math.prod(shape) * jnp.dtype(dtype).itemsize


def _vmem_limit(*terms):
    total = sum(_nbytes(s, d) * n for s, d, n in terms) + V7X_COMPILER_SCRATCH
    return int(min(V7X_VMEM_REQUEST_CAP, total))


def _params(sem, limit):
    return pltpu.CompilerParams(dimension_semantics=sem, vmem_limit_bytes=limit)


def _silu(x):
    return x * jax.nn.sigmoid(x)


def _dot(a, b):
    return jnp.dot(a, b, preferred_element_type=F32)


def _dot_nt(a, b):
    return lax.dot_general(a, b, (((1,), (1,)), ((), ())), preferred_element_type=F32)


def _dot_tn(a, b):
    return lax.dot_general(a, b, (((0,), (0,)), ((), ())), preferred_element_type=F32)


def _adaln_kernel(c_ref, w_ref, b_ref, o_ref):
    c = c_ref[...]
    o_ref[...] = jnp.sum(_silu(c) * w_ref[...], axis=0, keepdims=True) + b_ref[...]


def _adaln(c_col, w_ada, b_ada):
    depth, d, n = w_ada.shape
    tn = 1024
    return pl.pallas_call(
        _adaln_kernel,
        out_shape=jax.ShapeDtypeStruct((depth, 1, n), F32),
        grid=(depth, n // tn),
        in_specs=[pl.BlockSpec((d, 1), lambda l, j: (0, 0)),
                  pl.BlockSpec((None, d, tn), lambda l, j: (l, 0, j)),
                  pl.BlockSpec((None, 1, tn), lambda l, j: (l, 0, j))],
        out_specs=pl.BlockSpec((None, 1, tn), lambda l, j: (l, 0, j)),
        compiler_params=_params(("parallel", "parallel"),
                                _vmem_limit(((d, LANES), F32, 2), ((d, tn), F32, 3))),
        name="adaln",
    )(c_col, w_ada, b_ada.reshape(depth, 1, n))


def _modulated_norm(x, g, sc, sh):
    ms = jnp.mean(x * x, axis=-1, keepdims=True)
    return (x * lax.rsqrt(ms + EPS) * g) * (1.0 + sc) + sh


def _prenorm_kernel(x_ref, g_ref, sc_ref, sh_ref, o_ref):
    o_ref[...] = _modulated_norm(x_ref[...], g_ref[...], sc_ref[...], sh_ref[...]).astype(o_ref.dtype)


def _prenorm(x, gains, mod, l, sc_blk, sh_blk):
    s, d = x.shape
    tm = min(512, s)
    return pl.pallas_call(
        _prenorm_kernel,
        out_shape=jax.ShapeDtypeStruct((s, d), BF16),
        grid=(s // tm,),
        in_specs=[pl.BlockSpec((tm, d), lambda i: (i, 0)),
                  pl.BlockSpec((None, 1, d), lambda i: (l, 0, 0)),
                  pl.BlockSpec((None, 1, d), lambda i: (l, 0, sc_blk)),
                  pl.BlockSpec((None, 1, d), lambda i: (l, 0, sh_blk))],
        out_specs=pl.BlockSpec((tm, d), lambda i: (i, 0)),
        compiler_params=_params(("parallel",), _vmem_limit(((tm, d), F32, 4), ((tm, d), BF16, 2))),
        name="prenorm",
    )(x, gains, mod, mod)


REBASE_ROWS = 512


def _rebase_kernel(cur_ref, nxt_ref, o_ref):
    skip = nxt_ref.shape[0]
    o_ref[...] = jnp.concatenate([cur_ref[skip:, :], nxt_ref[...]], axis=0).astype(o_ref.dtype)


def _rebase_rows(w_t, row0):
    depth, rows, d = w_t.shape
    r = REBASE_ROWS
    base = row0 - SUBLANES
    n = rows - row0
    assert base % r == 0 and n % r == 0
    blk0, per = base // r, r // SUBLANES
    return pl.pallas_call(
        _rebase_kernel,
        out_shape=jax.ShapeDtypeStruct((depth, n, d), BF16),
        grid=(depth, n // r),
        in_specs=[pl.BlockSpec((None, r, d), lambda l, j: (l, blk0 + j, 0)),
                  pl.BlockSpec((None, SUBLANES, d), lambda l, j: (l, (blk0 + j + 1) * per, 0))],
        out_specs=pl.BlockSpec((None, r, d), lambda l, j: (l, j, 0)),
        compiler_params=_params(("parallel", "parallel"), _vmem_limit(
            ((r, d), F32, 3), ((SUBLANES, d), F32, 2), ((r, d), BF16, 3))),
        name="rebase_w_in",
    )(w_t, w_t)


def _first_row_tile():
    return pl.program_id(1) == 0


def _cast_weight(w_ref, wb_ref):
    @pl.when(_first_row_tile())
    def _():
        wb_ref[...] = w_ref[...].astype(BF16)


def _causal_conv(acc, buf_ref, cw_ref, cb_ref, cs, tm):
    kconv = cw_ref.shape[0]
    buf_ref[SUBLANES:2 * SUBLANES, cs] = acc[0:SUBLANES, :]
    out = cb_ref[:, cs] + cw_ref[kconv - 1:kconv, cs] * acc
    for j in range(kconv - 1):
        dist = kconv - 1 - j
        top = buf_ref[SUBLANES - dist:2 * SUBLANES - dist, cs]
        shifted = jnp.concatenate([top, pltpu.roll(acc, dist, 0)[SUBLANES:, :]], axis=0)
        out = out + cw_ref[j:j + 1, cs] * shifted
    buf_ref[0:SUBLANES, cs] = acc[tm - SUBLANES:tm, :]
    return out


def _col_subtiles(n):
    sub = min(MXU_COLS, n)
    return [slice(c * sub, (c + 1) * sub) for c in range(n // sub)]


def _proj_conv_kernel(h_ref, w_ref, cw_ref, cb_ref, o_ref, buf_ref, *, tm):
    @pl.when(_first_row_tile())
    def _():
        buf_ref[0:SUBLANES, :] = jnp.zeros((SUBLANES, buf_ref.shape[1]), F32)

    h = h_ref[...]
    for cs in _col_subtiles(o_ref.shape[1]):
        out = _causal_conv(_dot_nt(h, w_ref[cs, :]), buf_ref, cw_ref, cb_ref, cs, tm)
        o_ref[:, cs] = _silu(out).astype(o_ref.dtype)


def _proj_plain_kernel(h_ref, w_ref, o_ref):
    h = h_ref[...]
    for cs in _col_subtiles(o_ref.shape[1]):
        o_ref[:, cs] = _dot_nt(h, w_ref[cs, :]).astype(o_ref.dtype)


def _proj_mixed_kernel(h_ref, w_ref, o_ref, *, n_plain):
    j = pl.program_id(0)

    @pl.when(j < n_plain)
    def _():
        _proj_plain_kernel(h_ref, w_ref, o_ref)

    @pl.when(j >= n_plain)
    def _():
        h = h_ref[...]
        for cs in _col_subtiles(o_ref.shape[1]):
            o_ref[:, cs] = jax.nn.sigmoid(_dot_nt(h, w_ref[cs, :])).astype(o_ref.dtype)


def _proj_rope_kernel(h_ref, w_ref, ca_ref, cb_ref, cc_ref, o_ref, *, n_qtiles, qscale):
    scale = jnp.where(pl.program_id(0) < n_qtiles, qscale, 1.0).astype(F32)
    ca = ca_ref[...] * scale
    cb = cb_ref[...] * scale
    cc = cc_ref[...] * scale
    h = h_ref[...]
    for cs in _col_subtiles(o_ref.shape[1]):
        acc = _dot_nt(h, w_ref[cs, :])
        for s in range(acc.shape[1] // LANES):
            xs = acc[:, s * LANES:(s + 1) * LANES]
            r = (xs * ca + pltpu.roll(xs, LANES - ROPE_DIM // 2, 1) * cb
                 + pltpu.roll(xs, ROPE_DIM // 2, 1) * cc)
            o_ref[:, cs.start + s * LANES:cs.start + (s + 1) * LANES] = r.astype(o_ref.dtype)


def _proj(kernel, h, w, l, col0, n, out_dtype, *, tn, extra=(), extra_specs=(), scratch=(), extra_vmem=(),
          name):
    s, d = h.shape
    tm = min(1024, s)
    blk0 = col0 // tn
    assert blk0 * tn == col0 and n % tn == 0
    return pl.pallas_call(
        kernel,
        out_shape=jax.ShapeDtypeStruct((s, n), out_dtype),
        grid=(n // tn, s // tm),
        in_specs=[pl.BlockSpec((tm, d), lambda j, i: (i, 0)),
                  pl.BlockSpec((None, tn, d), lambda j, i: (l, blk0 + j, 0))] + list(extra_specs),
        out_specs=pl.BlockSpec((tm, tn), lambda j, i: (i, j)),
        scratch_shapes=list(scratch),
        compiler_params=_params(("parallel", "arbitrary"), _vmem_limit(
            ((tm, d), BF16, 2), ((d, tn), BF16, 2), ((tm, tn), out_dtype, 2),
            ((tm, MXU_COLS), F32, 8), *extra_vmem)),
        name=name,
    )(h, w, *extra)


def _mlstm_kernel(qk_ref, v_ref, gate_ref, bias_ref, so_ref, g_ref, o_ref, st_ref, m_ref):
    L = MLSTM_L

    @pl.when(pl.program_id(0) == 0)
    def _():
        st_ref[...] = jnp.zeros(st_ref.shape, F32)
        m_ref[...] = jnp.zeros(m_ref.shape, F32)

    ipre = gate_ref[...] + bias_ref[...]
    fpre = pltpu.roll(ipre, LANES - M_HEADS, 1)
    logf = jnp.minimum(fpre, 0.0) - jnp.log1p(jnp.exp(-jnp.abs(fpre)))

    row = lax.broadcasted_iota(jnp.int32, (L, L), 0)
    col = lax.broadcasted_iota(jnp.int32, (L, L), 1)
    tri = col <= row
    tri_b = jnp.where(tri, 1.0, 0.0).astype(BF16)
    hi = logf.astype(BF16)
    r1 = logf - hi.astype(F32)
    mid = r1.astype(BF16)
    lo = (r1 - mid.astype(F32)).astype(BF16)
    b = _dot(tri_b, hi) + _dot(tri_b, mid) + _dot(tri_b, lo)
    a = ipre - b
    a_t = a.T
    m_all = m_ref[...]
    lane = lax.broadcasted_iota(jnp.int32, (1, LANES), 1)
    ones_col = jnp.where(lax.broadcasted_iota(jnp.int32, (L, LANES), 1) == 0, 1.0, 0.0).astype(BF16)
    m_next = m_all

    for h in range(M_HEADS):
        a_row = a_t[h:h + 1, :]
        a_col = a[:, h:h + 1]
        b_col = b[:, h:h + 1]
        m_prev = m_all[:, h:h + 1]
        amask = jnp.where(tri, a_row, NEG)
        m_run = jnp.maximum(jnp.max(amask, axis=1, keepdims=True), m_prev)
        dmat = jnp.exp(amask - m_run)
        w_inter = jnp.exp(m_prev - m_run)
        q = qk_ref[:, h * M_QK_DIM:(h + 1) * M_QK_DIM] * jnp.asarray(M_QK_DIM ** -0.5, BF16)
        k = qk_ref[:, M_QK + h * M_QK_DIM:M_QK + (h + 1) * M_QK_DIM]
        v_ext = jnp.concatenate([v_ref[:, h * M_V_DIM:(h + 1) * M_V_DIM], ones_col], axis=1)
        sd = (_dot_nt(q, k) * dmat).astype(BF16)
        st = st_ref[h]
        tot = w_inter * _dot(q, st.astype(BF16)) + _dot(sd, v_ext)
        num = tot[:, :M_V_DIM]
        den = tot[:, M_V_DIM:M_V_DIM + 1]
        hval = num / jnp.maximum(jnp.abs(den), jnp.exp(-(b_col + m_run)))
        m_last = m_run[L - 1:L, :]
        w_s = jnp.exp(a_col - m_last)
        decay = jnp.exp(m_prev - m_last)
        rhs = (w_s * v_ext.astype(F32)).astype(BF16)
        st_ref[h] = decay * st + _dot_tn(k, rhs)
        m_next = jnp.where(lane == h, b_col[L - 1:L, :] + m_last, m_next)
        ms = jnp.mean(hval * hval, axis=-1, keepdims=True)
        sl = slice(h * M_V_DIM, (h + 1) * M_V_DIM)
        hn = hval * lax.rsqrt(ms + EPS) * g_ref[:, sl]
        o_ref[:, sl] = (so_ref[:, sl].astype(F32) * hn).astype(o_ref.dtype)

    m_ref[...] = m_next


def _mlstm(qk, v_so, gates, gate_bias, g_mlstm, l):
    s = qk.shape[0]
    L = MLSTM_L
    st_shape = (M_HEADS, M_QK_DIM, M_V_DIM + LANES)
    return pl.pallas_call(
        _mlstm_kernel,
        out_shape=jax.ShapeDtypeStruct((s, M_V), BF16),
        grid=(s // L,),
        in_specs=[pl.BlockSpec((L, 2 * M_QK), lambda c: (c, 0)),
                  pl.BlockSpec((L, M_V), lambda c: (c, 0)),
                  pl.BlockSpec((L, LANES), lambda c: (c, 0)),
                  pl.BlockSpec((None, 1, LANES), lambda c: (l, 0, 0)),
                  pl.BlockSpec((L, M_V), lambda c: (c, 1)),
                  pl.BlockSpec((None, 1, M_V), lambda c: (l, 0, 0))],
        out_specs=pl.BlockSpec((L, M_V), lambda c: (c, 0)),
        scratch_shapes=[pltpu.VMEM(st_shape, F32), pltpu.VMEM((1, LANES), F32)],
        compiler_params=_params(("arbitrary",), _vmem_limit(
            ((L, 2 * M_QK), BF16, 2), ((L, M_V), BF16, 6), ((L, LANES), F32, 2),
            (st_shape, F32, 2), ((L, M_V + LANES), F32, 8), ((L, L), F32, 8))),
        name="mlstm",
    )(qk, v_so, gates, gate_bias, v_so, g_mlstm)


def _attn_kernel(q_ref, k_ref, v_ref, lam_ref, g_ref, o_ref, vt_ref, acc_ref, m_ref, sa_ref, sb_ref,
                 mxa_ref, mxb_ref, *, lam_init, seq):
    t = ATT_T
    qi = pl.program_id(1)

    @pl.when(qi == 0)
    def _():
        ones_row = jnp.where(lax.broadcasted_iota(jnp.int32, (ATT_VROWS - B_V_DIM, t), 0) == 0, 1.0, 0.0)
        for kb in range(seq // t):
            v_t = v_ref[kb * t:(kb + 1) * t, :].astype(F32).T
            vt_ref[kb] = jnp.concatenate([v_t, ones_row], axis=0).astype(BF16)

    def query_block(qb):
        q = q_ref[pl.ds(pl.multiple_of(qb * t, t), t), :]
        qlane = lax.broadcasted_iota(jnp.int32, q.shape, 1)
        zero = jnp.zeros_like(q)
        return jnp.concatenate([jnp.where(qlane < B_HEAD_DIM, q, zero),
                                jnp.where(qlane >= B_HEAD_DIM, q, zero)], axis=0)

    qcat = query_block(qi)
    shift = CHUNK.bit_length() - 1
    kchunk = lax.broadcasted_iota(jnp.int32, (t, 2 * t), 0) >> shift
    qchunk = (lax.broadcasted_iota(jnp.int32, (t, 2 * t), 1) & (t - 1)) >> shift
    diag_mask = kchunk <= qchunk

    def scores(kb, s_ref, mx_ref, mask=None, qc=qcat):
        kblk = k_ref[pl.ds(pl.multiple_of(kb * t, t), t), :]
        s_t = _dot_nt(kblk, qc)
        if mask is not None:
            s_t = jnp.where(mask, s_t, NEG)
        s_ref[...] = s_t
        mx_ref[...] = jnp.max(s_t, axis=0, keepdims=True)

    def absorb(kb, s_t, mx):
        m_old = m_ref[...]
        m_new = jnp.maximum(m_old, mx)
        p = jnp.exp2(s_t - m_new).astype(BF16)
        acc_ref[...] = jnp.exp2(m_old - m_new) * acc_ref[...] + _dot(vt_ref[kb], p)
        m_ref[...] = m_new

    acc_ref[...] = jnp.zeros(acc_ref.shape, F32)
    m_ref[...] = jnp.full(m_ref.shape, NEG, F32)

    @pl.when(qi == 0)
    def _():
        scores(0, sa_ref, mxa_ref)

    def pair(kb):
        scores(kb + 1, sb_ref, mxb_ref)
        absorb(kb, sa_ref[...], mxa_ref[...])
        scores(kb + 2, sa_ref, mxa_ref)
        absorb(kb + 1, sb_ref[...], mxb_ref[...])

    def body(it, carry):
        pair(4 * it)
        pair(4 * it + 2)
        return carry

    lax.fori_loop(0, qi >> 2, body, 0)

    @pl.when((qi & 2) == 2)
    def _():
        pair(qi & ~3)

    odd = (qi & 1) == 1

    @pl.when(odd)
    def _():
        scores(qi, sb_ref, mxb_ref, diag_mask)
        absorb(qi - 1, sa_ref[...], mxa_ref[...])
        absorb(qi, sb_ref[...], mxb_ref[...])

    @pl.when(jnp.logical_not(odd))
    def _():
        s_d = jnp.where(diag_mask, sa_ref[...], NEG)
        absorb(qi, s_d, jnp.max(s_d, axis=0, keepdims=True))

    lf = lam_ref[...]
    lam = (jnp.exp(jnp.sum(lf[0:1] * lf[1:2], axis=1, keepdims=True))
           - jnp.exp(jnp.sum(lf[2:3] * lf[3:4], axis=1, keepdims=True)) + lam_init)
    acc = acc_ref[...]
    o1 = acc[:B_V_DIM, :t] / acc[B_V_DIM:B_V_DIM + 1, :t]
    o2 = acc[:B_V_DIM, t:] / acc[B_V_DIM:B_V_DIM + 1, t:]
    o_t = o1 - lam * o2
    ms = jnp.mean(o_t * o_t, axis=0, keepdims=True)
    o = (o_t * lax.rsqrt(ms + EPS)).T
    o_ref[...] = (o * (g_ref[...] * (1.0 - lam_init))).astype(o_ref.dtype)

    q_next = jnp.minimum(qi + 1, pl.num_programs(1) - 1)
    scores(0, sa_ref, mxa_ref, qc=query_block(q_next))


def _attention(qkb, v_sg, lambdas, g_diff, l, lam_init):
    s = qkb.shape[0]
    t = ATT_T
    kern = functools.partial(_attn_kernel, lam_init=lam_init, seq=s)
    return pl.pallas_call(
        kern,
        out_shape=jax.ShapeDtypeStruct((s, B_V), BF16),
        grid=(B_HEADS, s // t),
        in_specs=[pl.BlockSpec((s, LANES), lambda h, i: (0, h)),
                  pl.BlockSpec((s, LANES), lambda h, i: (0, B_HEADS + h)),
                  pl.BlockSpec((s, LANES), lambda h, i: (0, h)),
                  pl.BlockSpec((None, 4, B_HEAD_DIM), lambda h, i: (l, 0, 0)),
                  pl.BlockSpec((None, 1, LANES), lambda h, i: (l, 0, h))],
        out_specs=pl.BlockSpec((t, LANES), lambda h, i: (i, h)),
        scratch_shapes=[pltpu.VMEM((s // t, ATT_VROWS, t), BF16),
                        pltpu.VMEM((ATT_VROWS, 2 * t), F32),
                        pltpu.VMEM((1, 2 * t), F32),
                        pltpu.VMEM((t, 2 * t), F32),
                        pltpu.VMEM((t, 2 * t), F32),
                        pltpu.VMEM((1, 2 * t), F32),
                        pltpu.VMEM((1, 2 * t), F32)],
        compiler_params=_params(("arbitrary", "arbitrary"), _vmem_limit(
            ((s, LANES), BF16, 6), ((s // t, ATT_VROWS, t), BF16, 1), ((t, LANES), BF16, 6),
            ((t, 2 * t), F32, 8), ((ATT_VROWS, 2 * t), F32, 4))),
        name="diff_attn",
    )(qkb, qkb, v_sg, lambdas, g_diff)


def _mix_out_kernel(a_ref, b_ref, ga_ref, gb_ref, wa_ref, wb_ref, wo_ref, x_ref, g_ref, gt_ref, gn_ref, sc_ref,
                    sh_ref, o_ref, hn_ref, m_ref):
    a = a_ref[...]
    b = b_ref[...]
    for cs in _col_subtiles(m_ref.shape[1]):
        ya = _dot(a, wa_ref[:, cs])
        yb = _dot(b, wb_ref[:, cs])
        m_ref[:, cs] = (ga_ref[:, cs].astype(F32) * ya + gb_ref[:, cs].astype(F32) * yb).astype(m_ref.dtype)
    y = _dot(m_ref[...], wo_ref[...])
    _residual_epilogue(y, x_ref, g_ref, gt_ref, (gn_ref, sc_ref, sh_ref), o_ref, hn_ref)


def _mix_out(ya_in, yb_in, v_sg, w_a, w_b, w_out, x, g_post, mod, g_next, l):
    s, d = x.shape
    tm = min(MXU_COLS, s)
    row = lambda c: pl.BlockSpec((tm, d), lambda i: (i, c))
    weight = pl.BlockSpec((None, d, d), lambda i: (l, 0, 0), pipeline_mode=pl.Buffered(1))
    vec = lambda blk: pl.BlockSpec((None, 1, d), lambda i: (l, 0, blk))
    return pl.pallas_call(
        _mix_out_kernel,
        out_shape=[jax.ShapeDtypeStruct((s, d), F32), jax.ShapeDtypeStruct((s, d), BF16)],
        grid=(s // tm,),
        in_specs=[row(0), row(0), row(1), row(2), weight, weight, weight, row(0), vec(0), vec(2), vec(0), vec(4),
                  vec(3)],
        out_specs=[row(0), row(0)],
        scratch_shapes=[pltpu.VMEM((tm, d), BF16)],
        compiler_params=_params(("parallel",), _vmem_limit(
            ((d, d), BF16, 3), ((tm, d), BF16, 11), ((tm, d), F32, 7))),
        name="mix_out",
    )(ya_in, yb_in, v_sg, v_sg, w_a, w_b, w_out, x, g_post, mod, g_next, mod, mod)


def _residual_epilogue(y, x_ref, g_ref, gt_ref, nxt_refs, o_ref, hn_ref):
    ms = jnp.mean(y * y, axis=-1, keepdims=True)
    x_new = x_ref[...] + gt_ref[...] * (y * lax.rsqrt(ms + EPS) * g_ref[...])
    o_ref[...] = x_new
    if nxt_refs is not None:
        gn_ref, sc_ref, sh_ref = nxt_refs
        hn_ref[...] = _modulated_norm(x_new, gn_ref[...], sc_ref[...], sh_ref[...]).astype(hn_ref.dtype)


def _proj_res_kernel(a_ref, w_ref, x_ref, g_ref, gt_ref, *rest, with_next):
    if with_next:
        *nxt_refs, o_ref, hn_ref = rest
    else:
        (o_ref,), nxt_refs, hn_ref = rest, None, None
    _residual_epilogue(_dot(a_ref[...], w_ref[...]), x_ref, g_ref, gt_ref, nxt_refs, o_ref, hn_ref)


def _proj_res(a, w, x, gains, mod, l, gt_blk, tm, name, nxt=None):
    s, k = a.shape
    d = w.shape[2]
    tm = min(tm, s)
    rows = lambda width: pl.BlockSpec((tm, width), lambda i: (i, 0))
    vec = lambda arr, lay, blk: (arr, pl.BlockSpec((None, 1, d), lambda i: (lay, 0, blk)))
    operands = [(a, rows(k)),
                (w, pl.BlockSpec((None, k, d), lambda i: (l, 0, 0), pipeline_mode=pl.Buffered(1))),
                (x, rows(d)), vec(gains, l, 0), vec(mod, l, gt_blk)]
    out_shape = [jax.ShapeDtypeStruct((s, d), F32)]
    out_specs = [rows(d)]
    if nxt is not None:
        n_gains, n_l, n_sc, n_sh = nxt
        operands += [vec(n_gains, n_l, 0), vec(mod, n_l, n_sc), vec(mod, n_l, n_sh)]
        out_shape.append(jax.ShapeDtypeStruct((s, d), BF16))
        out_specs.append(rows(d))
    outs = pl.pallas_call(
        functools.partial(_proj_res_kernel, with_next=nxt is not None),
        out_shape=out_shape,
        grid=(s // tm,),
        in_specs=[spec for _, spec in operands],
        out_specs=out_specs,
        compiler_params=_params(("parallel",), _vmem_limit(
            ((tm, k), BF16, 2), ((k, d), BF16, 1), ((tm, d), F32, 10), ((tm, d), BF16, 2))),
        name=name,
    )(*[arr for arr, _ in operands])
    return outs if nxt is not None else (outs[0], None)


def _ffn_up_kernel(h_ref, wg_ref, wu_ref, cw_ref, cb_ref, o_ref, wgb_ref, wub_ref, buf_ref, *, tm):
    _cast_weight(wg_ref, wgb_ref)
    _cast_weight(wu_ref, wub_ref)

    @pl.when(_first_row_tile())
    def _():
        buf_ref[0:SUBLANES, :] = jnp.zeros((SUBLANES, buf_ref.shape[1]), F32)

    h = h_ref[...]
    for cs in _col_subtiles(o_ref.shape[1]):
        g = _causal_conv(_dot(h, wgb_ref[:, cs]), buf_ref, cw_ref, cb_ref, cs, tm)
        o_ref[:, cs] = (_silu(g) * _dot(h, wub_ref[:, cs])).astype(o_ref.dtype)


def _ffn_up(h, w_gate, w_up, conv_w, conv_b, l):
    s, d = h.shape
    n = w_gate.shape[2]
    tm, tn = min(1024, s), 512
    kconv = conv_w.shape[1]
    kern = functools.partial(_ffn_up_kernel, tm=tm)
    return pl.pallas_call(
        kern,
        out_shape=jax.ShapeDtypeStruct((s, n), BF16),
        grid=(n // tn, s // tm),
        in_specs=[pl.BlockSpec((tm, d), lambda j, i: (i, 0)),
                  pl.BlockSpec((None, d, tn), lambda j, i: (l, 0, j)),
                  pl.BlockSpec((None, d, tn), lambda j, i: (l, 0, j)),
                  pl.BlockSpec((None, kconv, tn), lambda j, i: (l, 0, j)),
                  pl.BlockSpec((None, 1, tn), lambda j, i: (l, 0, j))],
        out_specs=pl.BlockSpec((tm, tn), lambda j, i: (i, j)),
        scratch_shapes=[pltpu.VMEM((d, tn), BF16), pltpu.VMEM((d, tn), BF16),
                        pltpu.VMEM((2 * SUBLANES, tn), F32)],
        compiler_params=_params(("parallel", "arbitrary"), _vmem_limit(
            ((tm, d), BF16, 2), ((d, tn), F32, 4), ((d, tn), BF16, 2), ((tm, tn), BF16, 2),
            ((tm, tn), F32, 6))),
        name="ffn_up",
    )(h, w_gate, w_up, conv_w, conv_b)


def _rope_tables(positions):
    half = ROPE_DIM // 2
    inv = jnp.power(ROPE_THETA, -(jnp.arange(half, dtype=F32) * 2.0 / ROPE_DIM))
    ang = positions.astype(F32)[:, None] * inv
    cos, sin = jnp.cos(ang), jnp.sin(ang)
    s = positions.shape[0]
    rest = B_HEAD_DIM - ROPE_DIM
    reps = LANES // B_HEAD_DIM
    ca = jnp.tile(jnp.concatenate([cos, cos, jnp.ones((s, rest), F32)], axis=1), (1, reps))
    cb = jnp.tile(jnp.concatenate([-sin, jnp.zeros((s, B_HEAD_DIM - half), F32)], axis=1), (1, reps))
    cc = jnp.tile(jnp.concatenate([jnp.zeros((s, half), F32), sin, jnp.zeros((s, rest), F32)], axis=1),
                  (1, reps))
    return ca, cb, cc


def kernel(x, c, positions, w_ada, b_ada, g_pre_mix, g_post_mix, g_pre_ffn, g_post_ffn, w_in, conv_qk_w, conv_qk_b, b_igate, b_fgate, g_mlstm, lambdas, g_diff, w_a, w_b, w_out, w_gate, w_up, conv_ffn_w, conv_ffn_b, w_down):
    batch, s, d = x.shape
    depth = w_in.shape[0]
    assert batch == 1 and s % 1024 == 0 and d == M_V == B_V
    xs = x.reshape(s, d)
    tm_proj = min(1024, s)

    mod = _adaln(c.reshape(d, 1), w_ada, b_ada)
    ca, cb, cc = _rope_tables(positions[0])
    rope_specs = [pl.BlockSpec((tm_proj, LANES), lambda j, i: (i, 0))] * 3

    def gains(g):
        return g.reshape(depth, 1, g.shape[1])

    g_pre_mix, g_post_mix, g_pre_ffn, g_post_ffn = map(gains, (g_pre_mix, g_post_mix, g_pre_ffn, g_post_ffn))
    g_mlstm3, g_diff3 = gains(g_mlstm), gains(g_diff)
    conv_qk_b3, conv_ffn_b3 = gains(conv_qk_b), gains(conv_ffn_b)
    o_vm = 2 * M_QK
    o_om = o_vm + M_V
    o_gates = o_om + M_V
    o_tail = o_gates + 2 * M_HEADS
    gate_bias = jnp.concatenate([b_igate, b_fgate, jnp.zeros((depth, LANES - 2 * M_HEADS), F32)],
                                axis=1).reshape(depth, 1, LANES)
    assert math.log2(M_QK_DIM) % 2 == 0
    w_in_t = jnp.swapaxes(w_in, 1, 2)
    w_head = w_in_t[:, :o_gates + LANES, :].astype(BF16)
    w_tail = _rebase_rows(w_in_t, o_tail)
    w_a16, w_b16, w_out16, w_down16 = (w.astype(BF16) for w in (w_a, w_b, w_out, w_down))
    tn, tw = 1024, 2048

    h = _prenorm(xs, g_pre_mix, mod, 0, 1, 0)
    for l in range(depth):
        lam_init = 0.8 - 0.6 * math.exp(-0.3 * l)

        qk = _proj(functools.partial(_proj_conv_kernel, tm=tm_proj), h, w_head, l, 0, 2 * M_QK, BF16, tn=tn,
                   extra=(conv_qk_w, conv_qk_b3),
                   extra_specs=[pl.BlockSpec((None, conv_qk_w.shape[1], tn), lambda j, i: (l, 0, j)),
                                pl.BlockSpec((None, 1, tn), lambda j, i: (l, 0, j))],
                   scratch=[pltpu.VMEM((2 * SUBLANES, tn), F32)],
                   extra_vmem=(((tm_proj, tn), F32, 3),), name="proj_conv")
        v_so = _proj(functools.partial(_proj_mixed_kernel, n_plain=M_V // tw), h, w_head, l, o_vm, 2 * M_V,
                     BF16, tn=tw, name="proj_v_so")
        gates = _proj(_proj_plain_kernel, h, w_head, l, o_gates, LANES, F32, tn=LANES, name="proj_gates")
        qkb = _proj(functools.partial(_proj_rope_kernel, n_qtiles=B_QK // tw, qscale=B_HEAD_DIM ** -0.5 * LOG2E),
                    h, w_tail, l, 0, 2 * B_QK, BF16, tn=tw, extra=(ca, cb, cc), extra_specs=rope_specs,
                    extra_vmem=(((tm_proj, LANES), F32, 9),), name="proj_rope")
        v_sg = _proj(functools.partial(_proj_mixed_kernel, n_plain=B_V // tw), h, w_tail, l, 2 * B_QK,
                     B_V + 2 * d, BF16, tn=tw, name="proj_v_sg")

        y_a_in = _mlstm(qk, v_so, gates, gate_bias, g_mlstm3, l)
        y_b_in = _attention(qkb, v_sg, lambdas, g_diff3, l, lam_init)
        xs, h = _mix_out(y_a_in, y_b_in, v_sg, w_a16, w_b16, w_out16, xs, g_post_mix, mod, g_pre_ffn, l)

        act = _ffn_up(h, w_gate, w_up, conv_ffn_w, conv_ffn_b3, l)
        nxt = (g_pre_mix, l + 1, 1, 0) if l + 1 < depth else None
        xs, h = _proj_res(act, w_down16, xs, g_post_ffn, mod, l, 5, 256, "ffn_down", nxt=nxt)

    return xs.reshape(batch, s, d)
```

```python
import functools
import math

import jax
import jax.numpy as jnp
from jax import lax
from jax.experimental import pallas as pl
from jax.experimental.pallas import tpu as pltpu

F32 = jnp.float32
BF16 = jnp.bfloat16

M_HEADS = 4
M_QK_DIM = 256
M_V_DIM = 512
M_QK = M_HEADS * M_QK_DIM
M_V = M_HEADS * M_V_DIM
B_HEADS = 16
B_HEAD_DIM = 64
B_V_DIM = 2 * B_HEAD_DIM
B_QK = B_HEADS * 2 * B_HEAD_DIM
B_V = B_HEADS * B_V_DIM
CHUNK = 64
ROPE_THETA = 500000.0
ROPE_DIM = B_HEAD_DIM // 4
EPS = 1e-6

LANES = 128
SUBLANES = 8
MXU_COLS = 256
V7X_VMEM_REQUEST_CAP = 60000 * 1024
V7X_COMPILER_SCRATCH = 8 * 1024 * 1024

MLSTM_L = 256
ATT_T = 512
ATT_VROWS = B_V_DIM + 16
ATT_PAD = LANES
LOG2E = math.log2(math.e)
NEG = -1e30


def _nbytes(shape, dtype):
    return math.prod(shape) * jnp.dtype(dtype).itemsize


def _vmem_limit(*terms):
    total = sum(_nbytes(s, d) * n for s, d, n in terms) + V7X_COMPILER_SCRATCH
    return int(min(V7X_VMEM_REQUEST_CAP, total))


def _params(sem, limit):
    return pltpu.CompilerParams(dimension_semantics=sem, vmem_limit_bytes=limit)


def _silu(x):
    return x * jax.nn.sigmoid(x)


def _dot(a, b):
    return jnp.dot(a, b, preferred_element_type=F32)


def _dot_nt(a, b):
    return lax.dot_general(a, b, (((1,), (1,)), ((), ())), preferred_element_type=F32)


def _dot_tn(a, b):
    return lax.dot_general(a, b, (((0,), (0,)), ((), ())), preferred_element_type=F32)


def _adaln_kernel(c_ref, w_ref, b_ref, o_ref):
    c = c_ref[...]
    o_ref[...] = jnp.sum(_silu(c) * w_ref[...], axis=0, keepdims=True) + b_ref[...]


def _adaln(c_col, w_ada, b_ada):
    depth, d, n = w_ada.shape
    tn = 1024
    return pl.pallas_call(
        _adaln_kernel,
        out_shape=jax.ShapeDtypeStruct((depth, 1, n), F32),
        grid=(depth, n // tn),
        in_specs=[pl.BlockSpec((d, 1), lambda l, j: (0, 0)),
                  pl.BlockSpec((None, d, tn), lambda l, j: (l, 0, j)),
                  pl.BlockSpec((None, 1, tn), lambda l, j: (l, 0, j))],
        out_specs=pl.BlockSpec((None, 1, tn), lambda l, j: (l, 0, j)),
        compiler_params=_params(("parallel", "parallel"),
                                _vmem_limit(((d, LANES), F32, 2), ((d, tn), F32, 3))),
        name="adaln",
    )(c_col, w_ada, b_ada.reshape(depth, 1, n))


def _modulated_norm(x, g, sc, sh):
    ms = jnp.mean(x * x, axis=-1, keepdims=True)
    return (x * lax.rsqrt(ms + EPS) * g) * (1.0 + sc) + sh


def _prenorm_kernel(x_ref, g_ref, sc_ref, sh_ref, o_ref):
    o_ref[...] = _modulated_norm(x_ref[...], g_ref[...], sc_ref[...], sh_ref[...]).astype(o_ref.dtype)


def _prenorm(x, gains, mod, l, sc_blk, sh_blk):
    s, d = x.shape
    tm = min(512, s)
    return pl.pallas_call(
        _prenorm_kernel,
        out_shape=jax.ShapeDtypeStruct((s, d), BF16),
        grid=(s // tm,),
        in_specs=[pl.BlockSpec((tm, d), lambda i: (i, 0)),
                  pl.BlockSpec((None, 1, d), lambda i: (l, 0, 0)),
                  pl.BlockSpec((None, 1, d), lambda i: (l, 0, sc_blk)),
                  pl.BlockSpec((None, 1, d), lambda i: (l, 0, sh_blk))],
        out_specs=pl.BlockSpec((tm, d), lambda i: (i, 0)),
        compiler_params=_params(("parallel",), _vmem_limit(((tm, d), F32, 4), ((tm, d), BF16, 2))),
        name="prenorm",
    )(x, gains, mod, mod)


REBASE_ROWS = 512


def _rebase_kernel(cur_ref, nxt_ref, o_ref):
    skip = nxt_ref.shape[0]
    o_ref[...] = jnp.concatenate([cur_ref[skip:, :], nxt_ref[...]], axis=0).astype(o_ref.dtype)


def _rebase_rows(w_t, row0):
    depth, rows, d = w_t.shape
    r = REBASE_ROWS
    base = row0 - SUBLANES
    n = rows - row0
    assert base % r == 0 and n % r == 0
    blk0, per = base // r, r // SUBLANES
    return pl.pallas_call(
        _rebase_kernel,
        out_shape=jax.ShapeDtypeStruct((depth, n, d), BF16),
        grid=(depth, n // r),
        in_specs=[pl.BlockSpec((None, r, d), lambda l, j: (l, blk0 + j, 0)),
                  pl.BlockSpec((None, SUBLANES, d), lambda l, j: (l, (blk0 + j + 1) * per, 0))],
        out_specs=pl.BlockSpec((None, r, d), lambda l, j: (l, j, 0)),
        compiler_params=_params(("parallel", "parallel"), _vmem_limit(
            ((r, d), F32, 3), ((SUBLANES, d), F32, 2), ((r, d), BF16, 3))),
        name="rebase_w_in",
    )(w_t, w_t)


def _first_row_tile():
    return pl.program_id(1) == 0


def _cast_weight(w_ref, wb_ref):
    @pl.when(_first_row_tile())
    def _():
        wb_ref[...] = w_ref[...].astype(BF16)


def _causal_conv(acc, buf_ref, cw_ref, cb_ref, cs, tm):
    kconv = cw_ref.shape[0]
    buf_ref[SUBLANES:2 * SUBLANES, cs] = acc[0:SUBLANES, :]
    out = cb_ref[:, cs] + cw_ref[kconv - 1:kconv, cs] * acc
    for j in range(kconv - 1):
        dist = kconv - 1 - j
        top = buf_ref[SUBLANES - dist:2 * SUBLANES - dist, cs]
        shifted = jnp.concatenate([top, pltpu.roll(acc, dist, 0)[SUBLANES:, :]], axis=0)
        out = out + cw_ref[j:j + 1, cs] * shifted
    buf_ref[0:SUBLANES, cs] = acc[tm - SUBLANES:tm, :]
    return out


def _col_subtiles(n):
    sub = min(MXU_COLS, n)
    return [slice(c * sub, (c + 1) * sub) for c in range(n // sub)]


def _proj_conv_kernel(h_ref, w_ref, cw_ref, cb_ref, o_ref, buf_ref, *, tm):
    @pl.when(_first_row_tile())
    def _():
        buf_ref[0:SUBLANES, :] = jnp.zeros((SUBLANES, buf_ref.shape[1]), F32)

    h = h_ref[...]
    for cs in _col_subtiles(o_ref.shape[1]):
        out = _causal_conv(_dot_nt(h, w_ref[cs, :]), buf_ref, cw_ref, cb_ref, cs, tm)
        o_ref[:, cs] = _silu(out).astype(o_ref.dtype)


def _proj_plain_kernel(h_ref, w_ref, o_ref):
    h = h_ref[...]
    for cs in _col_subtiles(o_ref.shape[1]):
        o_ref[:, cs] = _dot_nt(h, w_ref[cs, :]).astype(o_ref.dtype)


def _proj_mixed_kernel(h_ref, w_ref, o_ref, *, n_plain):
    j = pl.program_id(0)

    @pl.when(j < n_plain)
    def _():
        _proj_plain_kernel(h_ref, w_ref, o_ref)

    @pl.when(j >= n_plain)
    def _():
        h = h_ref[...]
        for cs in _col_subtiles(o_ref.shape[1]):
            o_ref[:, cs] = jax.nn.sigmoid(_dot_nt(h, w_ref[cs, :])).astype(o_ref.dtype)


def _proj_rope_kernel(h_ref, w_ref, ca_ref, cb_ref, cc_ref, o_ref, *, n_qtiles, qscale):
    scale = jnp.where(pl.program_id(0) < n_qtiles, qscale, 1.0).astype(F32)
    ca = ca_ref[...] * scale
    cb = cb_ref[...] * scale
    cc = cc_ref[...] * scale
    h = h_ref[...]
    for cs in _col_subtiles(o_ref.shape[1]):
        acc = _dot_nt(h, w_ref[cs, :])
        for s in range(acc.shape[1] // LANES):
            xs = acc[:, s * LANES:(s + 1) * LANES]
            r = (xs * ca + pltpu.roll(xs, LANES - ROPE_DIM // 2, 1) * cb
                 + pltpu.roll(xs, ROPE_DIM // 2, 1) * cc)
            o_ref[:, cs.start + s * LANES:cs.start + (s + 1) * LANES] = r.astype(o_ref.dtype)


def _proj(kernel, h, w, l, col0, n, out_dtype, *, tn, extra=(), extra_specs=(), scratch=(), extra_vmem=(),
          name):
    s, d = h.shape
    tm = min(1024, s)
    blk0 = col0 // tn
    assert blk0 * tn == col0 and n % tn == 0
    return pl.pallas_call(
        kernel,
        out_shape=jax.ShapeDtypeStruct((s, n), out_dtype),
        grid=(n // tn, s // tm),
        in_specs=[pl.BlockSpec((tm, d), lambda j, i: (i, 0)),
                  pl.BlockSpec((None, tn, d), lambda j, i: (l, blk0 + j, 0))] + list(extra_specs),
        out_specs=pl.BlockSpec((tm, tn), lambda j, i: (i, j)),
        scratch_shapes=list(scratch),
        compiler_params=_params(("parallel", "arbitrary"), _vmem_limit(
            ((tm, d), BF16, 2), ((d, tn), BF16, 2), ((tm, tn), out_dtype, 2),
            ((tm, MXU_COLS), F32, 8), *extra_vmem)),
        name=name,
    )(h, w, *extra)


def _mlstm_kernel(qk_ref, v_ref, gate_ref, bias_ref, so_ref, g_ref, o_ref, st_ref, m_ref):
    L = MLSTM_L

    @pl.when(pl.program_id(0) == 0)
    def _():
        st_ref[...] = jnp.zeros(st_ref.shape, F32)
        m_ref[...] = jnp.zeros(m_ref.shape, F32)

    ipre = gate_ref[...] + bias_ref[...]
    fpre = pltpu.roll(ipre, LANES - M_HEADS, 1)
    logf = jnp.minimum(fpre, 0.0) - jnp.log1p(jnp.exp(-jnp.abs(fpre)))

    row = lax.broadcasted_iota(jnp.int32, (L, L), 0)
    col = lax.broadcasted_iota(jnp.int32, (L, L), 1)
    tri = col <= row
    tri_b = jnp.where(tri, 1.0, 0.0).astype(BF16)
    hi = logf.astype(BF16)
    r1 = logf - hi.astype(F32)
    mid = r1.astype(BF16)
    lo = (r1 - mid.astype(F32)).astype(BF16)
    b = _dot(tri_b, hi) + _dot(tri_b, mid) + _dot(tri_b, lo)
    a = ipre - b
    a_t = a.T
    m_all = m_ref[...]
    lane = lax.broadcasted_iota(jnp.int32, (1, LANES), 1)
    ones_col = jnp.where(lax.broadcasted_iota(jnp.int32, (L, LANES), 1) == 0, 1.0, 0.0).astype(BF16)
    m_next = m_all

    for h in range(M_HEADS):
        a_row = a_t[h:h + 1, :]
        a_col = a[:, h:h + 1]
        b_col = b[:, h:h + 1]
        m_prev = m_all[:, h:h + 1]
        amask = jnp.where(tri, a_row, NEG)
        m_run = jnp.maximum(jnp.max(amask, axis=1, keepdims=True), m_prev)
        dmat = jnp.exp(amask - m_run)
        w_inter = jnp.exp(m_prev - m_run)
        q = qk_ref[:, h * M_QK_DIM:(h + 1) * M_QK_DIM] * jnp.asarray(M_QK_DIM ** -0.5, BF16)
        k = qk_ref[:, M_QK + h * M_QK_DIM:M_QK + (h + 1) * M_QK_DIM]
        v_ext = jnp.concatenate([v_ref[:, h * M_V_DIM:(h + 1) * M_V_DIM], ones_col], axis=1)
        sd = (_dot_nt(q, k) * dmat).astype(BF16)
        st = st_ref[h]
        tot = w_inter * _dot(q, st.astype(BF16)) + _dot(sd, v_ext)
        num = tot[:, :M_V_DIM]
        den = tot[:, M_V_DIM:M_V_DIM + 1]
        hval = num / jnp.maximum(jnp.abs(den), jnp.exp(-(b_col + m_run)))
        m_last = m_run[L - 1:L, :]
        w_s = jnp.exp(a_col - m_last)
        decay = jnp.exp(m_prev - m_last)
        rhs = (w_s * v_ext.astype(F32)).astype(BF16)
        st_ref[h] = decay * st + _dot_tn(k, rhs)
        m_next = jnp.where(lane == h, b_col[L - 1:L, :] + m_last, m_next)
        ms = jnp.mean(hval * hval, axis=-1, keepdims=True)
        sl = slice(h * M_V_DIM, (h + 1) * M_V_DIM)
        hn = hval * lax.rsqrt(ms + EPS) * g_ref[:, sl]
        o_ref[:, sl] = (so_ref[:, sl].astype(F32) * hn).astype(o_ref.dtype)

    m_ref[...] = m_next


def _mlstm(qk, v_so, gates, gate_bias, g_mlstm, l):
    s = qk.shape[0]
    L = MLSTM_L
    st_shape = (M_HEADS, M_QK_DIM, M_V_DIM + LANES)
    return pl.pallas_call(
        _mlstm_kernel,
        out_shape=jax.ShapeDtypeStruct((s, M_V), BF16),
        grid=(s // L,),
        in_specs=[pl.BlockSpec((L, 2 * M_QK), lambda c: (c, 0)),
                  pl.BlockSpec((L, M_V), lambda c: (c, 0)),
                  pl.BlockSpec((L, LANES), lambda c: (c, 0)),
                  pl.BlockSpec((None, 1, LANES), lambda c: (l, 0, 0)),
                  pl.BlockSpec((L, M_V), lambda c: (c, 1)),
                  pl.BlockSpec((None, 1, M_V), lambda c: (l, 0, 0))],
        out_specs=pl.BlockSpec((L, M_V), lambda c: (c, 0)),
        scratch_shapes=[pltpu.VMEM(st_shape, F32), pltpu.VMEM((1, LANES), F32)],
        compiler_params=_params(("arbitrary",), _vmem_limit(
            ((L, 2 * M_QK), BF16, 2), ((L, M_V), BF16, 6), ((L, LANES), F32, 2),
            (st_shape, F32, 2), ((L, M_V + LANES), F32, 8), ((L, L), F32, 8))),
        name="mlstm",
    )(qk, v_so, gates, gate_bias, v_so, g_mlstm)


def _attn_kernel(q_ref, k_ref, v_ref, lam_ref, g_ref, o_ref, vt_ref, acc_ref, m_ref, sa_ref, sb_ref,
                 mxa_ref, mxb_ref, *, lam_init, seq):
    t = ATT_T
    qi = pl.program_id(1)

    @pl.when(qi == 0)
    def _():
        ones_row = jnp.where(lax.broadcasted_iota(jnp.int32, (ATT_VROWS - B_V_DIM, t), 0) == 0, 1.0, 0.0)
        for kb in range(seq // t):
            v_t = v_ref[kb * t:(kb + 1) * t, :].astype(F32).T
            vt_ref[kb] = jnp.concatenate([v_t, ones_row], axis=0).astype(BF16)

    def query_block(qb):
        q = q_ref[pl.ds(pl.multiple_of(qb * t, t), t), :]
        qlane = lax.broadcasted_iota(jnp.int32, q.shape, 1)
        zero = jnp.zeros_like(q)
        return jnp.concatenate([jnp.where(qlane < B_HEAD_DIM, q, zero),
                                jnp.where(qlane >= B_HEAD_DIM, q, zero)], axis=0)

    qcat = query_block(qi)
    shift = CHUNK.bit_length() - 1
    kchunk = lax.broadcasted_iota(jnp.int32, (t, 2 * t), 0) >> shift
    qchunk = (lax.broadcasted_iota(jnp.int32, (t, 2 * t), 1) & (t - 1)) >> shift
    diag_mask = kchunk <= qchunk

    def scores(kb, s_ref, mx_ref, mask=None, qc=qcat):
        kblk = k_ref[pl.ds(pl.multiple_of(kb * t, t), t), :]
        s_t = _dot_nt(kblk, qc)
        if mask is not None:
            s_t = jnp.where(mask, s_t, NEG)
        s_ref[:, :2 * t] = s_t
        mx_ref[...] = jnp.max(s_t, axis=0, keepdims=True)

    def absorb(kb, s_t, mx):
        m_old = m_ref[...]
        m_new = jnp.maximum(m_old, mx)
        p = jnp.exp2(s_t - m_new).astype(BF16)
        acc_ref[...] = jnp.exp2(m_old - m_new) * acc_ref[...] + _dot(vt_ref[kb], p)
        m_ref[...] = m_new

    acc_ref[...] = jnp.zeros(acc_ref.shape, F32)
    m_ref[...] = jnp.full(m_ref.shape, NEG, F32)

    @pl.when(qi == 0)
    def _():
        scores(0, sa_ref, mxa_ref)

    def pair(kb):
        scores(kb + 1, sb_ref, mxb_ref)
        absorb(kb, sa_ref[:, :2 * t], mxa_ref[...])
        scores(kb + 2, sa_ref, mxa_ref)
        absorb(kb + 1, sb_ref[:, :2 * t], mxb_ref[...])

    def body(it, carry):
        pair(4 * it)
        pair(4 * it + 2)
        return carry

    lax.fori_loop(0, qi >> 2, body, 0)

    @pl.when((qi & 2) == 2)
    def _():
        pair(qi & ~3)

    odd = (qi & 1) == 1

    @pl.when(odd)
    def _():
        scores(qi, sb_ref, mxb_ref, diag_mask)
        absorb(qi - 1, sa_ref[:, :2 * t], mxa_ref[...])
        absorb(qi, sb_ref[:, :2 * t], mxb_ref[...])

    @pl.when(jnp.logical_not(odd))
    def _():
        s_d = jnp.where(diag_mask, sa_ref[:, :2 * t], NEG)
        absorb(qi, s_d, jnp.max(s_d, axis=0, keepdims=True))

    lf = lam_ref[...]
    lam = (jnp.exp(jnp.sum(lf[0:1] * lf[1:2], axis=1, keepdims=True))
           - jnp.exp(jnp.sum(lf[2:3] * lf[3:4], axis=1, keepdims=True)) + lam_init)
    acc = acc_ref[...]
    o1 = acc[:B_V_DIM, :t] / acc[B_V_DIM:B_V_DIM + 1, :t]
    o2 = acc[:B_V_DIM, t:] / acc[B_V_DIM:B_V_DIM + 1, t:]
    o_t = o1 - lam * o2
    ms = jnp.mean(o_t * o_t, axis=0, keepdims=True)
    o = (o_t * lax.rsqrt(ms + EPS)).T
    o_ref[...] = (o * (g_ref[...] * (1.0 - lam_init))).astype(o_ref.dtype)

    q_next = jnp.minimum(qi + 1, pl.num_programs(1) - 1)
    scores(0, sa_ref, mxa_ref, qc=query_block(q_next))


def _attention(qkb, v_sg, lambdas, g_diff, l, lam_init):
    s = qkb.shape[0]
    t = ATT_T
    kern = functools.partial(_attn_kernel, lam_init=lam_init, seq=s)
    return pl.pallas_call(
        kern,
        out_shape=jax.ShapeDtypeStruct((s, B_V), BF16),
        grid=(B_HEADS, s // t),
        in_specs=[pl.BlockSpec((s, LANES), lambda h, i: (0, h)),
                  pl.BlockSpec((s, LANES), lambda h, i: (0, B_HEADS + h)),
                  pl.BlockSpec((s, LANES), lambda h, i: (0, h)),
                  pl.BlockSpec((None, 4, B_HEAD_DIM), lambda h, i: (l, 0, 0)),
                  pl.BlockSpec((None, 1, LANES), lambda h, i: (l, 0, h))],
        out_specs=pl.BlockSpec((t, LANES), lambda h, i: (i, h)),
        scratch_shapes=[pltpu.VMEM((s // t, ATT_VROWS, t), BF16),
                        pltpu.VMEM((ATT_VROWS, 2 * t), F32),
                        pltpu.VMEM((1, 2 * t), F32),
                        pltpu.VMEM((t, 2 * t + ATT_PAD), F32),
                        pltpu.VMEM((t, 2 * t + ATT_PAD), F32),
                        pltpu.VMEM((1, 2 * t), F32),
                        pltpu.VMEM((1, 2 * t), F32)],
        compiler_params=_params(("arbitrary", "arbitrary"), _vmem_limit(
            ((s, LANES), BF16, 6), ((s // t, ATT_VROWS, t), BF16, 1), ((t, LANES), BF16, 6),
            ((t, 2 * t), F32, 8), ((ATT_VROWS, 2 * t), F32, 4))),
        name="diff_attn",
    )(qkb, qkb, v_sg, lambdas, g_diff)


def _mix_out_kernel(a_ref, b_ref, ga_ref, gb_ref, wa_ref, wb_ref, wo_ref, x_ref, g_ref, gt_ref, gn_ref, sc_ref,
                    sh_ref, o_ref, hn_ref, m_ref):
    a = a_ref[...]
    b = b_ref[...]
    for cs in _col_subtiles(m_ref.shape[1]):
        ya = _dot(a, wa_ref[:, cs])
        yb = _dot(b, wb_ref[:, cs])
        m_ref[:, cs] = (ga_ref[:, cs].astype(F32) * ya + gb_ref[:, cs].astype(F32) * yb).astype(m_ref.dtype)
    y = _dot(m_ref[...], wo_ref[...])
    _residual_epilogue(y, x_ref, g_ref, gt_ref, (gn_ref, sc_ref, sh_ref), o_ref, hn_ref)


def _mix_out(ya_in, yb_in, v_sg, w_a, w_b, w_out, x, g_post, mod, g_next, l):
    s, d = x.shape
    tm = min(MXU_COLS, s)
    row = lambda c: pl.BlockSpec((tm, d), lambda i: (i, c))
    weight = pl.BlockSpec((None, d, d), lambda i: (l, 0, 0), pipeline_mode=pl.Buffered(1))
    vec = lambda blk: pl.BlockSpec((None, 1, d), lambda i: (l, 0, blk))
    return pl.pallas_call(
        _mix_out_kernel,
        out_shape=[jax.ShapeDtypeStruct((s, d), F32), jax.ShapeDtypeStruct((s, d), BF16)],
        grid=(s // tm,),
        in_specs=[row(0), row(0), row(1), row(2), weight, weight, weight, row(0), vec(0), vec(2), vec(0), vec(4),
                  vec(3)],
        out_specs=[row(0), row(0)],
        scratch_shapes=[pltpu.VMEM((tm, d), BF16)],
        compiler_params=_params(("parallel",), _vmem_limit(
            ((d, d), BF16, 3), ((tm, d), BF16, 11), ((tm, d), F32, 7))),
        name="mix_out",
    )(ya_in, yb_in, v_sg, v_sg, w_a, w_b, w_out, x, g_post, mod, g_next, mod, mod)


def _residual_epilogue(y, x_ref, g_ref, gt_ref, nxt_refs, o_ref, hn_ref):
    ms = jnp.mean(y * y, axis=-1, keepdims=True)
    x_new = x_ref[...] + gt_ref[...] * (y * lax.rsqrt(ms + EPS) * g_ref[...])
    o_ref[...] = x_new
    if nxt_refs is not None:
        gn_ref, sc_ref, sh_ref = nxt_refs
        hn_ref[...] = _modulated_norm(x_new, gn_ref[...], sc_ref[...], sh_ref[...]).astype(hn_ref.dtype)


def _proj_res_kernel(a_ref, w_ref, x_ref, g_ref, gt_ref, *rest, with_next):
    if with_next:
        *nxt_refs, o_ref, hn_ref = rest
    else:
        (o_ref,), nxt_refs, hn_ref = rest, None, None
    _residual_epilogue(_dot(a_ref[...], w_ref[...]), x_ref, g_ref, gt_ref, nxt_refs, o_ref, hn_ref)


def _proj_res(a, w, x, gains, mod, l, gt_blk, tm, name, nxt=None):
    s, k = a.shape
    d = w.shape[2]
    tm = min(tm, s)
    rows = lambda width: pl.BlockSpec((tm, width), lambda i: (i, 0))
    vec = lambda arr, lay, blk: (arr, pl.BlockSpec((None, 1, d), lambda i: (lay, 0, blk)))
    operands = [(a, rows(k)),
                (w, pl.BlockSpec((None, k, d), lambda i: (l, 0, 0), pipeline_mode=pl.Buffered(1))),
                (x, rows(d)), vec(gains, l, 0), vec(mod, l, gt_blk)]
    out_shape = [jax.ShapeDtypeStruct((s, d), F32)]
    out_specs = [rows(d)]
    if nxt is not None:
        n_gains, n_l, n_sc, n_sh = nxt
        operands += [vec(n_gains, n_l, 0), vec(mod, n_l, n_sc), vec(mod, n_l, n_sh)]
        out_shape.append(jax.ShapeDtypeStruct((s, d), BF16))
        out_specs.append(rows(d))
    outs = pl.pallas_call(
        functools.partial(_proj_res_kernel, with_next=nxt is not None),
        out_shape=out_shape,
        grid=(s // tm,),
        in_specs=[spec for _, spec in operands],
        out_specs=out_specs,
        compiler_params=_params(("parallel",), _vmem_limit(
            ((tm, k), BF16, 2), ((k, d), BF16, 1), ((tm, d), F32, 10), ((tm, d), BF16, 2))),
        name=name,
    )(*[arr for arr, _ in operands])
    return outs if nxt is not None else (outs[0], None)


def _ffn_up_kernel(h_ref, wg_ref, wu_ref, cw_ref, cb_ref, o_ref, wgb_ref, wub_ref, buf_ref, *, tm):
    _cast_weight(wg_ref, wgb_ref)
    _cast_weight(wu_ref, wub_ref)

    @pl.when(_first_row_tile())
    def _():
        buf_ref[0:SUBLANES, :] = jnp.zeros((SUBLANES, buf_ref.shape[1]), F32)

    h = h_ref[...]
    for cs in _col_subtiles(o_ref.shape[1]):
        g = _causal_conv(_dot(h, wgb_ref[:, cs]), buf_ref, cw_ref, cb_ref, cs, tm)
        o_ref[:, cs] = (_silu(g) * _dot(h, wub_ref[:, cs])).astype(o_ref.dtype)


def _ffn_up(h, w_gate, w_up, conv_w, conv_b, l):
    s, d = h.shape
    n = w_gate.shape[2]
    tm, tn = min(1024, s), 512
    kconv = conv_w.shape[1]
    kern = functools.partial(_ffn_up_kernel, tm=tm)
    return pl.pallas_call(
        kern,
        out_shape=jax.ShapeDtypeStruct((s, n), BF16),
        grid=(n // tn, s // tm),
        in_specs=[pl.BlockSpec((tm, d), lambda j, i: (i, 0)),
                  pl.BlockSpec((None, d, tn), lambda j, i: (l, 0, j)),
                  pl.BlockSpec((None, d, tn), lambda j, i: (l, 0, j)),
                  pl.BlockSpec((None, kconv, tn), lambda j, i: (l, 0, j)),
                  pl.BlockSpec((None, 1, tn), lambda j, i: (l, 0, j))],
        out_specs=pl.BlockSpec((tm, tn), lambda j, i: (i, j)),
        scratch_shapes=[pltpu.VMEM((d, tn), BF16), pltpu.VMEM((d, tn), BF16),
                        pltpu.VMEM((2 * SUBLANES, tn), F32)],
        compiler_params=_params(("parallel", "arbitrary"), _vmem_limit(
            ((tm, d), BF16, 2), ((d, tn), F32, 4), ((d, tn), BF16, 2), ((tm, tn), BF16, 2),
            ((tm, tn), F32, 6))),
        name="ffn_up",
    )(h, w_gate, w_up, conv_w, conv_b)


def _rope_tables(positions):
    half = ROPE_DIM // 2
    inv = jnp.power(ROPE_THETA, -(jnp.arange(half, dtype=F32) * 2.0 / ROPE_DIM))
    ang = positions.astype(F32)[:, None] * inv
    cos, sin = jnp.cos(ang), jnp.sin(ang)
    s = positions.shape[0]
    rest = B_HEAD_DIM - ROPE_DIM
    reps = LANES // B_HEAD_DIM
    ca = jnp.tile(jnp.concatenate([cos, cos, jnp.ones((s, rest), F32)], axis=1), (1, reps))
    cb = jnp.tile(jnp.concatenate([-sin, jnp.zeros((s, B_HEAD_DIM - half), F32)], axis=1), (1, reps))
    cc = jnp.tile(jnp.concatenate([jnp.zeros((s, half), F32), sin, jnp.zeros((s, rest), F32)], axis=1),
                  (1, reps))
    return ca, cb, cc


def kernel(x, c, positions, w_ada, b_ada, g_pre_mix, g_post_mix, g_pre_ffn, g_post_ffn, w_in, conv_qk_w, conv_qk_b, b_igate, b_fgate, g_mlstm, lambdas, g_diff, w_a, w_b, w_out, w_gate, w_up, conv_ffn_w, conv_ffn_b, w_down):
    batch, s, d = x.shape
    depth = w_in.shape[0]
    assert batch == 1 and s % 1024 == 0 and d == M_V == B_V
    xs = x.reshape(s, d)
    tm_proj = min(1024, s)

    mod = _adaln(c.reshape(d, 1), w_ada, b_ada)
    ca, cb, cc = _rope_tables(positions[0])
    rope_specs = [pl.BlockSpec((tm_proj, LANES), lambda j, i: (i, 0))] * 3

    def gains(g):
        return g.reshape(depth, 1, g.shape[1])

    g_pre_mix, g_post_mix, g_pre_ffn, g_post_ffn = map(gains, (g_pre_mix, g_post_mix, g_pre_ffn, g_post_ffn))
    g_mlstm3, g_diff3 = gains(g_mlstm), gains(g_diff)
    conv_qk_b3, conv_ffn_b3 = gains(conv_qk_b), gains(conv_ffn_b)
    o_vm = 2 * M_QK
    o_om = o_vm + M_V
    o_gates = o_om + M_V
    o_tail = o_gates + 2 * M_HEADS
    gate_bias = jnp.concatenate([b_igate, b_fgate, jnp.zeros((depth, LANES - 2 * M_HEADS), F32)],
                                axis=1).reshape(depth, 1, LANES)
    assert math.log2(M_QK_DIM) % 2 == 0
    w_in_t = jnp.swapaxes(w_in, 1, 2)
    w_head = w_in_t[:, :o_gates + LANES, :].astype(BF16)
    w_tail = _rebase_rows(w_in_t, o_tail)
    w_a16, w_b16, w_out16, w_down16 = (w.astype(BF16) for w in (w_a, w_b, w_out, w_down))
    tn, tw = 1024, 2048

    h = _prenorm(xs, g_pre_mix, mod, 0, 1, 0)
    for l in range(depth):
        lam_init = 0.8 - 0.6 * math.exp(-0.3 * l)

        qk = _proj(functools.partial(_proj_conv_kernel, tm=tm_proj), h, w_head, l, 0, 2 * M_QK, BF16, tn=tn,
                   extra=(conv_qk_w, conv_qk_b3),
                   extra_specs=[pl.BlockSpec((None, conv_qk_w.shape[1], tn), lambda j, i: (l, 0, j)),
                                pl.BlockSpec((None, 1, tn), lambda j, i: (l, 0, j))],
                   scratch=[pltpu.VMEM((2 * SUBLANES, tn), F32)],
                   extra_vmem=(((tm_proj, tn), F32, 3),), name="proj_conv")
        v_so = _proj(functools.partial(_proj_mixed_kernel, n_plain=M_V // tw), h, w_head, l, o_vm, 2 * M_V,
                     BF16, tn=tw, name="proj_v_so")
        gates = _proj(_proj_plain_kernel, h, w_head, l, o_gates, LANES, F32, tn=LANES, name="proj_gates")
        qkb = _proj(functools.partial(_proj_rope_kernel, n_qtiles=B_QK // tw, qscale=B_HEAD_DIM ** -0.5 * LOG2E),
                    h, w_tail, l, 0, 2 * B_QK, BF16, tn=tw, extra=(ca, cb, cc), extra_specs=rope_specs,
                    extra_vmem=(((tm_proj, LANES), F32, 9),), name="proj_rope")
        v_sg = _proj(functools.partial(_proj_mixed_kernel, n_plain=B_V // tw), h, w_tail, l, 2 * B_QK,
                     B_V + 2 * d, BF16, tn=tw, name="proj_v_sg")

        y_a_in = _mlstm(qk, v_so, gates, gate_bias, g_mlstm3, l)
        y_b_in = _attention(qkb, v_sg, lambdas, g_diff3, l, lam_init)
        xs, h = _mix_out(y_a_in, y_b_in, v_sg, w_a16, w_b16, w_out16, xs, g_post_mix, mod, g_pre_ffn, l)

        act = _ffn_up(h, w_gate, w_up, conv_ffn_w, conv_ffn_b3, l)
        nxt = (g_pre_mix, l + 1, 1, 0) if l + 1 < depth else None
        xs, h = _proj_res(act, w_down16, xs, g_post_ffn, mod, l, 5, 256, "ffn_down", nxt=nxt)

    return xs.reshape(batch, s, d)
```

```python
import functools
import math

import jax
import jax.numpy as jnp
from jax import lax
from jax.experimental import pallas as pl
from jax.experimental.pallas import tpu as pltpu

F32 = jnp.float32
BF16 = jnp.bfloat16

M_HEADS = 4
M_QK_DIM = 256
M_V_DIM = 512
M_QK = M_HEADS * M_QK_DIM
M_V = M_HEADS * M_V_DIM
B_HEADS = 16
B_HEAD_DIM = 64
B_V_DIM = 2 * B_HEAD_DIM
B_QK = B_HEADS * 2 * B_HEAD_DIM
B_V = B_HEADS * B_V_DIM
CHUNK = 64
ROPE_THETA = 500000.0
ROPE_DIM = B_HEAD_DIM // 4
EPS = 1e-6

LANES = 128
SUBLANES = 8
MXU_COLS = 256
V7X_VMEM_REQUEST_CAP = 60000 * 1024
V7X_COMPILER_SCRATCH = 8 * 1024 * 1024

MLSTM_L = 256
ATT_T = 512
ATT_VROWS = B_V_DIM + 16
ATT_PAD = LANES
LOG2E = math.log2(math.e)
NEG = -1e30


def _nbytes(shape, dtype):
    return math.prod(shape) * jnp.dtype(dtype).itemsize


def _vmem_limit(*terms):
    total = sum(_nbytes(s, d) * n for s, d, n in terms) + V7X_COMPILER_SCRATCH
    return int(min(V7X_VMEM_REQUEST_CAP, total))


def _params(sem, limit):
    return pltpu.CompilerParams(dimension_semantics=sem, vmem_limit_bytes=limit)


def _silu(x):
    return x * jax.nn.sigmoid(x)


def _dot(a, b):
    return jnp.dot(a, b, preferred_element_type=F32)


def _dot_nt(a, b):
    return lax.dot_general(a, b, (((1,), (1,)), ((), ())), preferred_element_type=F32)


def _dot_tn(a, b):
    return lax.dot_general(a, b, (((0,), (0,)), ((), ())), preferred_element_type=F32)


def _adaln_kernel(c_ref, w_ref, b_ref, o_ref):
    c = c_ref[...]
    o_ref[...] = jnp.sum(_silu(c) * w_ref[...], axis=0, keepdims=True) + b_ref[...]


def _adaln(c_col, w_ada, b_ada):
    depth, d, n = w_ada.shape
    tn = 1024
    return pl.pallas_call(
        _adaln_kernel,
        out_shape=jax.ShapeDtypeStruct((depth, 1, n), F32),
        grid=(depth, n // tn),
        in_specs=[pl.BlockSpec((d, 1), lambda l, j: (0, 0)),
                  pl.BlockSpec((None, d, tn), lambda l, j: (l, 0, j)),
                  pl.BlockSpec((None, 1, tn), lambda l, j: (l, 0, j))],
        out_specs=pl.BlockSpec((None, 1, tn), lambda l, j: (l, 0, j)),
        compiler_params=_params(("parallel", "parallel"),
                                _vmem_limit(((d, LANES), F32, 2), ((d, tn), F32, 3))),
        name="adaln",
    )(c_col, w_ada, b_ada.reshape(depth, 1, n))


def _modulated_norm(x, g, sc, sh):
    ms = jnp.mean(x * x, axis=-1, keepdims=True)
    return (x * lax.rsqrt(ms + EPS) * g) * (1.0 + sc) + sh


def _prenorm_kernel(x_ref, g_ref, sc_ref, sh_ref, o_ref):
    o_ref[...] = _modulated_norm(x_ref[...], g_ref[...], sc_ref[...], sh_ref[...]).astype(o_ref.dtype)


def _prenorm(x, gains, mod, l, sc_blk, sh_blk):
    s, d = x.shape
    tm = min(512, s)
    return pl.pallas_call(
        _prenorm_kernel,
        out_shape=jax.ShapeDtypeStruct((s, d), BF16),
        grid=(s // tm,),
        in_specs=[pl.BlockSpec((tm, d), lambda i: (i, 0)),
                  pl.BlockSpec((None, 1, d), lambda i: (l, 0, 0)),
                  pl.BlockSpec((None, 1, d), lambda i: (l, 0, sc_blk)),
                  pl.BlockSpec((None, 1, d), lambda i: (l, 0, sh_blk))],
        out_specs=pl.BlockSpec((tm, d), lambda i: (i, 0)),
        compiler_params=_params(("parallel",), _vmem_limit(((tm, d), F32, 4), ((tm, d), BF16, 2))),
        name="prenorm",
    )(x, gains, mod, mod)


REBASE_ROWS = 512


def _rebase_kernel(cur_ref, nxt_ref, o_ref):
    skip = nxt_ref.shape[0]
    o_ref[...] = jnp.concatenate([cur_ref[skip:, :], nxt_ref[...]], axis=0).astype(o_ref.dtype)


def _rebase_rows(w_t, row0):
    depth, rows, d = w_t.shape
    r = REBASE_ROWS
    base = row0 - SUBLANES
    n = rows - row0
    assert base % r == 0 and n % r == 0
    blk0, per = base // r, r // SUBLANES
    return pl.pallas_call(
        _rebase_kernel,
        out_shape=jax.ShapeDtypeStruct((depth, n, d), BF16),
        grid=(depth, n // r),
        in_specs=[pl.BlockSpec((None, r, d), lambda l, j: (l, blk0 + j, 0)),
                  pl.BlockSpec((None, SUBLANES, d), lambda l, j: (l, (blk0 + j + 1) * per, 0))],
        out_specs=pl.BlockSpec((None, r, d), lambda l, j: (l, j, 0)),
        compiler_params=_params(("parallel", "parallel"), _vmem_limit(
            ((r, d), F32, 3), ((SUBLANES, d), F32, 2), ((r, d), BF16, 3))),
        name="rebase_w_in",
    )(w_t, w_t)


def _first_row_tile():
    return pl.program_id(1) == 0


def _cast_weight(w_ref, wb_ref):
    @pl.when(_first_row_tile())
    def _():
        wb_ref[...] = w_ref[...].astype(BF16)


def _causal_conv(acc, buf_ref, cw_ref, cb_ref, cs, tm):
    kconv = cw_ref.shape[0]
    buf_ref[SUBLANES:2 * SUBLANES, cs] = acc[0:SUBLANES, :]
    out = cb_ref[:, cs] + cw_ref[kconv - 1:kconv, cs] * acc
    for j in range(kconv - 1):
        dist = kconv - 1 - j
        top = buf_ref[SUBLANES - dist:2 * SUBLANES - dist, cs]
        shifted = jnp.concatenate([top, pltpu.roll(acc, dist, 0)[SUBLANES:, :]], axis=0)
        out = out + cw_ref[j:j + 1, cs] * shifted
    buf_ref[0:SUBLANES, cs] = acc[tm - SUBLANES:tm, :]
    return out


def _col_subtiles(n):
    sub = min(MXU_COLS, n)
    return [slice(c * sub, (c + 1) * sub) for c in range(n // sub)]


def _proj_conv_kernel(h_ref, w_ref, cw_ref, cb_ref, o_ref, buf_ref, *, tm):
    @pl.when(_first_row_tile())
    def _():
        buf_ref[0:SUBLANES, :] = jnp.zeros((SUBLANES, buf_ref.shape[1]), F32)

    h = h_ref[...]
    for cs in _col_subtiles(o_ref.shape[1]):
        out = _causal_conv(_dot_nt(h, w_ref[cs, :]), buf_ref, cw_ref, cb_ref, cs, tm)
        o_ref[:, cs] = _silu(out).astype(o_ref.dtype)


def _proj_plain_kernel(h_ref, w_ref, o_ref):
    h = h_ref[...]
    for cs in _col_subtiles(o_ref.shape[1]):
        o_ref[:, cs] = _dot_nt(h, w_ref[cs, :]).astype(o_ref.dtype)


def _proj_mixed_kernel(h_ref, w_ref, o_ref, *, n_plain):
    j = pl.program_id(0)

    @pl.when(j < n_plain)
    def _():
        _proj_plain_kernel(h_ref, w_ref, o_ref)

    @pl.when(j >= n_plain)
    def _():
        h = h_ref[...]
        for cs in _col_subtiles(o_ref.shape[1]):
            o_ref[:, cs] = jax.nn.sigmoid(_dot_nt(h, w_ref[cs, :])).astype(o_ref.dtype)


def _proj_rope_kernel(h_ref, w_ref, ca_ref, cb_ref, cc_ref, o_ref, *, n_qtiles, qscale):
    scale = jnp.where(pl.program_id(0) < n_qtiles, qscale, 1.0).astype(F32)
    ca = ca_ref[...] * scale
    cb = cb_ref[...] * scale
    cc = cc_ref[...] * scale
    h = h_ref[...]
    for cs in _col_subtiles(o_ref.shape[1]):
        acc = _dot_nt(h, w_ref[cs, :])
        for s in range(acc.shape[1] // LANES):
            xs = acc[:, s * LANES:(s + 1) * LANES]
            r = (xs * ca + pltpu.roll(xs, LANES - ROPE_DIM // 2, 1) * cb
                 + pltpu.roll(xs, ROPE_DIM // 2, 1) * cc)
            o_ref[:, cs.start + s * LANES:cs.start + (s + 1) * LANES] = r.astype(o_ref.dtype)


def _proj(kernel, h, w, l, col0, n, out_dtype, *, tn, extra=(), extra_specs=(), scratch=(), extra_vmem=(),
          name):
    s, d = h.shape
    tm = min(1024, s)
    blk0 = col0 // tn
    assert blk0 * tn == col0 and n % tn == 0
    return pl.pallas_call(
        kernel,
        out_shape=jax.ShapeDtypeStruct((s, n), out_dtype),
        grid=(n // tn, s // tm),
        in_specs=[pl.BlockSpec((tm, d), lambda j, i: (i, 0)),
                  pl.BlockSpec((None, tn, d), lambda j, i: (l, blk0 + j, 0))] + list(extra_specs),
        out_specs=pl.BlockSpec((tm, tn), lambda j, i: (i, j)),
        scratch_shapes=list(scratch),
        compiler_params=_params(("parallel", "arbitrary"), _vmem_limit(
            ((tm, d), BF16, 2), ((d, tn), BF16, 2), ((tm, tn), out_dtype, 2),
            ((tm, MXU_COLS), F32, 8), *extra_vmem)),
        name=name,
    )(h, w, *extra)


def _mlstm_kernel(qk_ref, v_ref, gate_ref, bias_ref, so_ref, g_ref, o_ref, st_ref, m_ref):
    L = MLSTM_L

    @pl.when(pl.program_id(0) == 0)
    def _():
        st_ref[...] = jnp.zeros(st_ref.shape, F32)
        m_ref[...] = jnp.zeros(m_ref.shape, F32)

    ipre = gate_ref[...] + bias_ref[...]
    fpre = pltpu.roll(ipre, LANES - M_HEADS, 1)
    logf = jnp.minimum(fpre, 0.0) - jnp.log1p(jnp.exp(-jnp.abs(fpre)))

    row = lax.broadcasted_iota(jnp.int32, (L, L), 0)
    col = lax.broadcasted_iota(jnp.int32, (L, L), 1)
    tri = col <= row
    tri_b = jnp.where(tri, 1.0, 0.0).astype(BF16)
    hi = logf.astype(BF16)
    r1 = logf - hi.astype(F32)
    mid = r1.astype(BF16)
    lo = (r1 - mid.astype(F32)).astype(BF16)
    b = _dot(tri_b, hi) + _dot(tri_b, mid) + _dot(tri_b, lo)
    a = ipre - b
    a_t = a.T
    m_all = m_ref[...]
    lane = lax.broadcasted_iota(jnp.int32, (1, LANES), 1)
    ones_col = jnp.where(lax.broadcasted_iota(jnp.int32, (L, LANES), 1) == 0, 1.0, 0.0).astype(BF16)
    m_next = m_all

    for h in range(M_HEADS):
        a_row = a_t[h:h + 1, :]
        a_col = a[:, h:h + 1]
        b_col = b[:, h:h + 1]
        m_prev = m_all[:, h:h + 1]
        amask = jnp.where(tri, a_row, NEG)
        m_run = jnp.maximum(jnp.max(amask, axis=1, keepdims=True), m_prev)
        dmat = jnp.exp(amask - m_run)
        w_inter = jnp.exp(m_prev - m_run)
        q = qk_ref[:, h * M_QK_DIM:(h + 1) * M_QK_DIM] * jnp.asarray(M_QK_DIM ** -0.5, BF16)
        k = qk_ref[:, M_QK + h * M_QK_DIM:M_QK + (h + 1) * M_QK_DIM]
        v_ext = jnp.concatenate([v_ref[:, h * M_V_DIM:(h + 1) * M_V_DIM], ones_col], axis=1)
        sd = (_dot_nt(q, k) * dmat).astype(BF16)
        st = st_ref[h]
        tot = w_inter * _dot(q, st.astype(BF16)) + _dot(sd, v_ext)
        num = tot[:, :M_V_DIM]
        den = tot[:, M_V_DIM:M_V_DIM + 1]
        hval = num / jnp.maximum(jnp.abs(den), jnp.exp(-(b_col + m_run)))
        m_last = m_run[L - 1:L, :]
        w_s = jnp.exp(a_col - m_last)
        decay = jnp.exp(m_prev - m_last)
        rhs = (w_s * v_ext.astype(F32)).astype(BF16)
        st_ref[h] = decay * st + _dot_tn(k, rhs)
        m_next = jnp.where(lane == h, b_col[L - 1:L, :] + m_last, m_next)
        ms = jnp.mean(hval * hval, axis=-1, keepdims=True)
        sl = slice(h * M_V_DIM, (h + 1) * M_V_DIM)
        hn = hval * lax.rsqrt(ms + EPS) * g_ref[:, sl]
        o_ref[:, sl] = (so_ref[:, sl].astype(F32) * hn).astype(o_ref.dtype)

    m_ref[...] = m_next


def _mlstm(qk, v_so, gates, gate_bias, g_mlstm, l):
    s = qk.shape[0]
    L = MLSTM_L
    st_shape = (M_HEADS, M_QK_DIM, M_V_DIM + LANES)
    return pl.pallas_call(
        _mlstm_kernel,
        out_shape=jax.ShapeDtypeStruct((s, M_V), BF16),
        grid=(s // L,),
        in_specs=[pl.BlockSpec((L, 2 * M_QK), lambda c: (c, 0)),
                  pl.BlockSpec((L, M_V), lambda c: (c, 0)),
                  pl.BlockSpec((L, LANES), lambda c: (c, 0)),
                  pl.BlockSpec((None, 1, LANES), lambda c: (l, 0, 0)),
                  pl.BlockSpec((L, M_V), lambda c: (c, 1)),
                  pl.BlockSpec((None, 1, M_V), lambda c: (l, 0, 0))],
        out_specs=pl.BlockSpec((L, M_V), lambda c: (c, 0)),
        scratch_shapes=[pltpu.VMEM(st_shape, F32), pltpu.VMEM((1, LANES), F32)],
        compiler_params=_params(("arbitrary",), _vmem_limit(
            ((L, 2 * M_QK), BF16, 2), ((L, M_V), BF16, 6), ((L, LANES), F32, 2),
            (st_shape, F32, 2), ((L, M_V + LANES), F32, 8), ((L, L), F32, 8))),
        name="mlstm",
    )(qk, v_so, gates, gate_bias, v_so, g_mlstm)


def _attn_kernel(q_ref, k_ref, v_ref, lam_ref, g_ref, o_ref, vt_ref, acc_ref, m_ref, sa_ref, sb_ref,
                 mxa_ref, mxb_ref, *, lam_init, seq):
    t = ATT_T
    qi = pl.program_id(1)

    @pl.when(qi == 0)
    def _():
        ones_row = jnp.where(lax.broadcasted_iota(jnp.int32, (ATT_VROWS - B_V_DIM, t), 0) == 0, 1.0, 0.0)
        for kb in range(seq // t):
            v_t = v_ref[kb * t:(kb + 1) * t, :].astype(F32).T
            vt_ref[kb] = jnp.concatenate([v_t, ones_row], axis=0).astype(BF16)

    def query_block(qb):
        q = q_ref[pl.ds(pl.multiple_of(qb * t, t), t), :]
        qlane = lax.broadcasted_iota(jnp.int32, q.shape, 1)
        zero = jnp.zeros_like(q)
        return jnp.concatenate([jnp.where(qlane < B_HEAD_DIM, q, zero),
                                jnp.where(qlane >= B_HEAD_DIM, q, zero)], axis=0)

    qcat = query_block(qi)
    shift = CHUNK.bit_length() - 1
    kchunk = lax.broadcasted_iota(jnp.int32, (t, 2 * t), 0) >> shift
    qchunk = (lax.broadcasted_iota(jnp.int32, (t, 2 * t), 1) & (t - 1)) >> shift
    diag_mask = kchunk <= qchunk

    def scores(kb, s_ref, mx_ref, mask=None, qc=qcat):
        kblk = k_ref[pl.ds(pl.multiple_of(kb * t, t), t), :]
        s_t = _dot_nt(kblk, qc)
        if mask is not None:
            s_t = jnp.where(mask, s_t, NEG)
        s_ref[:, :2 * t] = s_t
        mx_ref[...] = jnp.max(s_t, axis=0, keepdims=True)

    def absorb(kb, s_t, mx):
        m_old = m_ref[...]
        m_new = jnp.maximum(m_old, mx)
        p = jnp.exp2(s_t - m_new).astype(BF16)
        acc_ref[...] = jnp.exp2(m_old - m_new) * acc_ref[...] + _dot(vt_ref[kb], p)
        m_ref[...] = m_new

    acc_ref[...] = jnp.zeros(acc_ref.shape, F32)
    m_ref[...] = jnp.full(m_ref.shape, NEG, F32)

    @pl.when(qi == 0)
    def _():
        scores(0, sa_ref, mxa_ref)

    def pair(kb):
        scores(kb + 1, sb_ref, mxb_ref)
        absorb(kb, sa_ref[:, :2 * t], mxa_ref[...])
        scores(kb + 2, sa_ref, mxa_ref)
        absorb(kb + 1, sb_ref[:, :2 * t], mxb_ref[...])

    def body(it, carry):
        pair(4 * it)
        pair(4 * it + 2)
        return carry

    lax.fori_loop(0, qi >> 2, body, 0)

    def finish():
        lf = lam_ref[...]
        lam = (jnp.exp(jnp.sum(lf[0:1] * lf[1:2], axis=1, keepdims=True))
               - jnp.exp(jnp.sum(lf[2:3] * lf[3:4], axis=1, keepdims=True)) + lam_init)
        acc = acc_ref[...]
        o1 = acc[:B_V_DIM, :t] / acc[B_V_DIM:B_V_DIM + 1, :t]
        o2 = acc[:B_V_DIM, t:] / acc[B_V_DIM:B_V_DIM + 1, t:]
        o_t = o1 - lam * o2
        ms = jnp.mean(o_t * o_t, axis=0, keepdims=True)
        o = (o_t * lax.rsqrt(ms + EPS)).T
        o_ref[...] = (o * (g_ref[...] * (1.0 - lam_init))).astype(o_ref.dtype)
        q_next = jnp.minimum(qi + 1, pl.num_programs(1) - 1)
        scores(0, sa_ref, mxa_ref, qc=query_block(q_next))

    def tail(rem):
        if rem & 2:
            pair(qi - rem)
        if rem & 1:
            scores(qi, sb_ref, mxb_ref, diag_mask)
            absorb(qi - 1, sa_ref[:, :2 * t], mxa_ref[...])
            absorb(qi, sb_ref[:, :2 * t], mxb_ref[...])
        else:
            s_d = jnp.where(diag_mask, sa_ref[:, :2 * t], NEG)
            absorb(qi, s_d, jnp.max(s_d, axis=0, keepdims=True))
        finish()

    for rem in range(4):
        pl.when((qi & 3) == rem)(functools.partial(tail, rem))


def _attention(qkb, v_sg, lambdas, g_diff, l, lam_init):
    s = qkb.shape[0]
    t = ATT_T
    kern = functools.partial(_attn_kernel, lam_init=lam_init, seq=s)
    return pl.pallas_call(
        kern,
        out_shape=jax.ShapeDtypeStruct((s, B_V), BF16),
        grid=(B_HEADS, s // t),
        in_specs=[pl.BlockSpec((s, LANES), lambda h, i: (0, h)),
                  pl.BlockSpec((s, LANES), lambda h, i: (0, B_HEADS + h)),
                  pl.BlockSpec((s, LANES), lambda h, i: (0, h)),
                  pl.BlockSpec((None, 4, B_HEAD_DIM), lambda h, i: (l, 0, 0)),
                  pl.BlockSpec((None, 1, LANES), lambda h, i: (l, 0, h))],
        out_specs=pl.BlockSpec((t, LANES), lambda h, i: (i, h)),
        scratch_shapes=[pltpu.VMEM((s // t, ATT_VROWS, t), BF16),
                        pltpu.VMEM((ATT_VROWS, 2 * t), F32),
                        pltpu.VMEM((1, 2 * t), F32),
                        pltpu.VMEM((t, 2 * t + ATT_PAD), F32),
                        pltpu.VMEM((t, 2 * t + ATT_PAD), F32),
                        pltpu.VMEM((1, 2 * t), F32),
                        pltpu.VMEM((1, 2 * t), F32)],
        compiler_params=_params(("arbitrary", "arbitrary"), _vmem_limit(
            ((s, LANES), BF16, 6), ((s // t, ATT_VROWS, t), BF16, 1), ((t, LANES), BF16, 6),
            ((t, 2 * t), F32, 8), ((ATT_VROWS, 2 * t), F32, 4))),
        name="diff_attn",
    )(qkb, qkb, v_sg, lambdas, g_diff)


def _mix_out_kernel(a_ref, b_ref, ga_ref, gb_ref, wa_ref, wb_ref, wo_ref, x_ref, g_ref, gt_ref, gn_ref, sc_ref,
                    sh_ref, o_ref, hn_ref, m_ref):
    a = a_ref[...]
    b = b_ref[...]
    for cs in _col_subtiles(m_ref.shape[1]):
        ya = _dot(a, wa_ref[:, cs])
        yb = _dot(b, wb_ref[:, cs])
        m_ref[:, cs] = (ga_ref[:, cs].astype(F32) * ya + gb_ref[:, cs].astype(F32) * yb).astype(m_ref.dtype)
    y = _dot(m_ref[...], wo_ref[...])
    _residual_epilogue(y, x_ref, g_ref, gt_ref, (gn_ref, sc_ref, sh_ref), o_ref, hn_ref)


def _mix_out(ya_in, yb_in, v_sg, w_a, w_b, w_out, x, g_post, mod, g_next, l):
    s, d = x.shape
    tm = min(MXU_COLS, s)
    row = lambda c: pl.BlockSpec((tm, d), lambda i: (i, c))
    weight = pl.BlockSpec((None, d, d), lambda i: (l, 0, 0), pipeline_mode=pl.Buffered(1))
    vec = lambda blk: pl.BlockSpec((None, 1, d), lambda i: (l, 0, blk))
    return pl.pallas_call(
        _mix_out_kernel,
        out_shape=[jax.ShapeDtypeStruct((s, d), F32), jax.ShapeDtypeStruct((s, d), BF16)],
        grid=(s // tm,),
        in_specs=[row(0), row(0), row(1), row(2), weight, weight, weight, row(0), vec(0), vec(2), vec(0), vec(4),
                  vec(3)],
        out_specs=[row(0), row(0)],
        scratch_shapes=[pltpu.VMEM((tm, d), BF16)],
        compiler_params=_params(("parallel",), _vmem_limit(
            ((d, d), BF16, 3), ((tm, d), BF16, 11), ((tm, d), F32, 7))),
        name="mix_out",
    )(ya_in, yb_in, v_sg, v_sg, w_a, w_b, w_out, x, g_post, mod, g_next, mod, mod)


def _residual_epilogue(y, x_ref, g_ref, gt_ref, nxt_refs, o_ref, hn_ref):
    ms = jnp.mean(y * y, axis=-1, keepdims=True)
    x_new = x_ref[...] + gt_ref[...] * (y * lax.rsqrt(ms + EPS) * g_ref[...])
    o_ref[...] = x_new
    if nxt_refs is not None:
        gn_ref, sc_ref, sh_ref = nxt_refs
        hn_ref[...] = _modulated_norm(x_new, gn_ref[...], sc_ref[...], sh_ref[...]).astype(hn_ref.dtype)


def _proj_res_kernel(a_ref, w_ref, x_ref, g_ref, gt_ref, *rest, with_next):
    if with_next:
        *nxt_refs, o_ref, hn_ref = rest
    else:
        (o_ref,), nxt_refs, hn_ref = rest, None, None
    _residual_epilogue(_dot(a_ref[...], w_ref[...]), x_ref, g_ref, gt_ref, nxt_refs, o_ref, hn_ref)


def _proj_res(a, w, x, gains, mod, l, gt_blk, tm, name, nxt=None):
    s, k = a.shape
    d = w.shape[2]
    tm = min(tm, s)
    rows = lambda width: pl.BlockSpec((tm, width), lambda i: (i, 0))
    vec = lambda arr, lay, blk: (arr, pl.BlockSpec((None, 1, d), lambda i: (lay, 0, blk)))
    operands = [(a, rows(k)),
                (w, pl.BlockSpec((None, k, d), lambda i: (l, 0, 0), pipeline_mode=pl.Buffered(1))),
                (x, rows(d)), vec(gains, l, 0), vec(mod, l, gt_blk)]
    out_shape = [jax.ShapeDtypeStruct((s, d), F32)]
    out_specs = [rows(d)]
    if nxt is not None:
        n_gains, n_l, n_sc, n_sh = nxt
        operands += [vec(n_gains, n_l, 0), vec(mod, n_l, n_sc), vec(mod, n_l, n_sh)]
        out_shape.append(jax.ShapeDtypeStruct((s, d), BF16))
        out_specs.append(rows(d))
    outs = pl.pallas_call(
        functools.partial(_proj_res_kernel, with_next=nxt is not None),
        out_shape=out_shape,
        grid=(s // tm,),
        in_specs=[spec for _, spec in operands],
        out_specs=out_specs,
        compiler_params=_params(("parallel",), _vmem_limit(
            ((tm, k), BF16, 2), ((k, d), BF16, 1), ((tm, d), F32, 10), ((tm, d), BF16, 2))),
        name=name,
    )(*[arr for arr, _ in operands])
    return outs if nxt is not None else (outs[0], None)


def _ffn_up_kernel(h_ref, wg_ref, wu_ref, cw_ref, cb_ref, o_ref, wgb_ref, wub_ref, buf_ref, *, tm):
    _cast_weight(wg_ref, wgb_ref)
    _cast_weight(wu_ref, wub_ref)

    @pl.when(_first_row_tile())
    def _():
        buf_ref[0:SUBLANES, :] = jnp.zeros((SUBLANES, buf_ref.shape[1]), F32)

    h = h_ref[...]
    for cs in _col_subtiles(o_ref.shape[1]):
        g = _causal_conv(_dot(h, wgb_ref[:, cs]), buf_ref, cw_ref, cb_ref, cs, tm)
        o_ref[:, cs] = (_silu(g) * _dot(h, wub_ref[:, cs])).astype(o_ref.dtype)


def _ffn_up(h, w_gate, w_up, conv_w, conv_b, l):
    s, d = h.shape
    n = w_gate.shape[2]
    tm, tn = min(1024, s), 512
    kconv = conv_w.shape[1]
    kern = functools.partial(_ffn_up_kernel, tm=tm)
    return pl.pallas_call(
        kern,
        out_shape=jax.ShapeDtypeStruct((s, n), BF16),
        grid=(n // tn, s // tm),
        in_specs=[pl.BlockSpec((tm, d), lambda j, i: (i, 0)),
                  pl.BlockSpec((None, d, tn), lambda j, i: (l, 0, j)),
                  pl.BlockSpec((None, d, tn), lambda j, i: (l, 0, j)),
                  pl.BlockSpec((None, kconv, tn), lambda j, i: (l, 0, j)),
                  pl.BlockSpec((None, 1, tn), lambda j, i: (l, 0, j))],
        out_specs=pl.BlockSpec((tm, tn), lambda j, i: (i, j)),
        scratch_shapes=[pltpu.VMEM((d, tn), BF16), pltpu.VMEM((d, tn), BF16),
                        pltpu.VMEM((2 * SUBLANES, tn), F32)],
        compiler_params=_params(("parallel", "arbitrary"), _vmem_limit(
            ((tm, d), BF16, 2), ((d, tn), F32, 4), ((d, tn), BF16, 2), ((tm, tn), BF16, 2),
            ((tm, tn), F32, 6))),
        name="ffn_up",
    )(h, w_gate, w_up, conv_w, conv_b)


def _rope_tables(positions):
    half = ROPE_DIM // 2
    inv = jnp.power(ROPE_THETA, -(jnp.arange(half, dtype=F32) * 2.0 / ROPE_DIM))
    ang = positions.astype(F32)[:, None] * inv
    cos, sin = jnp.cos(ang), jnp.sin(ang)
    s = positions.shape[0]
    rest = B_HEAD_DIM - ROPE_DIM
    reps = LANES // B_HEAD_DIM
    ca = jnp.tile(jnp.concatenate([cos, cos, jnp.ones((s, rest), F32)], axis=1), (1, reps))
    cb = jnp.tile(jnp.concatenate([-sin, jnp.zeros((s, B_HEAD_DIM - half), F32)], axis=1), (1, reps))
    cc = jnp.tile(jnp.concatenate([jnp.zeros((s, half), F32), sin, jnp.zeros((s, rest), F32)], axis=1),
                  (1, reps))
    return ca, cb, cc


def kernel(x, c, positions, w_ada, b_ada, g_pre_mix, g_post_mix, g_pre_ffn, g_post_ffn, w_in, conv_qk_w, conv_qk_b, b_igate, b_fgate, g_mlstm, lambdas, g_diff, w_a, w_b, w_out, w_gate, w_up, conv_ffn_w, conv_ffn_b, w_down):
    batch, s, d = x.shape
    depth = w_in.shape[0]
    assert batch == 1 and s % 1024 == 0 and d == M_V == B_V
    xs = x.reshape(s, d)
    tm_proj = min(1024, s)

    mod = _adaln(c.reshape(d, 1), w_ada, b_ada)
    ca, cb, cc = _rope_tables(positions[0])
    rope_specs = [pl.BlockSpec((tm_proj, LANES), lambda j, i: (i, 0))] * 3

    def gains(g):
        return g.reshape(depth, 1, g.shape[1])

    g_pre_mix, g_post_mix, g_pre_ffn, g_post_ffn = map(gains, (g_pre_mix, g_post_mix, g_pre_ffn, g_post_ffn))
    g_mlstm3, g_diff3 = gains(g_mlstm), gains(g_diff)
    conv_qk_b3, conv_ffn_b3 = gains(conv_qk_b), gains(conv_ffn_b)
    o_vm = 2 * M_QK
    o_om = o_vm + M_V
    o_gates = o_om + M_V
    o_tail = o_gates + 2 * M_HEADS
    gate_bias = jnp.concatenate([b_igate, b_fgate, jnp.zeros((depth, LANES - 2 * M_HEADS), F32)],
                                axis=1).reshape(depth, 1, LANES)
    assert math.log2(M_QK_DIM) % 2 == 0
    w_in_t = jnp.swapaxes(w_in, 1, 2)
    w_head = w_in_t[:, :o_gates + LANES, :].astype(BF16)
    w_tail = _rebase_rows(w_in_t, o_tail)
    w_a16, w_b16, w_out16, w_down16 = (w.astype(BF16) for w in (w_a, w_b, w_out, w_down))
    tn, tw = 1024, 2048

    h = _prenorm(xs, g_pre_mix, mod, 0, 1, 0)
    for l in range(depth):
        lam_init = 0.8 - 0.6 * math.exp(-0.3 * l)

        qk = _proj(functools.partial(_proj_conv_kernel, tm=tm_proj), h, w_head, l, 0, 2 * M_QK, BF16, tn=tn,
                   extra=(conv_qk_w, conv_qk_b3),
                   extra_specs=[pl.BlockSpec((None, conv_qk_w.shape[1], tn), lambda j, i: (l, 0, j)),
                                pl.BlockSpec((None, 1, tn), lambda j, i: (l, 0, j))],
                   scratch=[pltpu.VMEM((2 * SUBLANES, tn), F32)],
                   extra_vmem=(((tm_proj, tn), F32, 3),), name="proj_conv")
        v_so = _proj(functools.partial(_proj_mixed_kernel, n_plain=M_V // tw), h, w_head, l, o_vm, 2 * M_V,
                     BF16, tn=tw, name="proj_v_so")
        gates = _proj(_proj_plain_kernel, h, w_head, l, o_gates, LANES, F32, tn=LANES, name="proj_gates")
        qkb = _proj(functools.partial(_proj_rope_kernel, n_qtiles=B_QK // tw, qscale=B_HEAD_DIM ** -0.5 * LOG2E),
                    h, w_tail, l, 0, 2 * B_QK, BF16, tn=tw, extra=(ca, cb, cc), extra_specs=rope_specs,
                    extra_vmem=(((tm_proj, LANES), F32, 9),), name="proj_rope")
        v_sg = _proj(functools.partial(_proj_mixed_kernel, n_plain=B_V // tw), h, w_tail, l, 2 * B_QK,
                     B_V + 2 * d, BF16, tn=tw, name="proj_v_sg")

        y_a_in = _mlstm(qk, v_so, gates, gate_bias, g_mlstm3, l)
        y_b_in = _attention(qkb, v_sg, lambdas, g_diff3, l, lam_init)
        xs, h = _mix_out(y_a_in, y_b_in, v_sg, w_a16, w_b16, w_out16, xs, g_post_mix, mod, g_pre_ffn, l)

        act = _ffn_up(h, w_gate, w_up, conv_ffn_w, conv_ffn_b3, l)
        nxt = (g_pre_mix, l + 1, 1, 0) if l + 1 < depth else None
        xs, h = _proj_res(act, w_down16, xs, g_post_ffn, mod, l, 5, 256, "ffn_down", nxt=nxt)

    return xs.reshape(batch, s, d)
```

```python
import functools
import math

import jax
import jax.numpy as jnp
from jax import lax
from jax.experimental import pallas as pl
from jax.experimental.pallas import tpu as pltpu

F32 = jnp.float32
BF16 = jnp.bfloat16

M_HEADS = 4
M_QK_DIM = 256
M_V_DIM = 512
M_QK = M_HEADS * M_QK_DIM
M_V = M_HEADS * M_V_DIM
B_HEADS = 16
B_HEAD_DIM = 64
B_V_DIM = 2 * B_HEAD_DIM
B_QK = B_HEADS * 2 * B_HEAD_DIM
B_V = B_HEADS * B_V_DIM
CHUNK = 64
ROPE_THETA = 500000.0
ROPE_DIM = B_HEAD_DIM // 4
EPS = 1e-6

LANES = 128
SUBLANES = 8
MXU_COLS = 256
V7X_VMEM_REQUEST_CAP = 60000 * 1024
V7X_COMPILER_SCRATCH = 8 * 1024 * 1024

MLSTM_L = 256
ATT_T = 512
ATT_VROWS = B_V_DIM + 16
ATT_PAD = LANES
LOG2E = math.log2(math.e)
NEG = -1e30


def _nbytes(shape, dtype):
    return math.prod(shape) * jnp.dtype(dtype).itemsize


def _vmem_limit(*terms):
    total = sum(_nbytes(s, d) * n for s, d, n in terms) + V7X_COMPILER_SCRATCH
    return int(min(V7X_VMEM_REQUEST_CAP, total))


def _params(sem, limit):
    return pltpu.CompilerParams(dimension_semantics=sem, vmem_limit_bytes=limit)


def _silu(x):
    return x * jax.nn.sigmoid(x)


def _dot(a, b):
    return jnp.dot(a, b, preferred_element_type=F32)


def _dot_nt(a, b):
    return lax.dot_general(a, b, (((1,), (1,)), ((), ())), preferred_element_type=F32)


def _dot_tn(a, b):
    return lax.dot_general(a, b, (((0,), (0,)), ((), ())), preferred_element_type=F32)


def _adaln_kernel(c_ref, w_ref, b_ref, o_ref):
    c = c_ref[...]
    o_ref[...] = jnp.sum(_silu(c) * w_ref[...], axis=0, keepdims=True) + b_ref[...]


def _adaln(c_col, w_ada, b_ada):
    depth, d, n = w_ada.shape
    tn = 1024
    return pl.pallas_call(
        _adaln_kernel,
        out_shape=jax.ShapeDtypeStruct((depth, 1, n), F32),
        grid=(depth, n // tn),
        in_specs=[pl.BlockSpec((d, 1), lambda l, j: (0, 0)),
                  pl.BlockSpec((None, d, tn), lambda l, j: (l, 0, j)),
                  pl.BlockSpec((None, 1, tn), lambda l, j: (l, 0, j))],
        out_specs=pl.BlockSpec((None, 1, tn), lambda l, j: (l, 0, j)),
        compiler_params=_params(("parallel", "parallel"),
                                _vmem_limit(((d, LANES), F32, 2), ((d, tn), F32, 3))),
        name="adaln",
    )(c_col, w_ada, b_ada.reshape(depth, 1, n))


def _modulated_norm(x, g, sc, sh):
    ms = jnp.mean(x * x, axis=-1, keepdims=True)
    return (x * lax.rsqrt(ms + EPS) * g) * (1.0 + sc) + sh


def _prenorm_kernel(x_ref, g_ref, sc_ref, sh_ref, o_ref):
    o_ref[...] = _modulated_norm(x_ref[...], g_ref[...], sc_ref[...], sh_ref[...]).astype(o_ref.dtype)


def _prenorm(x, gains, mod, l, sc_blk, sh_blk):
    s, d = x.shape
    tm = min(512, s)
    return pl.pallas_call(
        _prenorm_kernel,
        out_shape=jax.ShapeDtypeStruct((s, d), BF16),
        grid=(s // tm,),
        in_specs=[pl.BlockSpec((tm, d), lambda i: (i, 0)),
                  pl.BlockSpec((None, 1, d), lambda i: (l, 0, 0)),
                  pl.BlockSpec((None, 1, d), lambda i: (l, 0, sc_blk)),
                  pl.BlockSpec((None, 1, d), lambda i: (l, 0, sh_blk))],
        out_specs=pl.BlockSpec((tm, d), lambda i: (i, 0)),
        compiler_params=_params(("parallel",), _vmem_limit(((tm, d), F32, 4), ((tm, d), BF16, 2))),
        name="prenorm",
    )(x, gains, mod, mod)


REBASE_ROWS = 512


def _rebase_kernel(cur_ref, nxt_ref, o_ref):
    skip = nxt_ref.shape[0]
    o_ref[...] = jnp.concatenate([cur_ref[skip:, :], nxt_ref[...]], axis=0).astype(o_ref.dtype)


def _rebase_rows(w_t, row0):
    depth, rows, d = w_t.shape
    r = REBASE_ROWS
    base = row0 - SUBLANES
    n = rows - row0
    assert base % r == 0 and n % r == 0
    blk0, per = base // r, r // SUBLANES
    return pl.pallas_call(
        _rebase_kernel,
        out_shape=jax.ShapeDtypeStruct((depth, n, d), BF16),
        grid=(depth, n // r),
        in_specs=[pl.BlockSpec((None, r, d), lambda l, j: (l, blk0 + j, 0)),
                  pl.BlockSpec((None, SUBLANES, d), lambda l, j: (l, (blk0 + j + 1) * per, 0))],
        out_specs=pl.BlockSpec((None, r, d), lambda l, j: (l, j, 0)),
        compiler_params=_params(("parallel", "parallel"), _vmem_limit(
            ((r, d), F32, 3), ((SUBLANES, d), F32, 2), ((r, d), BF16, 3))),
        name="rebase_w_in",
    )(w_t, w_t)


def _first_row_tile():
    return pl.program_id(1) == 0


def _cast_weight(w_ref, wb_ref):
    @pl.when(_first_row_tile())
    def _():
        wb_ref[...] = w_ref[...].astype(BF16)


def _causal_conv(acc, buf_ref, cw_ref, cb_ref, cs, tm):
    kconv = cw_ref.shape[0]
    buf_ref[SUBLANES:2 * SUBLANES, cs] = acc[0:SUBLANES, :]
    out = cb_ref[:, cs] + cw_ref[kconv - 1:kconv, cs] * acc
    for j in range(kconv - 1):
        dist = kconv - 1 - j
        top = buf_ref[SUBLANES - dist:2 * SUBLANES - dist, cs]
        shifted = jnp.concatenate([top, pltpu.roll(acc, dist, 0)[SUBLANES:, :]], axis=0)
        out = out + cw_ref[j:j + 1, cs] * shifted
    buf_ref[0:SUBLANES, cs] = acc[tm - SUBLANES:tm, :]
    return out


def _col_subtiles(n):
    sub = min(MXU_COLS, n)
    return [slice(c * sub, (c + 1) * sub) for c in range(n // sub)]


def _proj_conv_kernel(h_ref, w_ref, cw_ref, cb_ref, o_ref, buf_ref, *, tm):
    @pl.when(_first_row_tile())
    def _():
        buf_ref[0:SUBLANES, :] = jnp.zeros((SUBLANES, buf_ref.shape[1]), F32)

    h = h_ref[...]
    for cs in _col_subtiles(o_ref.shape[1]):
        out = _causal_conv(_dot_nt(h, w_ref[cs, :]), buf_ref, cw_ref, cb_ref, cs, tm)
        o_ref[:, cs] = _silu(out).astype(o_ref.dtype)


def _proj_plain_kernel(h_ref, w_ref, o_ref):
    h = h_ref[...]
    for cs in _col_subtiles(o_ref.shape[1]):
        o_ref[:, cs] = _dot_nt(h, w_ref[cs, :]).astype(o_ref.dtype)


def _proj_mixed_kernel(h_ref, w_ref, o_ref, *, n_plain):
    j = pl.program_id(0)

    @pl.when(j < n_plain)
    def _():
        _proj_plain_kernel(h_ref, w_ref, o_ref)

    @pl.when(j >= n_plain)
    def _():
        h = h_ref[...]
        for cs in _col_subtiles(o_ref.shape[1]):
            o_ref[:, cs] = jax.nn.sigmoid(_dot_nt(h, w_ref[cs, :])).astype(o_ref.dtype)


def _proj_rope_kernel(h_ref, w_ref, ca_ref, cb_ref, cc_ref, o_ref, *, n_qtiles, qscale):
    scale = jnp.where(pl.program_id(0) < n_qtiles, qscale, 1.0).astype(F32)
    ca = ca_ref[...] * scale
    cb = cb_ref[...] * scale
    cc = cc_ref[...] * scale
    h = h_ref[...]
    for cs in _col_subtiles(o_ref.shape[1]):
        acc = _dot_nt(h, w_ref[cs, :])
        for s in range(acc.shape[1] // LANES):
            xs = acc[:, s * LANES:(s + 1) * LANES]
            r = (xs * ca + pltpu.roll(xs, LANES - ROPE_DIM // 2, 1) * cb
                 + pltpu.roll(xs, ROPE_DIM // 2, 1) * cc)
            o_ref[:, cs.start + s * LANES:cs.start + (s + 1) * LANES] = r.astype(o_ref.dtype)


def _proj(kernel, h, w, l, col0, n, out_dtype, *, tn, extra=(), extra_specs=(), scratch=(), extra_vmem=(),
          name):
    s, d = h.shape
    tm = min(1024, s)
    blk0 = col0 // tn
    assert blk0 * tn == col0 and n % tn == 0
    return pl.pallas_call(
        kernel,
        out_shape=jax.ShapeDtypeStruct((s, n), out_dtype),
        grid=(n // tn, s // tm),
        in_specs=[pl.BlockSpec((tm, d), lambda j, i: (i, 0)),
                  pl.BlockSpec((None, tn, d), lambda j, i: (l, blk0 + j, 0))] + list(extra_specs),
        out_specs=pl.BlockSpec((tm, tn), lambda j, i: (i, j)),
        scratch_shapes=list(scratch),
        compiler_params=_params(("parallel", "arbitrary"), _vmem_limit(
            ((tm, d), BF16, 2), ((d, tn), BF16, 2), ((tm, tn), out_dtype, 2),
            ((tm, MXU_COLS), F32, 8), *extra_vmem)),
        name=name,
    )(h, w, *extra)


def _mlstm_kernel(qk_ref, v_ref, gate_ref, bias_ref, so_ref, g_ref, o_ref, st_ref, m_ref):
    L = MLSTM_L

    @pl.when(pl.program_id(0) == 0)
    def _():
        st_ref[...] = jnp.zeros(st_ref.shape, F32)
        m_ref[...] = jnp.zeros(m_ref.shape, F32)

    ipre = gate_ref[...] + bias_ref[...]
    fpre = pltpu.roll(ipre, LANES - M_HEADS, 1)
    logf = jnp.minimum(fpre, 0.0) - jnp.log1p(jnp.exp(-jnp.abs(fpre)))

    row = lax.broadcasted_iota(jnp.int32, (L, L), 0)
    col = lax.broadcasted_iota(jnp.int32, (L, L), 1)
    tri = col <= row
    tri_b = jnp.where(tri, 1.0, 0.0).astype(BF16)
    hi = logf.astype(BF16)
    r1 = logf - hi.astype(F32)
    mid = r1.astype(BF16)
    lo = (r1 - mid.astype(F32)).astype(BF16)
    b = _dot(tri_b, hi) + _dot(tri_b, mid) + _dot(tri_b, lo)
    a = ipre - b
    a_t = a.T
    m_all = m_ref[...]
    lane = lax.broadcasted_iota(jnp.int32, (1, LANES), 1)
    ones_col = jnp.where(lax.broadcasted_iota(jnp.int32, (L, LANES), 1) == 0, 1.0, 0.0).astype(BF16)
    m_next = m_all

    for h in range(M_HEADS):
        a_row = a_t[h:h + 1, :]
        a_col = a[:, h:h + 1]
        b_col = b[:, h:h + 1]
        m_prev = m_all[:, h:h + 1]
        amask = jnp.where(tri, a_row, NEG)
        m_run = jnp.maximum(jnp.max(amask, axis=1, keepdims=True), m_prev)
        dmat = jnp.exp(amask - m_run)
        w_inter = jnp.exp(m_prev - m_run)
        q = qk_ref[:, h * M_QK_DIM:(h + 1) * M_QK_DIM] * jnp.asarray(M_QK_DIM ** -0.5, BF16)
        k = qk_ref[:, M_QK + h * M_QK_DIM:M_QK + (h + 1) * M_QK_DIM]
        v_ext = jnp.concatenate([v_ref[:, h * M_V_DIM:(h + 1) * M_V_DIM], ones_col], axis=1)
        sd = (_dot_nt(q, k) * dmat).astype(BF16)
        st = st_ref[h]
        tot = w_inter * _dot(q, st.astype(BF16)) + _dot(sd, v_ext)
        num = tot[:, :M_V_DIM]
        den = tot[:, M_V_DIM:M_V_DIM + 1]
        hval = num / jnp.maximum(jnp.abs(den), jnp.exp(-(b_col + m_run)))
        m_last = m_run[L - 1:L, :]
        w_s = jnp.exp(a_col - m_last)
        decay = jnp.exp(m_prev - m_last)
        rhs = (w_s * v_ext.astype(F32)).astype(BF16)
        st_ref[h] = decay * st + _dot_tn(k, rhs)
        m_next = jnp.where(lane == h, b_col[L - 1:L, :] + m_last, m_next)
        ms = jnp.mean(hval * hval, axis=-1, keepdims=True)
        sl = slice(h * M_V_DIM, (h + 1) * M_V_DIM)
        hn = hval * lax.rsqrt(ms + EPS) * g_ref[:, sl]
        o_ref[:, sl] = (so_ref[:, sl].astype(F32) * hn).astype(o_ref.dtype)

    m_ref[...] = m_next


def _mlstm(qk, v_so, gates, gate_bias, g_mlstm, l):
    s = qk.shape[0]
    L = MLSTM_L
    st_shape = (M_HEADS, M_QK_DIM, M_V_DIM + LANES)
    return pl.pallas_call(
        _mlstm_kernel,
        out_shape=jax.ShapeDtypeStruct((s, M_V), BF16),
        grid=(s // L,),
        in_specs=[pl.BlockSpec((L, 2 * M_QK), lambda c: (c, 0)),
                  pl.BlockSpec((L, M_V), lambda c: (c, 0)),
                  pl.BlockSpec((L, LANES), lambda c: (c, 0)),
                  pl.BlockSpec((None, 1, LANES), lambda c: (l, 0, 0)),
                  pl.BlockSpec((L, M_V), lambda c: (c, 1)),
                  pl.BlockSpec((None, 1, M_V), lambda c: (l, 0, 0))],
        out_specs=pl.BlockSpec((L, M_V), lambda c: (c, 0)),
        scratch_shapes=[pltpu.VMEM(st_shape, F32), pltpu.VMEM((1, LANES), F32)],
        compiler_params=_params(("arbitrary",), _vmem_limit(
            ((L, 2 * M_QK), BF16, 2), ((L, M_V), BF16, 6), ((L, LANES), F32, 2),
            (st_shape, F32, 2), ((L, M_V + LANES), F32, 8), ((L, L), F32, 8))),
        name="mlstm",
    )(qk, v_so, gates, gate_bias, v_so, g_mlstm)


def _attn_kernel(q_ref, k_ref, v_ref, lam_ref, g_ref, o_ref, vt_ref, acc_ref, m_ref, sa_ref, sb_ref,
                 mxa_ref, mxb_ref, *, lam_init, seq):
    t = ATT_T
    qi = pl.program_id(1)

    @pl.when(qi == 0)
    def _():
        ones_row = jnp.where(lax.broadcasted_iota(jnp.int32, (ATT_VROWS - B_V_DIM, t), 0) == 0, 1.0, 0.0)
        for kb in range(seq // t):
            v_t = v_ref[kb * t:(kb + 1) * t, :].astype(F32).T
            vt_ref[kb] = jnp.concatenate([v_t, ones_row], axis=0).astype(BF16)

    def query_block(qb):
        q = q_ref[pl.ds(pl.multiple_of(qb * t, t), t), :]
        qlane = lax.broadcasted_iota(jnp.int32, q.shape, 1)
        zero = jnp.zeros_like(q)
        return jnp.concatenate([jnp.where(qlane < B_HEAD_DIM, q, zero),
                                jnp.where(qlane >= B_HEAD_DIM, q, zero)], axis=0)

    qcat = query_block(qi)
    shift = CHUNK.bit_length() - 1
    kchunk = lax.broadcasted_iota(jnp.int32, (t, 2 * t), 0) >> shift
    qchunk = (lax.broadcasted_iota(jnp.int32, (t, 2 * t), 1) & (t - 1)) >> shift
    diag_mask = kchunk <= qchunk

    def scores(kb, s_ref, mx_ref, mask=None, qc=qcat):
        kblk = k_ref[pl.ds(pl.multiple_of(kb * t, t), t), :]
        s_t = _dot_nt(kblk, qc)
        if mask is not None:
            s_t = jnp.where(mask, s_t, NEG)
        s_ref[:, :2 * t] = s_t
        mx_ref[...] = jnp.max(s_t, axis=0, keepdims=True)

    def absorb(kb, s_t, mx):
        m_old = m_ref[...]
        m_new = jnp.maximum(m_old, mx)
        p = jnp.exp2(s_t - m_new).astype(BF16)
        acc_ref[...] = jnp.exp2(m_old - m_new) * acc_ref[...] + _dot(vt_ref[kb], p)
        m_ref[...] = m_new

    acc_ref[...] = jnp.zeros(acc_ref.shape, F32)
    m_ref[...] = jnp.full(m_ref.shape, NEG, F32)

    @pl.when(qi == 0)
    def _():
        scores(0, sa_ref, mxa_ref)

    def pair(kb):
        scores(kb + 1, sb_ref, mxb_ref)
        absorb(kb, sa_ref[:, :2 * t], mxa_ref[...])
        scores(kb + 2, sa_ref, mxa_ref)
        absorb(kb + 1, sb_ref[:, :2 * t], mxb_ref[...])

    def body(it, carry):
        for u in range(0, 8, 2):
            pair(8 * it + u)
        return carry

    lax.fori_loop(0, qi >> 3, body, 0)

    @pl.when((qi & 4) == 4)
    def _():
        pair(qi & ~7)
        pair((qi & ~7) + 2)

    def finish():
        lf = lam_ref[...]
        lam = (jnp.exp(jnp.sum(lf[0:1] * lf[1:2], axis=1, keepdims=True))
               - jnp.exp(jnp.sum(lf[2:3] * lf[3:4], axis=1, keepdims=True)) + lam_init)
        acc = acc_ref[...]
        o1 = acc[:B_V_DIM, :t] / acc[B_V_DIM:B_V_DIM + 1, :t]
        o2 = acc[:B_V_DIM, t:] / acc[B_V_DIM:B_V_DIM + 1, t:]
        o_t = o1 - lam * o2
        ms = jnp.mean(o_t * o_t, axis=0, keepdims=True)
        o = (o_t * lax.rsqrt(ms + EPS)).T
        o_ref[...] = (o * (g_ref[...] * (1.0 - lam_init))).astype(o_ref.dtype)
        q_next = jnp.minimum(qi + 1, pl.num_programs(1) - 1)
        scores(0, sa_ref, mxa_ref, qc=query_block(q_next))

    def tail(rem):
        if rem & 2:
            pair(qi - rem)
        if rem & 1:
            scores(qi, sb_ref, mxb_ref, diag_mask)
            absorb(qi - 1, sa_ref[:, :2 * t], mxa_ref[...])
            absorb(qi, sb_ref[:, :2 * t], mxb_ref[...])
        else:
            s_d = jnp.where(diag_mask, sa_ref[:, :2 * t], NEG)
            absorb(qi, s_d, jnp.max(s_d, axis=0, keepdims=True))
        finish()

    for rem in range(4):
        pl.when((qi & 3) == rem)(functools.partial(tail, rem))


def _attention(qkb, v_sg, lambdas, g_diff, l, lam_init):
    s = qkb.shape[0]
    t = ATT_T
    kern = functools.partial(_attn_kernel, lam_init=lam_init, seq=s)
    return pl.pallas_call(
        kern,
        out_shape=jax.ShapeDtypeStruct((s, B_V), BF16),
        grid=(B_HEADS, s // t),
        in_specs=[pl.BlockSpec((s, LANES), lambda h, i: (0, h)),
                  pl.BlockSpec((s, LANES), lambda h, i: (0, B_HEADS + h)),
                  pl.BlockSpec((s, LANES), lambda h, i: (0, h)),
                  pl.BlockSpec((None, 4, B_HEAD_DIM), lambda h, i: (l, 0, 0)),
                  pl.BlockSpec((None, 1, LANES), lambda h, i: (l, 0, h))],
        out_specs=pl.BlockSpec((t, LANES), lambda h, i: (i, h)),
        scratch_shapes=[pltpu.VMEM((s // t, ATT_VROWS, t), BF16),
                        pltpu.VMEM((ATT_VROWS, 2 * t), F32),
                        pltpu.VMEM((1, 2 * t), F32),
                        pltpu.VMEM((t, 2 * t + ATT_PAD), F32),
                        pltpu.VMEM((t, 2 * t + ATT_PAD), F32),
                        pltpu.VMEM((1, 2 * t), F32),
                        pltpu.VMEM((1, 2 * t), F32)],
        compiler_params=_params(("arbitrary", "arbitrary"), _vmem_limit(
            ((s, LANES), BF16, 6), ((s // t, ATT_VROWS, t), BF16, 1), ((t, LANES), BF16, 6),
            ((t, 2 * t), F32, 8), ((ATT_VROWS, 2 * t), F32, 4))),
        name="diff_attn",
    )(qkb, qkb, v_sg, lambdas, g_diff)


def _mix_out_kernel(a_ref, b_ref, ga_ref, gb_ref, wa_ref, wb_ref, wo_ref, x_ref, g_ref, gt_ref, gn_ref, sc_ref,
                    sh_ref, o_ref, hn_ref, m_ref):
    a = a_ref[...]
    b = b_ref[...]
    for cs in _col_subtiles(m_ref.shape[1]):
        ya = _dot(a, wa_ref[:, cs])
        yb = _dot(b, wb_ref[:, cs])
        m_ref[:, cs] = (ga_ref[:, cs].astype(F32) * ya + gb_ref[:, cs].astype(F32) * yb).astype(m_ref.dtype)
    y = _dot(m_ref[...], wo_ref[...])
    _residual_epilogue(y, x_ref, g_ref, gt_ref, (gn_ref, sc_ref, sh_ref), o_ref, hn_ref)


def _mix_out(ya_in, yb_in, v_sg, w_a, w_b, w_out, x, g_post, mod, g_next, l):
    s, d = x.shape
    tm = min(MXU_COLS, s)
    row = lambda c: pl.BlockSpec((tm, d), lambda i: (i, c))
    weight = pl.BlockSpec((None, d, d), lambda i: (l, 0, 0), pipeline_mode=pl.Buffered(1))
    vec = lambda blk: pl.BlockSpec((None, 1, d), lambda i: (l, 0, blk))
    return pl.pallas_call(
        _mix_out_kernel,
        out_shape=[jax.ShapeDtypeStruct((s, d), F32), jax.ShapeDtypeStruct((s, d), BF16)],
        grid=(s // tm,),
        in_specs=[row(0), row(0), row(1), row(2), weight, weight, weight, row(0), vec(0), vec(2), vec(0), vec(4),
                  vec(3)],
        out_specs=[row(0), row(0)],
        scratch_shapes=[pltpu.VMEM((tm, d), BF16)],
        compiler_params=_params(("parallel",), _vmem_limit(
            ((d, d), BF16, 3), ((tm, d), BF16, 11), ((tm, d), F32, 7))),
        name="mix_out",
    )(ya_in, yb_in, v_sg, v_sg, w_a, w_b, w_out, x, g_post, mod, g_next, mod, mod)


def _residual_epilogue(y, x_ref, g_ref, gt_ref, nxt_refs, o_ref, hn_ref):
    ms = jnp.mean(y * y, axis=-1, keepdims=True)
    x_new = x_ref[...] + gt_ref[...] * (y * lax.rsqrt(ms + EPS) * g_ref[...])
    o_ref[...] = x_new
    if nxt_refs is not None:
        gn_ref, sc_ref, sh_ref = nxt_refs
        hn_ref[...] = _modulated_norm(x_new, gn_ref[...], sc_ref[...], sh_ref[...]).astype(hn_ref.dtype)


def _proj_res_kernel(a_ref, w_ref, x_ref, g_ref, gt_ref, *rest, with_next):
    if with_next:
        *nxt_refs, o_ref, hn_ref = rest
    else:
        (o_ref,), nxt_refs, hn_ref = rest, None, None
    _residual_epilogue(_dot(a_ref[...], w_ref[...]), x_ref, g_ref, gt_ref, nxt_refs, o_ref, hn_ref)


def _proj_res(a, w, x, gains, mod, l, gt_blk, tm, name, nxt=None):
    s, k = a.shape
    d = w.shape[2]
    tm = min(tm, s)
    rows = lambda width: pl.BlockSpec((tm, width), lambda i: (i, 0))
    vec = lambda arr, lay, blk: (arr, pl.BlockSpec((None, 1, d), lambda i: (lay, 0, blk)))
    operands = [(a, rows(k)),
                (w, pl.BlockSpec((None, k, d), lambda i: (l, 0, 0), pipeline_mode=pl.Buffered(1))),
                (x, rows(d)), vec(gains, l, 0), vec(mod, l, gt_blk)]
    out_shape = [jax.ShapeDtypeStruct((s, d), F32)]
    out_specs = [rows(d)]
    if nxt is not None:
        n_gains, n_l, n_sc, n_sh = nxt
        operands += [vec(n_gains, n_l, 0), vec(mod, n_l, n_sc), vec(mod, n_l, n_sh)]
        out_shape.append(jax.ShapeDtypeStruct((s, d), BF16))
        out_specs.append(rows(d))
    outs = pl.pallas_call(
        functools.partial(_proj_res_kernel, with_next=nxt is not None),
        out_shape=out_shape,
        grid=(s // tm,),
        in_specs=[spec for _, spec in operands],
        out_specs=out_specs,
        compiler_params=_params(("parallel",), _vmem_limit(
            ((tm, k), BF16, 2), ((k, d), BF16, 1), ((tm, d), F32, 10), ((tm, d), BF16, 2))),
        name=name,
    )(*[arr for arr, _ in operands])
    return outs if nxt is not None else (outs[0], None)


def _ffn_up_kernel(h_ref, wg_ref, wu_ref, cw_ref, cb_ref, o_ref, wgb_ref, wub_ref, buf_ref, *, tm):
    _cast_weight(wg_ref, wgb_ref)
    _cast_weight(wu_ref, wub_ref)

    @pl.when(_first_row_tile())
    def _():
        buf_ref[0:SUBLANES, :] = jnp.zeros((SUBLANES, buf_ref.shape[1]), F32)

    h = h_ref[...]
    for cs in _col_subtiles(o_ref.shape[1]):
        g = _causal_conv(_dot(h, wgb_ref[:, cs]), buf_ref, cw_ref, cb_ref, cs, tm)
        o_ref[:, cs] = (_silu(g) * _dot(h, wub_ref[:, cs])).astype(o_ref.dtype)


def _ffn_up(h, w_gate, w_up, conv_w, conv_b, l):
    s, d = h.shape
    n = w_gate.shape[2]
    tm, tn = min(1024, s), 512
    kconv = conv_w.shape[1]
    kern = functools.partial(_ffn_up_kernel, tm=tm)
    return pl.pallas_call(
        kern,
        out_shape=jax.ShapeDtypeStruct((s, n), BF16),
        grid=(n // tn, s // tm),
        in_specs=[pl.BlockSpec((tm, d), lambda j, i: (i, 0)),
                  pl.BlockSpec((None, d, tn), lambda j, i: (l, 0, j)),
                  pl.BlockSpec((None, d, tn), lambda j, i: (l, 0, j)),
                  pl.BlockSpec((None, kconv, tn), lambda j, i: (l, 0, j)),
                  pl.BlockSpec((None, 1, tn), lambda j, i: (l, 0, j))],
        out_specs=pl.BlockSpec((tm, tn), lambda j, i: (i, j)),
        scratch_shapes=[pltpu.VMEM((d, tn), BF16), pltpu.VMEM((d, tn), BF16),
                        pltpu.VMEM((2 * SUBLANES, tn), F32)],
        compiler_params=_params(("parallel", "arbitrary"), _vmem_limit(
            ((tm, d), BF16, 2), ((d, tn), F32, 4), ((d, tn), BF16, 2), ((tm, tn), BF16, 2),
            ((tm, tn), F32, 6))),
        name="ffn_up",
    )(h, w_gate, w_up, conv_w, conv_b)


def _rope_tables(positions):
    half = ROPE_DIM // 2
    inv = jnp.power(ROPE_THETA, -(jnp.arange(half, dtype=F32) * 2.0 / ROPE_DIM))
    ang = positions.astype(F32)[:, None] * inv
    cos, sin = jnp.cos(ang), jnp.sin(ang)
    s = positions.shape[0]
    rest = B_HEAD_DIM - ROPE_DIM
    reps = LANES // B_HEAD_DIM
    ca = jnp.tile(jnp.concatenate([cos, cos, jnp.ones((s, rest), F32)], axis=1), (1, reps))
    cb = jnp.tile(jnp.concatenate([-sin, jnp.zeros((s, B_HEAD_DIM - half), F32)], axis=1), (1, reps))
    cc = jnp.tile(jnp.concatenate([jnp.zeros((s, half), F32), sin, jnp.zeros((s, rest), F32)], axis=1),
                  (1, reps))
    return ca, cb, cc


def kernel(x, c, positions, w_ada, b_ada, g_pre_mix, g_post_mix, g_pre_ffn, g_post_ffn, w_in, conv_qk_w, conv_qk_b, b_igate, b_fgate, g_mlstm, lambdas, g_diff, w_a, w_b, w_out, w_gate, w_up, conv_ffn_w, conv_ffn_b, w_down):
    batch, s, d = x.shape
    depth = w_in.shape[0]
    assert batch == 1 and s % 1024 == 0 and d == M_V == B_V
    xs = x.reshape(s, d)
    tm_proj = min(1024, s)

    mod = _adaln(c.reshape(d, 1), w_ada, b_ada)
    ca, cb, cc = _rope_tables(positions[0])
    rope_specs = [pl.BlockSpec((tm_proj, LANES), lambda j, i: (i, 0))] * 3

    def gains(g):
        return g.reshape(depth, 1, g.shape[1])

    g_pre_mix, g_post_mix, g_pre_ffn, g_post_ffn = map(gains, (g_pre_mix, g_post_mix, g_pre_ffn, g_post_ffn))
    g_mlstm3, g_diff3 = gains(g_mlstm), gains(g_diff)
    conv_qk_b3, conv_ffn_b3 = gains(conv_qk_b), gains(conv_ffn_b)
    o_vm = 2 * M_QK
    o_om = o_vm + M_V
    o_gates = o_om + M_V
    o_tail = o_gates + 2 * M_HEADS
    gate_bias = jnp.concatenate([b_igate, b_fgate, jnp.zeros((depth, LANES - 2 * M_HEADS), F32)],
                                axis=1).reshape(depth, 1, LANES)
    assert math.log2(M_QK_DIM) % 2 == 0
    w_in_t = jnp.swapaxes(w_in, 1, 2)
    w_head = w_in_t[:, :o_gates + LANES, :].astype(BF16)
    w_tail = _rebase_rows(w_in_t, o_tail)
    w_a16, w_b16, w_out16, w_down16 = (w.astype(BF16) for w in (w_a, w_b, w_out, w_down))
    tn, tw = 1024, 2048

    h = _prenorm(xs, g_pre_mix, mod, 0, 1, 0)
    for l in range(depth):
        lam_init = 0.8 - 0.6 * math.exp(-0.3 * l)

        qk = _proj(functools.partial(_proj_conv_kernel, tm=tm_proj), h, w_head, l, 0, 2 * M_QK, BF16, tn=tn,
                   extra=(conv_qk_w, conv_qk_b3),
                   extra_specs=[pl.BlockSpec((None, conv_qk_w.shape[1], tn), lambda j, i: (l, 0, j)),
                                pl.BlockSpec((None, 1, tn), lambda j, i: (l, 0, j))],
                   scratch=[pltpu.VMEM((2 * SUBLANES, tn), F32)],
                   extra_vmem=(((tm_proj, tn), F32, 3),), name="proj_conv")
        v_so = _proj(functools.partial(_proj_mixed_kernel, n_plain=M_V // tw), h, w_head, l, o_vm, 2 * M_V,
                     BF16, tn=tw, name="proj_v_so")
        gates = _proj(_proj_plain_kernel, h, w_head, l, o_gates, LANES, F32, tn=LANES, name="proj_gates")
        qkb = _proj(functools.partial(_proj_rope_kernel, n_qtiles=B_QK // tw, qscale=B_HEAD_DIM ** -0.5 * LOG2E),
                    h, w_tail, l, 0, 2 * B_QK, BF16, tn=tw, extra=(ca, cb, cc), extra_specs=rope_specs,
                    extra_vmem=(((tm_proj, LANES), F32, 9),), name="proj_rope")
        v_sg = _proj(functools.partial(_proj_mixed_kernel, n_plain=B_V // tw), h, w_tail, l, 2 * B_QK,
                     B_V + 2 * d, BF16, tn=tw, name="proj_v_sg")

        y_a_in = _mlstm(qk, v_so, gates, gate_bias, g_mlstm3, l)
        y_b_in = _attention(qkb, v_sg, lambdas, g_diff3, l, lam_init)
        xs, h = _mix_out(y_a_in, y_b_in, v_sg, w_a16, w_b16, w_out16, xs, g_post_mix, mod, g_pre_ffn, l)

        act = _ffn_up(h, w_gate, w_up, conv_ffn_w, conv_ffn_b3, l)
        nxt = (g_pre_mix, l + 1, 1, 0) if l + 1 < depth else None
        xs, h = _proj_res(act, w_down16, xs, g_post_ffn, mod, l, 5, 256, "ffn_down", nxt=nxt)

    return xs.reshape(batch, s, d)
```

```python
import functools
import math

import jax
import jax.numpy as jnp
from jax import lax
from jax.experimental import pallas as pl
from jax.experimental.pallas import tpu as pltpu

F32 = jnp.float32
BF16 = jnp.bfloat16

M_HEADS = 4
M_QK_DIM = 256
M_V_DIM = 512
M_QK = M_HEADS * M_QK_DIM
M_V = M_HEADS * M_V_DIM
B_HEADS = 16
B_HEAD_DIM = 64
B_V_DIM = 2 * B_HEAD_DIM
B_QK = B_HEADS * 2 * B_HEAD_DIM
B_V = B_HEADS * B_V_DIM
CHUNK = 64
ROPE_THETA = 500000.0
ROPE_DIM = B_HEAD_DIM // 4
EPS = 1e-6

LANES = 128
SUBLANES = 8
MXU_COLS = 256
V7X_VMEM_REQUEST_CAP = 60000 * 1024
V7X_COMPILER_SCRATCH = 8 * 1024 * 1024

MLSTM_L = 256
ATT_T = 512
ATT_VROWS = B_V_DIM + 16
ATT_PAD = LANES
LOG2E = math.log2(math.e)
NEG = -1e30


def _nbytes(shape, dtype):
    return math.prod(shape) * jnp.dtype(dtype).itemsize


def _vmem_limit(*terms):
    total = sum(_nbytes(s, d) * n for s, d, n in terms) + V7X_COMPILER_SCRATCH
    return int(min(V7X_VMEM_REQUEST_CAP, total))


def _params(sem, limit):
    return pltpu.CompilerParams(dimension_semantics=sem, vmem_limit_bytes=limit)


def _silu(x):
    return x * jax.nn.sigmoid(x)


def _dot(a, b):
    return jnp.dot(a, b, preferred_element_type=F32)


def _dot_nt(a, b):
    return lax.dot_general(a, b, (((1,), (1,)), ((), ())), preferred_element_type=F32)


def _dot_tn(a, b):
    return lax.dot_general(a, b, (((0,), (0,)), ((), ())), preferred_element_type=F32)


def _adaln_kernel(c_ref, w_ref, b_ref, o_ref):
    c = c_ref[...]
    o_ref[...] = jnp.sum(_silu(c) * w_ref[...], axis=0, keepdims=True) + b_ref[...]


def _adaln(c_col, w_ada, b_ada):
    depth, d, n = w_ada.shape
    tn = 1024
    return pl.pallas_call(
        _adaln_kernel,
        out_shape=jax.ShapeDtypeStruct((depth, 1, n), F32),
        grid=(depth, n // tn),
        in_specs=[pl.BlockSpec((d, 1), lambda l, j: (0, 0)),
                  pl.BlockSpec((None, d, tn), lambda l, j: (l, 0, j)),
                  pl.BlockSpec((None, 1, tn), lambda l, j: (l, 0, j))],
        out_specs=pl.BlockSpec((None, 1, tn), lambda l, j: (l, 0, j)),
        compiler_params=_params(("parallel", "parallel"),
                                _vmem_limit(((d, LANES), F32, 2), ((d, tn), F32, 3))),
        name="adaln",
    )(c_col, w_ada, b_ada.reshape(depth, 1, n))


def _modulated_norm(x, g, sc, sh):
    ms = jnp.mean(x * x, axis=-1, keepdims=True)
    return (x * lax.rsqrt(ms + EPS) * g) * (1.0 + sc) + sh


def _prenorm_kernel(x_ref, g_ref, sc_ref, sh_ref, o_ref):
    o_ref[...] = _modulated_norm(x_ref[...], g_ref[...], sc_ref[...], sh_ref[...]).astype(o_ref.dtype)


def _prenorm(x, gains, mod, l, sc_blk, sh_blk):
    s, d = x.shape
    tm = min(512, s)
    return pl.pallas_call(
        _prenorm_kernel,
        out_shape=jax.ShapeDtypeStruct((s, d), BF16),
        grid=(s // tm,),
        in_specs=[pl.BlockSpec((tm, d), lambda i: (i, 0)),
                  pl.BlockSpec((None, 1, d), lambda i: (l, 0, 0)),
                  pl.BlockSpec((None, 1, d), lambda i: (l, 0, sc_blk)),
                  pl.BlockSpec((None, 1, d), lambda i: (l, 0, sh_blk))],
        out_specs=pl.BlockSpec((tm, d), lambda i: (i, 0)),
        compiler_params=_params(("parallel",), _vmem_limit(((tm, d), F32, 4), ((tm, d), BF16, 2))),
        name="prenorm",
    )(x, gains, mod, mod)


REBASE_ROWS = 512


def _rebase_kernel(cur_ref, nxt_ref, o_ref):
    skip = nxt_ref.shape[0]
    o_ref[...] = jnp.concatenate([cur_ref[skip:, :], nxt_ref[...]], axis=0).astype(o_ref.dtype)


def _rebase_rows(w_t, row0):
    depth, rows, d = w_t.shape
    r = REBASE_ROWS
    base = row0 - SUBLANES
    n = rows - row0
    assert base % r == 0 and n % r == 0
    blk0, per = base // r, r // SUBLANES
    return pl.pallas_call(
        _rebase_kernel,
        out_shape=jax.ShapeDtypeStruct((depth, n, d), BF16),
        grid=(depth, n // r),
        in_specs=[pl.BlockSpec((None, r, d), lambda l, j: (l, blk0 + j, 0)),
                  pl.BlockSpec((None, SUBLANES, d), lambda l, j: (l, (blk0 + j + 1) * per, 0))],
        out_specs=pl.BlockSpec((None, r, d), lambda l, j: (l, j, 0)),
        compiler_params=_params(("parallel", "parallel"), _vmem_limit(
            ((r, d), F32, 3), ((SUBLANES, d), F32, 2), ((r, d), BF16, 3))),
        name="rebase_w_in",
    )(w_t, w_t)


def _first_row_tile():
    return pl.program_id(1) == 0


def _cast_weight(w_ref, wb_ref):
    @pl.when(_first_row_tile())
    def _():
        wb_ref[...] = w_ref[...].astype(BF16)


def _causal_conv(acc, buf_ref, cw_ref, cb_ref, cs, tm):
    kconv = cw_ref.shape[0]
    buf_ref[SUBLANES:2 * SUBLANES, cs] = acc[0:SUBLANES, :]
    out = cb_ref[:, cs] + cw_ref[kconv - 1:kconv, cs] * acc
    for j in range(kconv - 1):
        dist = kconv - 1 - j
        top = buf_ref[SUBLANES - dist:2 * SUBLANES - dist, cs]
        shifted = jnp.concatenate([top, pltpu.roll(acc, dist, 0)[SUBLANES:, :]], axis=0)
        out = out + cw_ref[j:j + 1, cs] * shifted
    buf_ref[0:SUBLANES, cs] = acc[tm - SUBLANES:tm, :]
    return out


def _col_subtiles(n):
    sub = min(MXU_COLS, n)
    return [slice(c * sub, (c + 1) * sub) for c in range(n // sub)]


def _proj_conv_kernel(h_ref, w_ref, cw_ref, cb_ref, o_ref, buf_ref, *, tm):
    @pl.when(_first_row_tile())
    def _():
        buf_ref[0:SUBLANES, :] = jnp.zeros((SUBLANES, buf_ref.shape[1]), F32)

    h = h_ref[...]
    for cs in _col_subtiles(o_ref.shape[1]):
        out = _causal_conv(_dot_nt(h, w_ref[cs, :]), buf_ref, cw_ref, cb_ref, cs, tm)
        o_ref[:, cs] = _silu(out).astype(o_ref.dtype)


def _proj_plain_kernel(h_ref, w_ref, o_ref):
    h = h_ref[...]
    for cs in _col_subtiles(o_ref.shape[1]):
        o_ref[:, cs] = _dot_nt(h, w_ref[cs, :]).astype(o_ref.dtype)


def _proj_mixed_kernel(h_ref, w_ref, o_ref, *, n_plain):
    j = pl.program_id(0)

    @pl.when(j < n_plain)
    def _():
        _proj_plain_kernel(h_ref, w_ref, o_ref)

    @pl.when(j >= n_plain)
    def _():
        h = h_ref[...]
        for cs in _col_subtiles(o_ref.shape[1]):
            o_ref[:, cs] = jax.nn.sigmoid(_dot_nt(h, w_ref[cs, :])).astype(o_ref.dtype)


def _proj_rope_kernel(h_ref, w_ref, ca_ref, cb_ref, cc_ref, o_ref, *, n_qtiles, qscale):
    scale = jnp.where(pl.program_id(0) < n_qtiles, qscale, 1.0).astype(F32)
    ca = ca_ref[...] * scale
    cb = cb_ref[...] * scale
    cc = cc_ref[...] * scale
    h = h_ref[...]
    for cs in _col_subtiles(o_ref.shape[1]):
        acc = _dot_nt(h, w_ref[cs, :])
        for s in range(acc.shape[1] // LANES):
            xs = acc[:, s * LANES:(s + 1) * LANES]
            r = (xs * ca + pltpu.roll(xs, LANES - ROPE_DIM // 2, 1) * cb
                 + pltpu.roll(xs, ROPE_DIM // 2, 1) * cc)
            o_ref[:, cs.start + s * LANES:cs.start + (s + 1) * LANES] = r.astype(o_ref.dtype)


def _proj(kernel, h, w, l, col0, n, out_dtype, *, tn, extra=(), extra_specs=(), scratch=(), extra_vmem=(),
          name):
    s, d = h.shape
    tm = min(1024, s)
    blk0 = col0 // tn
    assert blk0 * tn == col0 and n % tn == 0
    return pl.pallas_call(
        kernel,
        out_shape=jax.ShapeDtypeStruct((s, n), out_dtype),
        grid=(n // tn, s // tm),
        in_specs=[pl.BlockSpec((tm, d), lambda j, i: (i, 0)),
                  pl.BlockSpec((None, tn, d), lambda j, i: (l, blk0 + j, 0))] + list(extra_specs),
        out_specs=pl.BlockSpec((tm, tn), lambda j, i: (i, j)),
        scratch_shapes=list(scratch),
        compiler_params=_params(("parallel", "arbitrary"), _vmem_limit(
            ((tm, d), BF16, 2), ((d, tn), BF16, 2), ((tm, tn), out_dtype, 2),
            ((tm, MXU_COLS), F32, 8), *extra_vmem)),
        name=name,
    )(h, w, *extra)


def _mlstm_kernel(qk_ref, v_ref, gate_ref, bias_ref, so_ref, g_ref, o_ref, st_ref, m_ref):
    L = MLSTM_L

    @pl.when(pl.program_id(0) == 0)
    def _():
        st_ref[...] = jnp.zeros(st_ref.shape, F32)
        m_ref[...] = jnp.zeros(m_ref.shape, F32)

    ipre = gate_ref[...] + bias_ref[...]
    fpre = pltpu.roll(ipre, LANES - M_HEADS, 1)
    logf = jnp.minimum(fpre, 0.0) - jnp.log1p(jnp.exp(-jnp.abs(fpre)))

    row = lax.broadcasted_iota(jnp.int32, (L, L), 0)
    col = lax.broadcasted_iota(jnp.int32, (L, L), 1)
    tri = col <= row
    tri_b = jnp.where(tri, 1.0, 0.0).astype(BF16)
    hi = logf.astype(BF16)
    r1 = logf - hi.astype(F32)
    mid = r1.astype(BF16)
    lo = (r1 - mid.astype(F32)).astype(BF16)
    b = _dot(tri_b, hi) + _dot(tri_b, mid) + _dot(tri_b, lo)
    a = ipre - b
    a_t = a.T
    m_all = m_ref[...]
    lane = lax.broadcasted_iota(jnp.int32, (1, LANES), 1)
    ones_col = jnp.where(lax.broadcasted_iota(jnp.int32, (L, LANES), 1) == 0, 1.0, 0.0).astype(BF16)
    m_next = m_all

    for h in range(M_HEADS):
        a_row = a_t[h:h + 1, :]
        a_col = a[:, h:h + 1]
        b_col = b[:, h:h + 1]
        m_prev = m_all[:, h:h + 1]
        amask = jnp.where(tri, a_row, NEG)
        m_run = jnp.maximum(jnp.max(amask, axis=1, keepdims=True), m_prev)
        dmat = jnp.exp(amask - m_run)
        w_inter = jnp.exp(m_prev - m_run)
        q = qk_ref[:, h * M_QK_DIM:(h + 1) * M_QK_DIM] * jnp.asarray(M_QK_DIM ** -0.5, BF16)
        k = qk_ref[:, M_QK + h * M_QK_DIM:M_QK + (h + 1) * M_QK_DIM]
        v_ext = jnp.concatenate([v_ref[:, h * M_V_DIM:(h + 1) * M_V_DIM], ones_col], axis=1)
        sd = (_dot_nt(q, k) * dmat).astype(BF16)
        st = st_ref[h]
        tot = w_inter * _dot(q, st.astype(BF16)) + _dot(sd, v_ext)
        num = tot[:, :M_V_DIM]
        den = tot[:, M_V_DIM:M_V_DIM + 1]
        hval = num / jnp.maximum(jnp.abs(den), jnp.exp(-(b_col + m_run)))
        m_last = m_run[L - 1:L, :]
        w_s = jnp.exp(a_col - m_last)
        decay = jnp.exp(m_prev - m_last)
        rhs = (w_s * v_ext.astype(F32)).astype(BF16)
        st_ref[h] = decay * st + _dot_tn(k, rhs)
        m_next = jnp.where(lane == h, b_col[L - 1:L, :] + m_last, m_next)
        ms = jnp.mean(hval * hval, axis=-1, keepdims=True)
        sl = slice(h * M_V_DIM, (h + 1) * M_V_DIM)
        hn = hval * lax.rsqrt(ms + EPS) * g_ref[:, sl]
        o_ref[:, sl] = (so_ref[:, sl].astype(F32) * hn).astype(o_ref.dtype)

    m_ref[...] = m_next


def _mlstm(qk, v_so, gates, gate_bias, g_mlstm, l):
    s = qk.shape[0]
    L = MLSTM_L
    st_shape = (M_HEADS, M_QK_DIM, M_V_DIM + LANES)
    return pl.pallas_call(
        _mlstm_kernel,
        out_shape=jax.ShapeDtypeStruct((s, M_V), BF16),
        grid=(s // L,),
        in_specs=[pl.BlockSpec((L, 2 * M_QK), lambda c: (c, 0)),
                  pl.BlockSpec((L, M_V), lambda c: (c, 0)),
                  pl.BlockSpec((L, LANES), lambda c: (c, 0)),
                  pl.BlockSpec((None, 1, LANES), lambda c: (l, 0, 0)),
                  pl.BlockSpec((L, M_V), lambda c: (c, 1)),
                  pl.BlockSpec((None, 1, M_V), lambda c: (l, 0, 0))],
        out_specs=pl.BlockSpec((L, M_V), lambda c: (c, 0)),
        scratch_shapes=[pltpu.VMEM(st_shape, F32), pltpu.VMEM((1, LANES), F32)],
        compiler_params=_params(("arbitrary",), _vmem_limit(
            ((L, 2 * M_QK), BF16, 2), ((L, M_V), BF16, 6), ((L, LANES), F32, 2),
            (st_shape, F32, 2), ((L, M_V + LANES), F32, 8), ((L, L), F32, 8))),
        name="mlstm",
    )(qk, v_so, gates, gate_bias, v_so, g_mlstm)


def _attn_kernel(q_ref, k_ref, v_ref, lam_ref, g_ref, o_ref, vt_ref, acc_ref, m_ref, sa_ref, sb_ref,
                 mxa_ref, mxb_ref, *, lam_init, seq):
    t = ATT_T
    qi = pl.program_id(1)

    @pl.when(qi == 0)
    def _():
        ones_row = jnp.where(lax.broadcasted_iota(jnp.int32, (ATT_VROWS - B_V_DIM, t), 0) == 0, 1.0, 0.0)
        for kb in range(seq // t):
            v_t = v_ref[kb * t:(kb + 1) * t, :].astype(F32).T
            vt_ref[kb] = jnp.concatenate([v_t, ones_row], axis=0).astype(BF16)

    def query_block(qb):
        q = q_ref[pl.ds(pl.multiple_of(qb * t, t), t), :]
        qlane = lax.broadcasted_iota(jnp.int32, q.shape, 1)
        zero = jnp.zeros_like(q)
        return jnp.concatenate([jnp.where(qlane < B_HEAD_DIM, q, zero),
                                jnp.where(qlane >= B_HEAD_DIM, q, zero)], axis=0)

    qcat = query_block(qi)
    shift = CHUNK.bit_length() - 1
    kchunk = lax.broadcasted_iota(jnp.int32, (t, 2 * t), 0) >> shift
    qchunk = (lax.broadcasted_iota(jnp.int32, (t, 2 * t), 1) & (t - 1)) >> shift
    diag_mask = kchunk <= qchunk

    def scores(kb, s_ref, mx_ref, mask=None, qc=qcat):
        kblk = k_ref[pl.ds(pl.multiple_of(kb * t, t), t), :]
        s_t = _dot_nt(kblk, qc)
        if mask is not None:
            s_t = jnp.where(mask, s_t, NEG)
        s_ref[:, :2 * t] = s_t
        mx_ref[...] = jnp.max(s_t, axis=0, keepdims=True)

    def absorb(kb, s_t, mx):
        m_old = m_ref[...]
        m_new = jnp.maximum(m_old, mx)
        p = jnp.exp2(s_t - m_new).astype(BF16)
        acc_ref[...] = jnp.exp2(m_old - m_new) * acc_ref[...] + _dot(vt_ref[kb], p)
        m_ref[...] = m_new

    acc_ref[...] = jnp.zeros(acc_ref.shape, F32)
    m_ref[...] = jnp.full(m_ref.shape, NEG, F32)

    @pl.when(qi == 0)
    def _():
        scores(0, sa_ref, mxa_ref)

    def pair(kb):
        scores(kb + 1, sb_ref, mxb_ref)
        absorb(kb, sa_ref[:, :2 * t], mxa_ref[...])
        scores(kb + 2, sa_ref, mxa_ref)
        absorb(kb + 1, sb_ref[:, :2 * t], mxb_ref[...])

    def body(it, carry):
        for u in range(0, 8, 2):
            pair(8 * it + u)
        return carry

    lax.fori_loop(0, qi >> 3, body, 0)

    def finish():
        lf = lam_ref[...]
        lam = (jnp.exp(jnp.sum(lf[0:1] * lf[1:2], axis=1, keepdims=True))
               - jnp.exp(jnp.sum(lf[2:3] * lf[3:4], axis=1, keepdims=True)) + lam_init)
        acc = acc_ref[...]
        o1 = acc[:B_V_DIM, :t] / acc[B_V_DIM:B_V_DIM + 1, :t]
        o2 = acc[:B_V_DIM, t:] / acc[B_V_DIM:B_V_DIM + 1, t:]
        o_t = o1 - lam * o2
        ms = jnp.mean(o_t * o_t, axis=0, keepdims=True)
        o = (o_t * lax.rsqrt(ms + EPS)).T
        o_ref[...] = (o * (g_ref[...] * (1.0 - lam_init))).astype(o_ref.dtype)
        q_next = jnp.minimum(qi + 1, pl.num_programs(1) - 1)
        scores(0, sa_ref, mxa_ref, qc=query_block(q_next))

    def tail(rem):
        for u in range(0, rem - 1, 2):
            pair(qi - rem + u)
        if rem & 1:
            scores(qi, sb_ref, mxb_ref, diag_mask)
            absorb(qi - 1, sa_ref[:, :2 * t], mxa_ref[...])
            absorb(qi, sb_ref[:, :2 * t], mxb_ref[...])
        else:
            s_d = jnp.where(diag_mask, sa_ref[:, :2 * t], NEG)
            absorb(qi, s_d, jnp.max(s_d, axis=0, keepdims=True))
        finish()

    for rem in range(8):
        pl.when((qi & 7) == rem)(functools.partial(tail, rem))


def _attention(qkb, v_sg, lambdas, g_diff, l, lam_init):
    s = qkb.shape[0]
    t = ATT_T
    kern = functools.partial(_attn_kernel, lam_init=lam_init, seq=s)
    return pl.pallas_call(
        kern,
        out_shape=jax.ShapeDtypeStruct((s, B_V), BF16),
        grid=(B_HEADS, s // t),
        in_specs=[pl.BlockSpec((s, LANES), lambda h, i: (0, h)),
                  pl.BlockSpec((s, LANES), lambda h, i: (0, B_HEADS + h)),
                  pl.BlockSpec((s, LANES), lambda h, i: (0, h)),
                  pl.BlockSpec((None, 4, B_HEAD_DIM), lambda h, i: (l, 0, 0)),
                  pl.BlockSpec((None, 1, LANES), lambda h, i: (l, 0, h))],
        out_specs=pl.BlockSpec((t, LANES), lambda h, i: (i, h)),
        scratch_shapes=[pltpu.VMEM((s // t, ATT_VROWS, t), BF16),
                        pltpu.VMEM((ATT_VROWS, 2 * t), F32),
                        pltpu.VMEM((1, 2 * t), F32),
                        pltpu.VMEM((t, 2 * t + ATT_PAD), F32),
                        pltpu.VMEM((t, 2 * t + ATT_PAD), F32),
                        pltpu.VMEM((1, 2 * t), F32),
                        pltpu.VMEM((1, 2 * t), F32)],
        compiler_params=_params(("arbitrary", "arbitrary"), _vmem_limit(
            ((s, LANES), BF16, 6), ((s // t, ATT_VROWS, t), BF16, 1), ((t, LANES), BF16, 6),
            ((t, 2 * t), F32, 8), ((ATT_VROWS, 2 * t), F32, 4))),
        name="diff_attn",
    )(qkb, qkb, v_sg, lambdas, g_diff)


def _mix_out_kernel(a_ref, b_ref, ga_ref, gb_ref, wa_ref, wb_ref, wo_ref, x_ref, g_ref, gt_ref, gn_ref, sc_ref,
                    sh_ref, o_ref, hn_ref, m_ref):
    a = a_ref[...]
    b = b_ref[...]
    for cs in _col_subtiles(m_ref.shape[1]):
        ya = _dot(a, wa_ref[:, cs])
        yb = _dot(b, wb_ref[:, cs])
        m_ref[:, cs] = (ga_ref[:, cs].astype(F32) * ya + gb_ref[:, cs].astype(F32) * yb).astype(m_ref.dtype)
    y = _dot(m_ref[...], wo_ref[...])
    _residual_epilogue(y, x_ref, g_ref, gt_ref, (gn_ref, sc_ref, sh_ref), o_ref, hn_ref)


def _mix_out(ya_in, yb_in, v_sg, w_a, w_b, w_out, x, g_post, mod, g_next, l):
    s, d = x.shape
    tm = min(MXU_COLS, s)
    row = lambda c: pl.BlockSpec((tm, d), lambda i: (i, c))
    weight = pl.BlockSpec((None, d, d), lambda i: (l, 0, 0), pipeline_mode=pl.Buffered(1))
    vec = lambda blk: pl.BlockSpec((None, 1, d), lambda i: (l, 0, blk))
    return pl.pallas_call(
        _mix_out_kernel,
        out_shape=[jax.ShapeDtypeStruct((s, d), F32), jax.ShapeDtypeStruct((s, d), BF16)],
        grid=(s // tm,),
        in_specs=[row(0), row(0), row(1), row(2), weight, weight, weight, row(0), vec(0), vec(2), vec(0), vec(4),
                  vec(3)],
        out_specs=[row(0), row(0)],
        scratch_shapes=[pltpu.VMEM((tm, d), BF16)],
        compiler_params=_params(("parallel",), _vmem_limit(
            ((d, d), BF16, 3), ((tm, d), BF16, 11), ((tm, d), F32, 7))),
        name="mix_out",
    )(ya_in, yb_in, v_sg, v_sg, w_a, w_b, w_out, x, g_post, mod, g_next, mod, mod)


def _residual_epilogue(y, x_ref, g_ref, gt_ref, nxt_refs, o_ref, hn_ref):
    ms = jnp.mean(y * y, axis=-1, keepdims=True)
    x_new = x_ref[...] + gt_ref[...] * (y * lax.rsqrt(ms + EPS) * g_ref[...])
    o_ref[...] = x_new
    if nxt_refs is not None:
        gn_ref, sc_ref, sh_ref = nxt_refs
        hn_ref[...] = _modulated_norm(x_new, gn_ref[...], sc_ref[...], sh_ref[...]).astype(hn_ref.dtype)


def _proj_res_kernel(a_ref, w_ref, x_ref, g_ref, gt_ref, *rest, with_next):
    if with_next:
        *nxt_refs, o_ref, hn_ref = rest
    else:
        (o_ref,), nxt_refs, hn_ref = rest, None, None
    _residual_epilogue(_dot(a_ref[...], w_ref[...]), x_ref, g_ref, gt_ref, nxt_refs, o_ref, hn_ref)


def _proj_res(a, w, x, gains, mod, l, gt_blk, tm, name, nxt=None):
    s, k = a.shape
    d = w.shape[2]
    tm = min(tm, s)
    rows = lambda width: pl.BlockSpec((tm, width), lambda i: (i, 0))
    vec = lambda arr, lay, blk: (arr, pl.BlockSpec((None, 1, d), lambda i: (lay, 0, blk)))
    operands = [(a, rows(k)),
                (w, pl.BlockSpec((None, k, d), lambda i: (l, 0, 0), pipeline_mode=pl.Buffered(1))),
                (x, rows(d)), vec(gains, l, 0), vec(mod, l, gt_blk)]
    out_shape = [jax.ShapeDtypeStruct((s, d), F32)]
    out_specs = [rows(d)]
    if nxt is not None:
        n_gains, n_l, n_sc, n_sh = nxt
        operands += [vec(n_gains, n_l, 0), vec(mod, n_l, n_sc), vec(mod, n_l, n_sh)]
        out_shape.append(jax.ShapeDtypeStruct((s, d), BF16))
        out_specs.append(rows(d))
    outs = pl.pallas_call(
        functools.partial(_proj_res_kernel, with_next=nxt is not None),
        out_shape=out_shape,
        grid=(s // tm,),
        in_specs=[spec for _, spec in operands],
        out_specs=out_specs,
        compiler_params=_params(("parallel",), _vmem_limit(
            ((tm, k), BF16, 2), ((k, d), BF16, 1), ((tm, d), F32, 10), ((tm, d), BF16, 2))),
        name=name,
    )(*[arr for arr, _ in operands])
    return outs if nxt is not None else (outs[0], None)


def _ffn_up_kernel(h_ref, wg_ref, wu_ref, cw_ref, cb_ref, o_ref, wgb_ref, wub_ref, buf_ref, *, tm):
    _cast_weight(wg_ref, wgb_ref)
    _cast_weight(wu_ref, wub_ref)

    @pl.when(_first_row_tile())
    def _():
        buf_ref[0:SUBLANES, :] = jnp.zeros((SUBLANES, buf_ref.shape[1]), F32)

    h = h_ref[...]
    for cs in _col_subtiles(o_ref.shape[1]):
        g = _causal_conv(_dot(h, wgb_ref[:, cs]), buf_ref, cw_ref, cb_ref, cs, tm)
        o_ref[:, cs] = (_silu(g) * _dot(h, wub_ref[:, cs])).astype(o_ref.dtype)


def _ffn_up(h, w_gate, w_up, conv_w, conv_b, l):
    s, d = h.shape
    n = w_gate.shape[2]
    tm, tn = min(1024, s), 512
    kconv = conv_w.shape[1]
    kern = functools.partial(_ffn_up_kernel, tm=tm)
    return pl.pallas_call(
        kern,
        out_shape=jax.ShapeDtypeStruct((s, n), BF16),
        grid=(n // tn, s // tm),
        in_specs=[pl.BlockSpec((tm, d), lambda j, i: (i, 0)),
                  pl.BlockSpec((None, d, tn), lambda j, i: (l, 0, j)),
                  pl.BlockSpec((None, d, tn), lambda j, i: (l, 0, j)),
                  pl.BlockSpec((None, kconv, tn), lambda j, i: (l, 0, j)),
                  pl.BlockSpec((None, 1, tn), lambda j, i: (l, 0, j))],
        out_specs=pl.BlockSpec((tm, tn), lambda j, i: (i, j)),
        scratch_shapes=[pltpu.VMEM((d, tn), BF16), pltpu.VMEM((d, tn), BF16),
                        pltpu.VMEM((2 * SUBLANES, tn), F32)],
        compiler_params=_params(("parallel", "arbitrary"), _vmem_limit(
            ((tm, d), BF16, 2), ((d, tn), F32, 4), ((d, tn), BF16, 2), ((tm, tn), BF16, 2),
            ((tm, tn), F32, 6))),
        name="ffn_up",
    )(h, w_gate, w_up, conv_w, conv_b)


def _rope_tables(positions):
    half = ROPE_DIM // 2
    inv = jnp.power(ROPE_THETA, -(jnp.arange(half, dtype=F32) * 2.0 / ROPE_DIM))
    ang = positions.astype(F32)[:, None] * inv
    cos, sin = jnp.cos(ang), jnp.sin(ang)
    s = positions.shape[0]
    rest = B_HEAD_DIM - ROPE_DIM
    reps = LANES // B_HEAD_DIM
    ca = jnp.tile(jnp.concatenate([cos, cos, jnp.ones((s, rest), F32)], axis=1), (1, reps))
    cb = jnp.tile(jnp.concatenate([-sin, jnp.zeros((s, B_HEAD_DIM - half), F32)], axis=1), (1, reps))
    cc = jnp.tile(jnp.concatenate([jnp.zeros((s, half), F32), sin, jnp.zeros((s, rest), F32)], axis=1),
                  (1, reps))
    return ca, cb, cc


def kernel(x, c, positions, w_ada, b_ada, g_pre_mix, g_post_mix, g_pre_ffn, g_post_ffn, w_in, conv_qk_w, conv_qk_b, b_igate, b_fgate, g_mlstm, lambdas, g_diff, w_a, w_b, w_out, w_gate, w_up, conv_ffn_w, conv_ffn_b, w_down):
    batch, s, d = x.shape
    depth = w_in.shape[0]
    assert batch == 1 and s % 1024 == 0 and d == M_V == B_V
    xs = x.reshape(s, d)
    tm_proj = min(1024, s)

    mod = _adaln(c.reshape(d, 1), w_ada, b_ada)
    ca, cb, cc = _rope_tables(positions[0])
    rope_specs = [pl.BlockSpec((tm_proj, LANES), lambda j, i: (i, 0))] * 3

    def gains(g):
        return g.reshape(depth, 1, g.shape[1])

    g_pre_mix, g_post_mix, g_pre_ffn, g_post_ffn = map(gains, (g_pre_mix, g_post_mix, g_pre_ffn, g_post_ffn))
    g_mlstm3, g_diff3 = gains(g_mlstm), gains(g_diff)
    conv_qk_b3, conv_ffn_b3 = gains(conv_qk_b), gains(conv_ffn_b)
    o_vm = 2 * M_QK
    o_om = o_vm + M_V
    o_gates = o_om + M_V
    o_tail = o_gates + 2 * M_HEADS
    gate_bias = jnp.concatenate([b_igate, b_fgate, jnp.zeros((depth, LANES - 2 * M_HEADS), F32)],
                                axis=1).reshape(depth, 1, LANES)
    assert math.log2(M_QK_DIM) % 2 == 0
    w_in_t = jnp.swapaxes(w_in, 1, 2)
    w_head = w_in_t[:, :o_gates + LANES, :].astype(BF16)
    w_tail = _rebase_rows(w_in_t, o_tail)
    w_a16, w_b16, w_out16, w_down16 = (w.astype(BF16) for w in (w_a, w_b, w_out, w_down))
    tn, tw = 1024, 2048

    h = _prenorm(xs, g_pre_mix, mod, 0, 1, 0)
    for l in range(depth):
        lam_init = 0.8 - 0.6 * math.exp(-0.3 * l)

        qk = _proj(functools.partial(_proj_conv_kernel, tm=tm_proj), h, w_head, l, 0, 2 * M_QK, BF16, tn=tn,
                   extra=(conv_qk_w, conv_qk_b3),
                   extra_specs=[pl.BlockSpec((None, conv_qk_w.shape[1], tn), lambda j, i: (l, 0, j)),
                                pl.BlockSpec((None, 1, tn), lambda j, i: (l, 0, j))],
                   scratch=[pltpu.VMEM((2 * SUBLANES, tn), F32)],
                   extra_vmem=(((tm_proj, tn), F32, 3),), name="proj_conv")
        v_so = _proj(functools.partial(_proj_mixed_kernel, n_plain=M_V // tw), h, w_head, l, o_vm, 2 * M_V,
                     BF16, tn=tw, name="proj_v_so")
        gates = _proj(_proj_plain_kernel, h, w_head, l, o_gates, LANES, F32, tn=LANES, name="proj_gates")
        qkb = _proj(functools.partial(_proj_rope_kernel, n_qtiles=B_QK // tw, qscale=B_HEAD_DIM ** -0.5 * LOG2E),
                    h, w_tail, l, 0, 2 * B_QK, BF16, tn=tw, extra=(ca, cb, cc), extra_specs=rope_specs,
                    extra_vmem=(((tm_proj, LANES), F32, 9),), name="proj_rope")
        v_sg = _proj(functools.partial(_proj_mixed_kernel, n_plain=B_V // tw), h, w_tail, l, 2 * B_QK,
                     B_V + 2 * d, BF16, tn=tw, name="proj_v_sg")

        y_a_in = _mlstm(qk, v_so, gates, gate_bias, g_mlstm3, l)
        y_b_in = _attention(qkb, v_sg, lambdas, g_diff3, l, lam_init)
        xs, h = _mix_out(y_a_in, y_b_in, v_sg, w_a16, w_b16, w_out16, xs, g_post_mix, mod, g_pre_ffn, l)

        act = _ffn_up(h, w_gate, w_up, conv_ffn_w, conv_ffn_b3, l)
        nxt = (g_pre_mix, l + 1, 1, 0) if l + 1 < depth else None
        xs, h = _proj_res(act, w_down16, xs, g_post_ffn, mod, l, 5, 256, "ffn_down", nxt=nxt)

    return xs.reshape(batch, s, d)
```

```python
import functools
import math

import jax
import jax.numpy as jnp
from jax import lax
from jax.experimental import pallas as pl
from jax.experimental.pallas import tpu as pltpu

F32 = jnp.float32
BF16 = jnp.bfloat16

M_HEADS = 4
M_QK_DIM = 256
M_V_DIM = 512
M_QK = M_HEADS * M_QK_DIM
M_V = M_HEADS * M_V_DIM
B_HEADS = 16
B_HEAD_DIM = 64
B_V_DIM = 2 * B_HEAD_DIM
B_QK = B_HEADS * 2 * B_HEAD_DIM
B_V = B_HEADS * B_V_DIM
CHUNK = 64
ROPE_THETA = 500000.0
ROPE_DIM = B_HEAD_DIM // 4
EPS = 1e-6

LANES = 128
SUBLANES = 8
MXU_COLS = 256
V7X_VMEM_REQUEST_CAP = 60000 * 1024
V7X_COMPILER_SCRATCH = 8 * 1024 * 1024

MLSTM_L = 256
ATT_T = 512
ATT_VROWS = B_V_DIM + 16
ATT_PAD = LANES
LOG2E = math.log2(math.e)
NEG = -1e30


def _nbytes(shape, dtype):
    return math.prod(shape) * jnp.dtype(dtype).itemsize


def _vmem_limit(*terms):
    total = sum(_nbytes(s, d) * n for s, d, n in terms) + V7X_COMPILER_SCRATCH
    return int(min(V7X_VMEM_REQUEST_CAP, total))


def _params(sem, limit):
    return pltpu.CompilerParams(dimension_semantics=sem, vmem_limit_bytes=limit)


def _silu(x):
    return x * jax.nn.sigmoid(x)


def _dot(a, b):
    return jnp.dot(a, b, preferred_element_type=F32)


def _dot_nt(a, b):
    return lax.dot_general(a, b, (((1,), (1,)), ((), ())), preferred_element_type=F32)


def _dot_tn(a, b):
    return lax.dot_general(a, b, (((0,), (0,)), ((), ())), preferred_element_type=F32)


def _adaln_kernel(c_ref, w_ref, b_ref, o_ref):
    c = c_ref[...]
    o_ref[...] = jnp.sum(_silu(c) * w_ref[...], axis=0, keepdims=True) + b_ref[...]


def _adaln(c_col, w_ada, b_ada):
    depth, d, n = w_ada.shape
    tn = 1024
    return pl.pallas_call(
        _adaln_kernel,
        out_shape=jax.ShapeDtypeStruct((depth, 1, n), F32),
        grid=(depth, n // tn),
        in_specs=[pl.BlockSpec((d, 1), lambda l, j: (0, 0)),
                  pl.BlockSpec((None, d, tn), lambda l, j: (l, 0, j)),
                  pl.BlockSpec((None, 1, tn), lambda l, j: (l, 0, j))],
        out_specs=pl.BlockSpec((None, 1, tn), lambda l, j: (l, 0, j)),
        compiler_params=_params(("parallel", "parallel"),
                                _vmem_limit(((d, LANES), F32, 2), ((d, tn), F32, 3))),
        name="adaln",
    )(c_col, w_ada, b_ada.reshape(depth, 1, n))


def _modulated_norm(x, g, sc, sh):
    ms = jnp.mean(x * x, axis=-1, keepdims=True)
    return (x * (g * (1.0 + sc))) * lax.rsqrt(ms + EPS) + sh


def _prenorm_kernel(x_ref, g_ref, sc_ref, sh_ref, o_ref):
    o_ref[...] = _modulated_norm(x_ref[...], g_ref[...], sc_ref[...], sh_ref[...]).astype(o_ref.dtype)


def _prenorm(x, gains, mod, l, sc_blk, sh_blk):
    s, d = x.shape
    tm = min(512, s)
    return pl.pallas_call(
        _prenorm_kernel,
        out_shape=jax.ShapeDtypeStruct((s, d), BF16),
        grid=(s // tm,),
        in_specs=[pl.BlockSpec((tm, d), lambda i: (i, 0)),
                  pl.BlockSpec((None, 1, d), lambda i: (l, 0, 0)),
                  pl.BlockSpec((None, 1, d), lambda i: (l, 0, sc_blk)),
                  pl.BlockSpec((None, 1, d), lambda i: (l, 0, sh_blk))],
        out_specs=pl.BlockSpec((tm, d), lambda i: (i, 0)),
        compiler_params=_params(("parallel",), _vmem_limit(((tm, d), F32, 4), ((tm, d), BF16, 2))),
        name="prenorm",
    )(x, gains, mod, mod)


REBASE_ROWS = 512


def _rebase_kernel(cur_ref, nxt_ref, o_ref):
    skip = nxt_ref.shape[0]
    o_ref[...] = jnp.concatenate([cur_ref[skip:, :], nxt_ref[...]], axis=0).astype(o_ref.dtype)


def _rebase_rows(w_t, row0):
    depth, rows, d = w_t.shape
    r = REBASE_ROWS
    base = row0 - SUBLANES
    n = rows - row0
    assert base % r == 0 and n % r == 0
    blk0, per = base // r, r // SUBLANES
    return pl.pallas_call(
        _rebase_kernel,
        out_shape=jax.ShapeDtypeStruct((depth, n, d), BF16),
        grid=(depth, n // r),
        in_specs=[pl.BlockSpec((None, r, d), lambda l, j: (l, blk0 + j, 0)),
                  pl.BlockSpec((None, SUBLANES, d), lambda l, j: (l, (blk0 + j + 1) * per, 0))],
        out_specs=pl.BlockSpec((None, r, d), lambda l, j: (l, j, 0)),
        compiler_params=_params(("parallel", "parallel"), _vmem_limit(
            ((r, d), F32, 3), ((SUBLANES, d), F32, 2), ((r, d), BF16, 3))),
        name="rebase_w_in",
    )(w_t, w_t)


def _first_row_tile():
    return pl.program_id(1) == 0


def _cast_weight(w_ref, wb_ref):
    @pl.when(_first_row_tile())
    def _():
        wb_ref[...] = w_ref[...].astype(BF16)


def _causal_conv(acc, buf_ref, cw_ref, cb_ref, cs, tm):
    kconv = cw_ref.shape[0]
    buf_ref[SUBLANES:2 * SUBLANES, cs] = acc[0:SUBLANES, :]
    out = cb_ref[:, cs] + cw_ref[kconv - 1:kconv, cs] * acc
    for j in range(kconv - 1):
        dist = kconv - 1 - j
        top = buf_ref[SUBLANES - dist:2 * SUBLANES - dist, cs]
        shifted = jnp.concatenate([top, pltpu.roll(acc, dist, 0)[SUBLANES:, :]], axis=0)
        out = out + cw_ref[j:j + 1, cs] * shifted
    buf_ref[0:SUBLANES, cs] = acc[tm - SUBLANES:tm, :]
    return out


def _col_subtiles(n):
    sub = min(MXU_COLS, n)
    return [slice(c * sub, (c + 1) * sub) for c in range(n // sub)]


def _proj_conv_kernel(h_ref, w_ref, cw_ref, cb_ref, o_ref, buf_ref, *, tm):
    @pl.when(_first_row_tile())
    def _():
        buf_ref[0:SUBLANES, :] = jnp.zeros((SUBLANES, buf_ref.shape[1]), F32)

    h = h_ref[...]
    for cs in _col_subtiles(o_ref.shape[1]):
        out = _causal_conv(_dot_nt(h, w_ref[cs, :]), buf_ref, cw_ref, cb_ref, cs, tm)
        o_ref[:, cs] = _silu(out).astype(o_ref.dtype)


def _proj_plain_kernel(h_ref, w_ref, o_ref):
    h = h_ref[...]
    for cs in _col_subtiles(o_ref.shape[1]):
        o_ref[:, cs] = _dot_nt(h, w_ref[cs, :]).astype(o_ref.dtype)


def _proj_mixed_kernel(h_ref, w_ref, o_ref, *, n_plain):
    j = pl.program_id(0)

    @pl.when(j < n_plain)
    def _():
        _proj_plain_kernel(h_ref, w_ref, o_ref)

    @pl.when(j >= n_plain)
    def _():
        h = h_ref[...]
        for cs in _col_subtiles(o_ref.shape[1]):
            o_ref[:, cs] = jax.nn.sigmoid(_dot_nt(h, w_ref[cs, :])).astype(o_ref.dtype)


def _proj_rope_kernel(h_ref, w_ref, ca_ref, cb_ref, cc_ref, o_ref, *, n_qtiles, qscale):
    scale = jnp.where(pl.program_id(0) < n_qtiles, qscale, 1.0).astype(F32)
    ca = ca_ref[...] * scale
    cb = cb_ref[...] * scale
    cc = cc_ref[...] * scale
    h = h_ref[...]
    for cs in _col_subtiles(o_ref.shape[1]):
        acc = _dot_nt(h, w_ref[cs, :])
        for s in range(acc.shape[1] // LANES):
            xs = acc[:, s * LANES:(s + 1) * LANES]
            r = (xs * ca + pltpu.roll(xs, LANES - ROPE_DIM // 2, 1) * cb
                 + pltpu.roll(xs, ROPE_DIM // 2, 1) * cc)
            o_ref[:, cs.start + s * LANES:cs.start + (s + 1) * LANES] = r.astype(o_ref.dtype)


def _proj(kernel, h, w, l, col0, n, out_dtype, *, tn, extra=(), extra_specs=(), scratch=(), extra_vmem=(),
          name):
    s, d = h.shape
    tm = min(1024, s)
    blk0 = col0 // tn
    assert blk0 * tn == col0 and n % tn == 0
    return pl.pallas_call(
        kernel,
        out_shape=jax.ShapeDtypeStruct((s, n), out_dtype),
        grid=(n // tn, s // tm),
        in_specs=[pl.BlockSpec((tm, d), lambda j, i: (i, 0)),
                  pl.BlockSpec((None, tn, d), lambda j, i: (l, blk0 + j, 0))] + list(extra_specs),
        out_specs=pl.BlockSpec((tm, tn), lambda j, i: (i, j)),
        scratch_shapes=list(scratch),
        compiler_params=_params(("parallel", "arbitrary"), _vmem_limit(
            ((tm, d), BF16, 2), ((d, tn), BF16, 2), ((tm, tn), out_dtype, 2),
            ((tm, MXU_COLS), F32, 8), *extra_vmem)),
        name=name,
    )(h, w, *extra)


def _mlstm_kernel(qk_ref, v_ref, gate_ref, bias_ref, so_ref, g_ref, o_ref, st_ref, m_ref):
    L = MLSTM_L

    @pl.when(pl.program_id(0) == 0)
    def _():
        st_ref[...] = jnp.zeros(st_ref.shape, F32)
        m_ref[...] = jnp.zeros(m_ref.shape, F32)

    ipre = gate_ref[...] + bias_ref[...]
    fpre = pltpu.roll(ipre, LANES - M_HEADS, 1)
    logf = jnp.minimum(fpre, 0.0) - jnp.log1p(jnp.exp(-jnp.abs(fpre)))

    row = lax.broadcasted_iota(jnp.int32, (L, L), 0)
    col = lax.broadcasted_iota(jnp.int32, (L, L), 1)
    tri = col <= row
    tri_b = jnp.where(tri, 1.0, 0.0).astype(BF16)
    hi = logf.astype(BF16)
    r1 = logf - hi.astype(F32)
    mid = r1.astype(BF16)
    lo = (r1 - mid.astype(F32)).astype(BF16)
    b = _dot(tri_b, hi) + _dot(tri_b, mid) + _dot(tri_b, lo)
    a = ipre - b
    a_t = a.T
    m_all = m_ref[...]
    lane = lax.broadcasted_iota(jnp.int32, (1, LANES), 1)
    ones_col = jnp.where(lax.broadcasted_iota(jnp.int32, (L, LANES), 1) == 0, 1.0, 0.0).astype(BF16)
    m_next = m_all

    for h in range(M_HEADS):
        a_row = a_t[h:h + 1, :]
        a_col = a[:, h:h + 1]
        b_col = b[:, h:h + 1]
        m_prev = m_all[:, h:h + 1]
        amask = jnp.where(tri, a_row, NEG)
        m_run = jnp.maximum(jnp.max(amask, axis=1, keepdims=True), m_prev)
        dmat = jnp.exp(amask - m_run)
        w_inter = jnp.exp(m_prev - m_run)
        q = qk_ref[:, h * M_QK_DIM:(h + 1) * M_QK_DIM] * jnp.asarray(M_QK_DIM ** -0.5, BF16)
        k = qk_ref[:, M_QK + h * M_QK_DIM:M_QK + (h + 1) * M_QK_DIM]
        v_ext = jnp.concatenate([v_ref[:, h * M_V_DIM:(h + 1) * M_V_DIM], ones_col], axis=1)
        sd = (_dot_nt(q, k) * dmat).astype(BF16)
        st = st_ref[h]
        tot = w_inter * _dot(q, st.astype(BF16)) + _dot(sd, v_ext)
        num = tot[:, :M_V_DIM]
        den = tot[:, M_V_DIM:M_V_DIM + 1]
        hval = num / jnp.maximum(jnp.abs(den), jnp.exp(-(b_col + m_run)))
        m_last = m_run[L - 1:L, :]
        w_s = jnp.exp(a_col - m_last)
        decay = jnp.exp(m_prev - m_last)
        rhs = (w_s * v_ext.astype(F32)).astype(BF16)
        st_ref[h] = decay * st + _dot_tn(k, rhs)
        m_next = jnp.where(lane == h, b_col[L - 1:L, :] + m_last, m_next)
        ms = jnp.mean(hval * hval, axis=-1, keepdims=True)
        sl = slice(h * M_V_DIM, (h + 1) * M_V_DIM)
        hn = hval * lax.rsqrt(ms + EPS) * g_ref[:, sl]
        o_ref[:, sl] = (so_ref[:, sl].astype(F32) * hn).astype(o_ref.dtype)

    m_ref[...] = m_next


def _mlstm(qk, v_so, gates, gate_bias, g_mlstm, l):
    s = qk.shape[0]
    L = MLSTM_L
    st_shape = (M_HEADS, M_QK_DIM, M_V_DIM + LANES)
    return pl.pallas_call(
        _mlstm_kernel,
        out_shape=jax.ShapeDtypeStruct((s, M_V), BF16),
        grid=(s // L,),
        in_specs=[pl.BlockSpec((L, 2 * M_QK), lambda c: (c, 0)),
                  pl.BlockSpec((L, M_V), lambda c: (c, 0)),
                  pl.BlockSpec((L, LANES), lambda c: (c, 0)),
                  pl.BlockSpec((None, 1, LANES), lambda c: (l, 0, 0)),
                  pl.BlockSpec((L, M_V), lambda c: (c, 1)),
                  pl.BlockSpec((None, 1, M_V), lambda c: (l, 0, 0))],
        out_specs=pl.BlockSpec((L, M_V), lambda c: (c, 0)),
        scratch_shapes=[pltpu.VMEM(st_shape, F32), pltpu.VMEM((1, LANES), F32)],
        compiler_params=_params(("arbitrary",), _vmem_limit(
            ((L, 2 * M_QK), BF16, 2), ((L, M_V), BF16, 6), ((L, LANES), F32, 2),
            (st_shape, F32, 2), ((L, M_V + LANES), F32, 8), ((L, L), F32, 8))),
        name="mlstm",
    )(qk, v_so, gates, gate_bias, v_so, g_mlstm)


def _attn_kernel(q_ref, k_ref, v_ref, lam_ref, g_ref, o_ref, vt_ref, acc_ref, m_ref, sa_ref, sb_ref,
                 mxa_ref, mxb_ref, *, lam_init, seq):
    t = ATT_T
    qi = pl.program_id(1)

    @pl.when(qi == 0)
    def _():
        ones_row = jnp.where(lax.broadcasted_iota(jnp.int32, (ATT_VROWS - B_V_DIM, t), 0) == 0, 1.0, 0.0)
        for kb in range(seq // t):
            v_t = v_ref[kb * t:(kb + 1) * t, :].astype(F32).T
            vt_ref[kb] = jnp.concatenate([v_t, ones_row], axis=0).astype(BF16)

    def query_block(qb):
        q = q_ref[pl.ds(pl.multiple_of(qb * t, t), t), :]
        qlane = lax.broadcasted_iota(jnp.int32, q.shape, 1)
        zero = jnp.zeros_like(q)
        return jnp.concatenate([jnp.where(qlane < B_HEAD_DIM, q, zero),
                                jnp.where(qlane >= B_HEAD_DIM, q, zero)], axis=0)

    qcat = query_block(qi)
    shift = CHUNK.bit_length() - 1
    kchunk = lax.broadcasted_iota(jnp.int32, (t, 2 * t), 0) >> shift
    qchunk = (lax.broadcasted_iota(jnp.int32, (t, 2 * t), 1) & (t - 1)) >> shift
    diag_mask = kchunk <= qchunk

    def scores(kb, s_ref, mx_ref, mask=None, qc=qcat):
        kblk = k_ref[pl.ds(pl.multiple_of(kb * t, t), t), :]
        s_t = _dot_nt(kblk, qc)
        if mask is not None:
            s_t = jnp.where(mask, s_t, NEG)
        s_ref[:, :2 * t] = s_t
        mx_ref[...] = jnp.max(s_t, axis=0, keepdims=True)

    def absorb(kb, s_t, mx):
        m_old = m_ref[...]
        m_new = jnp.maximum(m_old, mx)
        p = jnp.exp2(s_t - m_new).astype(BF16)
        acc_ref[...] = jnp.exp2(m_old - m_new) * acc_ref[...] + _dot(vt_ref[kb], p)
        m_ref[...] = m_new

    acc_ref[...] = jnp.zeros(acc_ref.shape, F32)
    m_ref[...] = jnp.full(m_ref.shape, NEG, F32)

    @pl.when(qi == 0)
    def _():
        scores(0, sa_ref, mxa_ref)

    def pair(kb):
        scores(kb + 1, sb_ref, mxb_ref)
        absorb(kb, sa_ref[:, :2 * t], mxa_ref[...])
        scores(kb + 2, sa_ref, mxa_ref)
        absorb(kb + 1, sb_ref[:, :2 * t], mxb_ref[...])

    def body(it, carry):
        for u in range(0, 8, 2):
            pair(8 * it + u)
        return carry

    lax.fori_loop(0, qi >> 3, body, 0)

    @pl.when((qi & 4) == 4)
    def _():
        pair(qi & ~7)
        pair((qi & ~7) + 2)

    def finish():
        lf = lam_ref[...]
        lam = (jnp.exp(jnp.sum(lf[0:1] * lf[1:2], axis=1, keepdims=True))
               - jnp.exp(jnp.sum(lf[2:3] * lf[3:4], axis=1, keepdims=True)) + lam_init)
        acc = acc_ref[...]
        o1 = acc[:B_V_DIM, :t] / acc[B_V_DIM:B_V_DIM + 1, :t]
        o2 = acc[:B_V_DIM, t:] / acc[B_V_DIM:B_V_DIM + 1, t:]
        o_t = o1 - lam * o2
        ms = jnp.mean(o_t * o_t, axis=0, keepdims=True)
        o = (o_t * lax.rsqrt(ms + EPS)).T
        o_ref[...] = (o * (g_ref[...] * (1.0 - lam_init))).astype(o_ref.dtype)
        q_next = jnp.minimum(qi + 1, pl.num_programs(1) - 1)
        scores(0, sa_ref, mxa_ref, qc=query_block(q_next))

    def tail(rem):
        if rem & 2:
            pair(qi - rem)
        if rem & 1:
            scores(qi, sb_ref, mxb_ref, diag_mask)
            absorb(qi - 1, sa_ref[:, :2 * t], mxa_ref[...])
            absorb(qi, sb_ref[:, :2 * t], mxb_ref[...])
        else:
            s_d = jnp.where(diag_mask, sa_ref[:, :2 * t], NEG)
            absorb(qi, s_d, jnp.max(s_d, axis=0, keepdims=True))
        finish()

    for rem in range(4):
        pl.when((qi & 3) == rem)(functools.partial(tail, rem))


def _attention(qkb, v_sg, lambdas, g_diff, l, lam_init):
    s = qkb.shape[0]
    t = ATT_T
    kern = functools.partial(_attn_kernel, lam_init=lam_init, seq=s)
    return pl.pallas_call(
        kern,
        out_shape=jax.ShapeDtypeStruct((s, B_V), BF16),
        grid=(B_HEADS, s // t),
        in_specs=[pl.BlockSpec((s, LANES), lambda h, i: (0, h)),
                  pl.BlockSpec((s, LANES), lambda h, i: (0, B_HEADS + h)),
                  pl.BlockSpec((s, LANES), lambda h, i: (0, h)),
                  pl.BlockSpec((None, 4, B_HEAD_DIM), lambda h, i: (l, 0, 0)),
                  pl.BlockSpec((None, 1, LANES), lambda h, i: (l, 0, h))],
        out_specs=pl.BlockSpec((t, LANES), lambda h, i: (i, h)),
        scratch_shapes=[pltpu.VMEM((s // t, ATT_VROWS, t), BF16),
                        pltpu.VMEM((ATT_VROWS, 2 * t), F32),
                        pltpu.VMEM((1, 2 * t), F32),
                        pltpu.VMEM((t, 2 * t + ATT_PAD), F32),
                        pltpu.VMEM((t, 2 * t + ATT_PAD), F32),
                        pltpu.VMEM((1, 2 * t), F32),
                        pltpu.VMEM((1, 2 * t), F32)],
        compiler_params=_params(("arbitrary", "arbitrary"), _vmem_limit(
            ((s, LANES), BF16, 6), ((s // t, ATT_VROWS, t), BF16, 1), ((t, LANES), BF16, 6),
            ((t, 2 * t), F32, 8), ((ATT_VROWS, 2 * t), F32, 4))),
        name="diff_attn",
    )(qkb, qkb, v_sg, lambdas, g_diff)


def _mix_out_kernel(a_ref, b_ref, ga_ref, gb_ref, wa_ref, wb_ref, wo_ref, x_ref, g_ref, gt_ref, gn_ref, sc_ref,
                    sh_ref, o_ref, hn_ref, m_ref):
    a = a_ref[...]
    b = b_ref[...]
    for cs in _col_subtiles(m_ref.shape[1]):
        ya = _dot(a, wa_ref[:, cs])
        yb = _dot(b, wb_ref[:, cs])
        m_ref[:, cs] = (ga_ref[:, cs].astype(F32) * ya + gb_ref[:, cs].astype(F32) * yb).astype(m_ref.dtype)
    y = _dot(m_ref[...], wo_ref[...])
    _residual_epilogue(y, x_ref, g_ref, gt_ref, (gn_ref, sc_ref, sh_ref), o_ref, hn_ref)


def _mix_out(ya_in, yb_in, v_sg, w_a, w_b, w_out, x, g_post, mod, g_next, l):
    s, d = x.shape
    tm = min(MXU_COLS, s)
    row = lambda c: pl.BlockSpec((tm, d), lambda i: (i, c))
    weight = pl.BlockSpec((None, d, d), lambda i: (l, 0, 0), pipeline_mode=pl.Buffered(1))
    vec = lambda blk: pl.BlockSpec((None, 1, d), lambda i: (l, 0, blk))
    return pl.pallas_call(
        _mix_out_kernel,
        out_shape=[jax.ShapeDtypeStruct((s, d), F32), jax.ShapeDtypeStruct((s, d), BF16)],
        grid=(s // tm,),
        in_specs=[row(0), row(0), row(1), row(2), weight, weight, weight, row(0), vec(0), vec(2), vec(0), vec(4),
                  vec(3)],
        out_specs=[row(0), row(0)],
        scratch_shapes=[pltpu.VMEM((tm, d), BF16)],
        compiler_params=_params(("parallel",), _vmem_limit(
            ((d, d), BF16, 3), ((tm, d), BF16, 11), ((tm, d), F32, 7))),
        name="mix_out",
    )(ya_in, yb_in, v_sg, v_sg, w_a, w_b, w_out, x, g_post, mod, g_next, mod, mod)


def _residual_epilogue(y, x_ref, g_ref, gt_ref, nxt_refs, o_ref, hn_ref):
    ms = jnp.mean(y * y, axis=-1, keepdims=True)
    x_new = x_ref[...] + (y * (gt_ref[...] * g_ref[...])) * lax.rsqrt(ms + EPS)
    o_ref[...] = x_new
    if nxt_refs is not None:
        gn_ref, sc_ref, sh_ref = nxt_refs
        hn_ref[...] = _modulated_norm(x_new, gn_ref[...], sc_ref[...], sh_ref[...]).astype(hn_ref.dtype)


def _proj_res_kernel(a_ref, w_ref, x_ref, g_ref, gt_ref, *rest, with_next):
    if with_next:
        *nxt_refs, o_ref, hn_ref = rest
    else:
        (o_ref,), nxt_refs, hn_ref = rest, None, None
    _residual_epilogue(_dot(a_ref[...], w_ref[...]), x_ref, g_ref, gt_ref, nxt_refs, o_ref, hn_ref)


def _proj_res(a, w, x, gains, mod, l, gt_blk, tm, name, nxt=None):
    s, k = a.shape
    d = w.shape[2]
    tm = min(tm, s)
    rows = lambda width: pl.BlockSpec((tm, width), lambda i: (i, 0))
    vec = lambda arr, lay, blk: (arr, pl.BlockSpec((None, 1, d), lambda i: (lay, 0, blk)))
    operands = [(a, rows(k)),
                (w, pl.BlockSpec((None, k, d), lambda i: (l, 0, 0), pipeline_mode=pl.Buffered(1))),
                (x, rows(d)), vec(gains, l, 0), vec(mod, l, gt_blk)]
    out_shape = [jax.ShapeDtypeStruct((s, d), F32)]
    out_specs = [rows(d)]
    if nxt is not None:
        n_gains, n_l, n_sc, n_sh = nxt
        operands += [vec(n_gains, n_l, 0), vec(mod, n_l, n_sc), vec(mod, n_l, n_sh)]
        out_shape.append(jax.ShapeDtypeStruct((s, d), BF16))
        out_specs.append(rows(d))
    outs = pl.pallas_call(
        functools.partial(_proj_res_kernel, with_next=nxt is not None),
        out_shape=out_shape,
        grid=(s // tm,),
        in_specs=[spec for _, spec in operands],
        out_specs=out_specs,
        compiler_params=_params(("parallel",), _vmem_limit(
            ((tm, k), BF16, 2), ((k, d), BF16, 1), ((tm, d), F32, 10), ((tm, d), BF16, 2))),
        name=name,
    )(*[arr for arr, _ in operands])
    return outs if nxt is not None else (outs[0], None)


def _ffn_up_kernel(h_ref, wg_ref, wu_ref, cw_ref, cb_ref, o_ref, wgb_ref, wub_ref, buf_ref, *, tm):
    _cast_weight(wg_ref, wgb_ref)
    _cast_weight(wu_ref, wub_ref)

    @pl.when(_first_row_tile())
    def _():
        buf_ref[0:SUBLANES, :] = jnp.zeros((SUBLANES, buf_ref.shape[1]), F32)

    h = h_ref[...]
    subtiles = _col_subtiles(o_ref.shape[1])
    gates = [_silu(_causal_conv(_dot(h, wgb_ref[:, cs]), buf_ref, cw_ref, cb_ref, cs, tm)) for cs in subtiles]
    for cs, g in zip(subtiles, gates):
        o_ref[:, cs] = (g * _dot(h, wub_ref[:, cs])).astype(o_ref.dtype)


def _ffn_up(h, w_gate, w_up, conv_w, conv_b, l):
    s, d = h.shape
    n = w_gate.shape[2]
    tm, tn = min(1024, s), 512
    kconv = conv_w.shape[1]
    kern = functools.partial(_ffn_up_kernel, tm=tm)
    return pl.pallas_call(
        kern,
        out_shape=jax.ShapeDtypeStruct((s, n), BF16),
        grid=(n // tn, s // tm),
        in_specs=[pl.BlockSpec((tm, d), lambda j, i: (i, 0)),
                  pl.BlockSpec((None, d, tn), lambda j, i: (l, 0, j)),
                  pl.BlockSpec((None, d, tn), lambda j, i: (l, 0, j)),
                  pl.BlockSpec((None, kconv, tn), lambda j, i: (l, 0, j)),
                  pl.BlockSpec((None, 1, tn), lambda j, i: (l, 0, j))],
        out_specs=pl.BlockSpec((tm, tn), lambda j, i: (i, j)),
        scratch_shapes=[pltpu.VMEM((d, tn), BF16), pltpu.VMEM((d, tn), BF16),
                        pltpu.VMEM((2 * SUBLANES, tn), F32)],
        compiler_params=_params(("parallel", "arbitrary"), _vmem_limit(
            ((tm, d), BF16, 2), ((d, tn), F32, 4), ((d, tn), BF16, 2), ((tm, tn), BF16, 2),
            ((tm, tn), F32, 6))),
        name="ffn_up",
    )(h, w_gate, w_up, conv_w, conv_b)


def _rope_tables(positions):
    half = ROPE_DIM // 2
    inv = jnp.power(ROPE_THETA, -(jnp.arange(half, dtype=F32) * 2.0 / ROPE_DIM))
    ang = positions.astype(F32)[:, None] * inv
    cos, sin = jnp.cos(ang), jnp.sin(ang)
    s = positions.shape[0]
    rest = B_HEAD_DIM - ROPE_DIM
    reps = LANES // B_HEAD_DIM
    ca = jnp.tile(jnp.concatenate([cos, cos, jnp.ones((s, rest), F32)], axis=1), (1, reps))
    cb = jnp.tile(jnp.concatenate([-sin, jnp.zeros((s, B_HEAD_DIM - half), F32)], axis=1), (1, reps))
    cc = jnp.tile(jnp.concatenate([jnp.zeros((s, half), F32), sin, jnp.zeros((s, rest), F32)], axis=1),
                  (1, reps))
    return ca, cb, cc


def kernel(x, c, positions, w_ada, b_ada, g_pre_mix, g_post_mix, g_pre_ffn, g_post_ffn, w_in, conv_qk_w, conv_qk_b, b_igate, b_fgate, g_mlstm, lambdas, g_diff, w_a, w_b, w_out, w_gate, w_up, conv_ffn_w, conv_ffn_b, w_down):
    batch, s, d = x.shape
    depth = w_in.shape[0]
    assert batch == 1 and s % 1024 == 0 and d == M_V == B_V
    xs = x.reshape(s, d)
    tm_proj = min(1024, s)

    mod = _adaln(c.reshape(d, 1), w_ada, b_ada)
    ca, cb, cc = _rope_tables(positions[0])
    rope_specs = [pl.BlockSpec((tm_proj, LANES), lambda j, i: (i, 0))] * 3

    def gains(g):
        return g.reshape(depth, 1, g.shape[1])

    g_pre_mix, g_post_mix, g_pre_ffn, g_post_ffn = map(gains, (g_pre_mix, g_post_mix, g_pre_ffn, g_post_ffn))
    g_mlstm3, g_diff3 = gains(g_mlstm), gains(g_diff)
    conv_qk_b3, conv_ffn_b3 = gains(conv_qk_b), gains(conv_ffn_b)
    o_vm = 2 * M_QK
    o_om = o_vm + M_V
    o_gates = o_om + M_V
    o_tail = o_gates + 2 * M_HEADS
    gate_bias = jnp.concatenate([b_igate, b_fgate, jnp.zeros((depth, LANES - 2 * M_HEADS), F32)],
                                axis=1).reshape(depth, 1, LANES)
    assert math.log2(M_QK_DIM) % 2 == 0
    w_in_t = jnp.swapaxes(w_in, 1, 2)
    w_head = w_in_t[:, :o_gates + LANES, :].astype(BF16)
    w_tail = _rebase_rows(w_in_t, o_tail)
    w_a16, w_b16, w_out16, w_down16 = (w.astype(BF16) for w in (w_a, w_b, w_out, w_down))
    tn, tw = 1024, 2048

    h = _prenorm(xs, g_pre_mix, mod, 0, 1, 0)
    for l in range(depth):
        lam_init = 0.8 - 0.6 * math.exp(-0.3 * l)

        qk = _proj(functools.partial(_proj_conv_kernel, tm=tm_proj), h, w_head, l, 0, 2 * M_QK, BF16, tn=tn,
                   extra=(conv_qk_w, conv_qk_b3),
                   extra_specs=[pl.BlockSpec((None, conv_qk_w.shape[1], tn), lambda j, i: (l, 0, j)),
                                pl.BlockSpec((None, 1, tn), lambda j, i: (l, 0, j))],
                   scratch=[pltpu.VMEM((2 * SUBLANES, tn), F32)],
                   extra_vmem=(((tm_proj, tn), F32, 3),), name="proj_conv")
        v_so = _proj(functools.partial(_proj_mixed_kernel, n_plain=M_V // tw), h, w_head, l, o_vm, 2 * M_V,
                     BF16, tn=tw, name="proj_v_so")
        gates = _proj(_proj_plain_kernel, h, w_head, l, o_gates, LANES, F32, tn=LANES, name="proj_gates")
        qkb = _proj(functools.partial(_proj_rope_kernel, n_qtiles=B_QK // tw, qscale=B_HEAD_DIM ** -0.5 * LOG2E),
                    h, w_tail, l, 0, 2 * B_QK, BF16, tn=tw, extra=(ca, cb, cc), extra_specs=rope_specs,
                    extra_vmem=(((tm_proj, LANES), F32, 9),), name="proj_rope")
        v_sg = _proj(functools.partial(_proj_mixed_kernel, n_plain=B_V // tw), h, w_tail, l, 2 * B_QK,
                     B_V + 2 * d, BF16, tn=tw, name="proj_v_sg")

        y_a_in = _mlstm(qk, v_so, gates, gate_bias, g_mlstm3, l)
        y_b_in = _attention(qkb, v_sg, lambdas, g_diff3, l, lam_init)
        xs, h = _mix_out(y_a_in, y_b_in, v_sg, w_a16, w_b16, w_out16, xs, g_post_mix, mod, g_pre_ffn, l)

        act = _ffn_up(h, w_gate, w_up, conv_ffn_w, conv_ffn_b3, l)
        nxt = (g_pre_mix, l + 1, 1, 0) if l + 1 < depth else None
        xs, h = _proj_res(act, w_down16, xs, g_post_ffn, mod, l, 5, 256, "ffn_down", nxt=nxt)

    return xs.reshape(batch, s, d)
```

```python
import functools
import math

import jax
import jax.numpy as jnp
from jax import lax
from jax.experimental import pallas as pl
from jax.experimental.pallas import tpu as pltpu

F32 = jnp.float32
BF16 = jnp.bfloat16

M_HEADS = 4
M_QK_DIM = 256
M_V_DIM = 512
M_QK = M_HEADS * M_QK_DIM
M_V = M_HEADS * M_V_DIM
B_HEADS = 16
B_HEAD_DIM = 64
B_V_DIM = 2 * B_HEAD_DIM
B_QK = B_HEADS * 2 * B_HEAD_DIM
B_V = B_HEADS * B_V_DIM
CHUNK = 64
ROPE_THETA = 500000.0
ROPE_DIM = B_HEAD_DIM // 4
EPS = 1e-6

LANES = 128
SUBLANES = 8
MXU_COLS = 256
V7X_VMEM_REQUEST_CAP = 60000 * 1024
V7X_COMPILER_SCRATCH = 8 * 1024 * 1024

MLSTM_L = 256
ATT_T = 512
ATT_VROWS = B_V_DIM + 16
ATT_PAD = LANES
LOG2E = math.log2(math.e)
NEG = -1e30


def _nbytes(shape, dtype):
    return math.prod(shape) * jnp.dtype(dtype).itemsize


def _vmem_limit(*terms):
    total = sum(_nbytes(s, d) * n for s, d, n in terms) + V7X_COMPILER_SCRATCH
    return int(min(V7X_VMEM_REQUEST_CAP, total))


def _params(sem, limit):
    return pltpu.CompilerParams(dimension_semantics=sem, vmem_limit_bytes=limit)


def _silu(x):
    return x * jax.nn.sigmoid(x)


def _dot(a, b):
    return jnp.dot(a, b, preferred_element_type=F32)


def _dot_nt(a, b):
    return lax.dot_general(a, b, (((1,), (1,)), ((), ())), preferred_element_type=F32)


def _dot_tn(a, b):
    return lax.dot_general(a, b, (((0,), (0,)), ((), ())), preferred_element_type=F32)


def _adaln_kernel(c_ref, w_ref, b_ref, o_ref):
    c = c_ref[...]
    o_ref[...] = jnp.sum(_silu(c) * w_ref[...], axis=0, keepdims=True) + b_ref[...]


def _adaln(c_col, w_ada, b_ada):
    depth, d, n = w_ada.shape
    tn = 1024
    return pl.pallas_call(
        _adaln_kernel,
        out_shape=jax.ShapeDtypeStruct((depth, 1, n), F32),
        grid=(depth, n // tn),
        in_specs=[pl.BlockSpec((d, 1), lambda l, j: (0, 0)),
                  pl.BlockSpec((None, d, tn), lambda l, j: (l, 0, j)),
                  pl.BlockSpec((None, 1, tn), lambda l, j: (l, 0, j))],
        out_specs=pl.BlockSpec((None, 1, tn), lambda l, j: (l, 0, j)),
        compiler_params=_params(("parallel", "parallel"),
                                _vmem_limit(((d, LANES), F32, 2), ((d, tn), F32, 3))),
        name="adaln",
    )(c_col, w_ada, b_ada.reshape(depth, 1, n))


def _modulated_norm(x, g, sc, sh):
    ms = jnp.mean(x * x, axis=-1, keepdims=True)
    return (x * (g * (1.0 + sc))) * lax.rsqrt(ms + EPS) + sh


def _prenorm_kernel(x_ref, g_ref, sc_ref, sh_ref, o_ref):
    o_ref[...] = _modulated_norm(x_ref[...], g_ref[...], sc_ref[...], sh_ref[...]).astype(o_ref.dtype)


def _prenorm(x, gains, mod, l, sc_blk, sh_blk):
    s, d = x.shape
    tm = min(512, s)
    return pl.pallas_call(
        _prenorm_kernel,
        out_shape=jax.ShapeDtypeStruct((s, d), BF16),
        grid=(s // tm,),
        in_specs=[pl.BlockSpec((tm, d), lambda i: (i, 0)),
                  pl.BlockSpec((None, 1, d), lambda i: (l, 0, 0)),
                  pl.BlockSpec((None, 1, d), lambda i: (l, 0, sc_blk)),
                  pl.BlockSpec((None, 1, d), lambda i: (l, 0, sh_blk))],
        out_specs=pl.BlockSpec((tm, d), lambda i: (i, 0)),
        compiler_params=_params(("parallel",), _vmem_limit(((tm, d), F32, 4), ((tm, d), BF16, 2))),
        name="prenorm",
    )(x, gains, mod, mod)


REBASE_ROWS = 512


def _rebase_kernel(cur_ref, nxt_ref, o_ref):
    skip = nxt_ref.shape[0]
    o_ref[...] = jnp.concatenate([cur_ref[skip:, :], nxt_ref[...]], axis=0).astype(o_ref.dtype)


def _rebase_rows(w_t, row0):
    depth, rows, d = w_t.shape
    r = REBASE_ROWS
    base = row0 - SUBLANES
    n = rows - row0
    assert base % r == 0 and n % r == 0
    blk0, per = base // r, r // SUBLANES
    return pl.pallas_call(
        _rebase_kernel,
        out_shape=jax.ShapeDtypeStruct((depth, n, d), BF16),
        grid=(depth, n // r),
        in_specs=[pl.BlockSpec((None, r, d), lambda l, j: (l, blk0 + j, 0)),
                  pl.BlockSpec((None, SUBLANES, d), lambda l, j: (l, (blk0 + j + 1) * per, 0))],
        out_specs=pl.BlockSpec((None, r, d), lambda l, j: (l, j, 0)),
        compiler_params=_params(("parallel", "parallel"), _vmem_limit(
            ((r, d), F32, 3), ((SUBLANES, d), F32, 2), ((r, d), BF16, 3))),
        name="rebase_w_in",
    )(w_t, w_t)


def _first_row_tile():
    return pl.program_id(1) == 0


def _cast_weight(w_ref, wb_ref):
    @pl.when(_first_row_tile())
    def _():
        wb_ref[...] = w_ref[...].astype(BF16)


def _causal_conv(acc, buf_ref, cw_ref, cb_ref, cs, tm):
    kconv = cw_ref.shape[0]
    buf_ref[SUBLANES:2 * SUBLANES, cs] = acc[0:SUBLANES, :]
    out = cb_ref[:, cs] + cw_ref[kconv - 1:kconv, cs] * acc
    for j in range(kconv - 1):
        dist = kconv - 1 - j
        top = buf_ref[SUBLANES - dist:2 * SUBLANES - dist, cs]
        shifted = jnp.concatenate([top, pltpu.roll(acc, dist, 0)[SUBLANES:, :]], axis=0)
        out = out + cw_ref[j:j + 1, cs] * shifted
    buf_ref[0:SUBLANES, cs] = acc[tm - SUBLANES:tm, :]
    return out


def _col_subtiles(n):
    sub = min(MXU_COLS, n)
    return [slice(c * sub, (c + 1) * sub) for c in range(n // sub)]


def _proj_conv_kernel(h_ref, w_ref, cw_ref, cb_ref, o_ref, buf_ref, *, tm):
    @pl.when(_first_row_tile())
    def _():
        buf_ref[0:SUBLANES, :] = jnp.zeros((SUBLANES, buf_ref.shape[1]), F32)

    h = h_ref[...]
    for cs in _col_subtiles(o_ref.shape[1]):
        out = _causal_conv(_dot_nt(h, w_ref[cs, :]), buf_ref, cw_ref, cb_ref, cs, tm)
        o_ref[:, cs] = _silu(out).astype(o_ref.dtype)


def _proj_plain_kernel(h_ref, w_ref, o_ref):
    h = h_ref[...]
    for cs in _col_subtiles(o_ref.shape[1]):
        o_ref[:, cs] = _dot_nt(h, w_ref[cs, :]).astype(o_ref.dtype)


def _proj_mixed_kernel(h_ref, w_ref, o_ref, *, n_plain):
    j = pl.program_id(0)

    @pl.when(j < n_plain)
    def _():
        _proj_plain_kernel(h_ref, w_ref, o_ref)

    @pl.when(j >= n_plain)
    def _():
        h = h_ref[...]
        for cs in _col_subtiles(o_ref.shape[1]):
            o_ref[:, cs] = jax.nn.sigmoid(_dot_nt(h, w_ref[cs, :])).astype(o_ref.dtype)


def _proj_rope_kernel(h_ref, w_ref, ca_ref, cb_ref, cc_ref, o_ref, *, n_qtiles, qscale):
    scale = jnp.where(pl.program_id(0) < n_qtiles, qscale, 1.0).astype(F32)
    ca = ca_ref[...] * scale
    cb = cb_ref[...] * scale
    cc = cc_ref[...] * scale
    h = h_ref[...]
    for cs in _col_subtiles(o_ref.shape[1]):
        acc = _dot_nt(h, w_ref[cs, :])
        for s in range(acc.shape[1] // LANES):
            xs = acc[:, s * LANES:(s + 1) * LANES]
            r = (xs * ca + pltpu.roll(xs, LANES - ROPE_DIM // 2, 1) * cb
                 + pltpu.roll(xs, ROPE_DIM // 2, 1) * cc)
            o_ref[:, cs.start + s * LANES:cs.start + (s + 1) * LANES] = r.astype(o_ref.dtype)


def _proj(kernel, h, w, l, col0, n, out_dtype, *, tn, extra=(), extra_specs=(), scratch=(), extra_vmem=(),
          name):
    s, d = h.shape
    tm = min(1024, s)
    blk0 = col0 // tn
    assert blk0 * tn == col0 and n % tn == 0
    return pl.pallas_call(
        kernel,
        out_shape=jax.ShapeDtypeStruct((s, n), out_dtype),
        grid=(n // tn, s // tm),
        in_specs=[pl.BlockSpec((tm, d), lambda j, i: (i, 0)),
                  pl.BlockSpec((None, tn, d), lambda j, i: (l, blk0 + j, 0))] + list(extra_specs),
        out_specs=pl.BlockSpec((tm, tn), lambda j, i: (i, j)),
        scratch_shapes=list(scratch),
        compiler_params=_params(("parallel", "arbitrary"), _vmem_limit(
            ((tm, d), BF16, 2), ((d, tn), BF16, 2), ((tm, tn), out_dtype, 2),
            ((tm, MXU_COLS), F32, 8), *extra_vmem)),
        name=name,
    )(h, w, *extra)


def _mlstm_kernel(qk_ref, v_ref, gate_ref, bias_ref, so_ref, g_ref, o_ref, st_ref, m_ref):
    L = MLSTM_L

    @pl.when(pl.program_id(0) == 0)
    def _():
        st_ref[...] = jnp.zeros(st_ref.shape, F32)
        m_ref[...] = jnp.zeros(m_ref.shape, F32)

    ipre = gate_ref[...] + bias_ref[...]
    fpre = pltpu.roll(ipre, LANES - M_HEADS, 1)
    logf = jnp.minimum(fpre, 0.0) - jnp.log1p(jnp.exp(-jnp.abs(fpre)))

    row = lax.broadcasted_iota(jnp.int32, (L, L), 0)
    col = lax.broadcasted_iota(jnp.int32, (L, L), 1)
    tri = col <= row
    tri_b = jnp.where(tri, 1.0, 0.0).astype(BF16)
    hi = logf.astype(BF16)
    r1 = logf - hi.astype(F32)
    mid = r1.astype(BF16)
    lo = (r1 - mid.astype(F32)).astype(BF16)
    b = _dot(tri_b, hi) + _dot(tri_b, mid) + _dot(tri_b, lo)
    a = ipre - b
    a_t = a.T
    m_all = m_ref[...]
    lane = lax.broadcasted_iota(jnp.int32, (1, LANES), 1)
    ones_col = jnp.where(lax.broadcasted_iota(jnp.int32, (L, LANES), 1) == 0, 1.0, 0.0).astype(BF16)
    m_next = m_all

    for h in range(M_HEADS):
        a_row = a_t[h:h + 1, :]
        a_col = a[:, h:h + 1]
        b_col = b[:, h:h + 1]
        m_prev = m_all[:, h:h + 1]
        amask = jnp.where(tri, a_row, NEG)
        m_run = jnp.maximum(jnp.max(amask, axis=1, keepdims=True), m_prev)
        dmat = jnp.exp(amask - m_run)
        w_inter = jnp.exp(m_prev - m_run)
        q = qk_ref[:, h * M_QK_DIM:(h + 1) * M_QK_DIM] * jnp.asarray(M_QK_DIM ** -0.5, BF16)
        k = qk_ref[:, M_QK + h * M_QK_DIM:M_QK + (h + 1) * M_QK_DIM]
        v_ext = jnp.concatenate([v_ref[:, h * M_V_DIM:(h + 1) * M_V_DIM], ones_col], axis=1)
        sd = (_dot_nt(q, k) * dmat).astype(BF16)
        st = st_ref[h]
        tot = w_inter * _dot(q, st.astype(BF16)) + _dot(sd, v_ext)
        num = tot[:, :M_V_DIM]
        den = tot[:, M_V_DIM:M_V_DIM + 1]
        hval = num / jnp.maximum(jnp.abs(den), jnp.exp(-(b_col + m_run)))
        m_last = m_run[L - 1:L, :]
        w_s = jnp.exp(a_col - m_last)
        decay = jnp.exp(m_prev - m_last)
        rhs = (w_s * v_ext.astype(F32)).astype(BF16)
        st_ref[h] = decay * st + _dot_tn(k, rhs)
        m_next = jnp.where(lane == h, b_col[L - 1:L, :] + m_last, m_next)
        ms = jnp.mean(hval * hval, axis=-1, keepdims=True)
        sl = slice(h * M_V_DIM, (h + 1) * M_V_DIM)
        hn = hval * lax.rsqrt(ms + EPS) * g_ref[:, sl]
        o_ref[:, sl] = (so_ref[:, sl].astype(F32) * hn).astype(o_ref.dtype)

    m_ref[...] = m_next


def _mlstm(qk, v_so, gates, gate_bias, g_mlstm, l):
    s = qk.shape[0]
    L = MLSTM_L
    st_shape = (M_HEADS, M_QK_DIM, M_V_DIM + LANES)
    return pl.pallas_call(
        _mlstm_kernel,
        out_shape=jax.ShapeDtypeStruct((s, M_V), BF16),
        grid=(s // L,),
        in_specs=[pl.BlockSpec((L, 2 * M_QK), lambda c: (c, 0)),
                  pl.BlockSpec((L, M_V), lambda c: (c, 0)),
                  pl.BlockSpec((L, LANES), lambda c: (c, 0)),
                  pl.BlockSpec((None, 1, LANES), lambda c: (l, 0, 0)),
                  pl.BlockSpec((L, M_V), lambda c: (c, 1)),
                  pl.BlockSpec((None, 1, M_V), lambda c: (l, 0, 0))],
        out_specs=pl.BlockSpec((L, M_V), lambda c: (c, 0)),
        scratch_shapes=[pltpu.VMEM(st_shape, F32), pltpu.VMEM((1, LANES), F32)],
        compiler_params=_params(("arbitrary",), _vmem_limit(
            ((L, 2 * M_QK), BF16, 2), ((L, M_V), BF16, 6), ((L, LANES), F32, 2),
            (st_shape, F32, 2), ((L, M_V + LANES), F32, 8), ((L, L), F32, 8))),
        name="mlstm",
    )(qk, v_so, gates, gate_bias, v_so, g_mlstm)


def _attn_kernel(q_ref, k_ref, v_ref, lam_ref, g_ref, o_ref, vt_ref, acc_ref, m_ref, s0_ref, s1_ref, s2_ref,
                 mx0_ref, mx1_ref, mx2_ref, *, lam_init, seq):
    t = ATT_T
    qi = pl.program_id(1)

    @pl.when(qi == 0)
    def _():
        ones_row = jnp.where(lax.broadcasted_iota(jnp.int32, (ATT_VROWS - B_V_DIM, t), 0) == 0, 1.0, 0.0)
        for kb in range(seq // t):
            v_t = v_ref[kb * t:(kb + 1) * t, :].astype(F32).T
            vt_ref[kb] = jnp.concatenate([v_t, ones_row], axis=0).astype(BF16)

    def query_block(qb):
        q = q_ref[pl.ds(pl.multiple_of(qb * t, t), t), :]
        qlane = lax.broadcasted_iota(jnp.int32, q.shape, 1)
        zero = jnp.zeros_like(q)
        return jnp.concatenate([jnp.where(qlane < B_HEAD_DIM, q, zero),
                                jnp.where(qlane >= B_HEAD_DIM, q, zero)], axis=0)

    qcat = query_block(qi)
    bufs = ((s0_ref, mx0_ref), (s1_ref, mx1_ref), (s2_ref, mx2_ref))
    shift = CHUNK.bit_length() - 1
    kchunk = lax.broadcasted_iota(jnp.int32, (t, 2 * t), 0) >> shift
    qchunk = (lax.broadcasted_iota(jnp.int32, (t, 2 * t), 1) & (t - 1)) >> shift
    diag_mask = kchunk <= qchunk

    def scores(kb, slot, mask=None, qc=qcat):
        s_ref, mx_ref = bufs[slot]
        kblk = k_ref[pl.ds(pl.multiple_of(kb * t, t), t), :]
        s_t = _dot_nt(kblk, qc)
        if mask is not None:
            s_t = jnp.where(mask, s_t, NEG)
        s_ref[:, :2 * t] = s_t
        mx_ref[...] = jnp.max(s_t, axis=0, keepdims=True)

    def absorb(kb, slot, mask=None):
        s_ref, mx_ref = bufs[slot]
        s_t = s_ref[:, :2 * t]
        if mask is None:
            mx = mx_ref[...]
        else:
            s_t = jnp.where(mask, s_t, NEG)
            mx = jnp.max(s_t, axis=0, keepdims=True)
        m_old = m_ref[...]
        m_new = jnp.maximum(m_old, mx)
        p = jnp.exp2(s_t - m_new).astype(BF16)
        acc_ref[...] = jnp.exp2(m_old - m_new) * acc_ref[...] + _dot(vt_ref[kb], p)
        m_ref[...] = m_new

    acc_ref[...] = jnp.zeros(acc_ref.shape, F32)
    m_ref[...] = jnp.full(m_ref.shape, NEG, F32)

    @pl.when(qi == 0)
    def _():
        scores(0, 0)
        scores(1, 1)

    def run(b0, count):
        for u in range(count):
            scores(b0 + u + 2, (u + 2) % 3)
            absorb(b0 + u, u % 3)

    def body(it, carry):
        run(6 * it, 6)
        return carry

    n_loop = jnp.maximum(qi - 2, 0) // 6
    lax.fori_loop(0, n_loop, body, 0)
    left = qi - 6 * n_loop
    extra = left >= 5

    @pl.when(extra)
    def _():
        run(6 * n_loop, 3)

    b_tail = 6 * n_loop + jnp.where(extra, 3, 0)

    def finish():
        lf = lam_ref[...]
        lam = (jnp.exp(jnp.sum(lf[0:1] * lf[1:2], axis=1, keepdims=True))
               - jnp.exp(jnp.sum(lf[2:3] * lf[3:4], axis=1, keepdims=True)) + lam_init)
        acc = acc_ref[...]
        o1 = acc[:B_V_DIM, :t] / acc[B_V_DIM:B_V_DIM + 1, :t]
        o2 = acc[:B_V_DIM, t:] / acc[B_V_DIM:B_V_DIM + 1, t:]
        o_t = o1 - lam * o2
        ms = jnp.mean(o_t * o_t, axis=0, keepdims=True)
        o = (o_t * lax.rsqrt(ms + EPS)).T
        o_ref[...] = (o * (g_ref[...] * (1.0 - lam_init))).astype(o_ref.dtype)
        q_next = query_block(jnp.minimum(qi + 1, pl.num_programs(1) - 1))
        scores(0, 0, qc=q_next)
        scores(1, 1, qc=q_next)

    def tail(rem):
        for u in range(rem + 1):
            if u + 2 <= rem:
                scores(b_tail + u + 2, (u + 2) % 3, diag_mask if u + 2 == rem else None)
            absorb(b_tail + u, u % 3, diag_mask if (u == rem and rem < 2) else None)
        finish()

    for rem in range(5):
        pl.when(qi - b_tail == rem)(functools.partial(tail, rem))


def _attention(qkb, v_sg, lambdas, g_diff, l, lam_init):
    s = qkb.shape[0]
    t = ATT_T
    kern = functools.partial(_attn_kernel, lam_init=lam_init, seq=s)
    return pl.pallas_call(
        kern,
        out_shape=jax.ShapeDtypeStruct((s, B_V), BF16),
        grid=(B_HEADS, s // t),
        in_specs=[pl.BlockSpec((s, LANES), lambda h, i: (0, h)),
                  pl.BlockSpec((s, LANES), lambda h, i: (0, B_HEADS + h)),
                  pl.BlockSpec((s, LANES), lambda h, i: (0, h)),
                  pl.BlockSpec((None, 4, B_HEAD_DIM), lambda h, i: (l, 0, 0)),
                  pl.BlockSpec((None, 1, LANES), lambda h, i: (l, 0, h))],
        out_specs=pl.BlockSpec((t, LANES), lambda h, i: (i, h)),
        scratch_shapes=[pltpu.VMEM((s // t, ATT_VROWS, t), BF16),
                        pltpu.VMEM((ATT_VROWS, 2 * t), F32),
                        pltpu.VMEM((1, 2 * t), F32),
                        pltpu.VMEM((t, 2 * t + ATT_PAD), F32),
                        pltpu.VMEM((t, 2 * t + ATT_PAD), F32),
                        pltpu.VMEM((t, 2 * t + ATT_PAD), F32),
                        pltpu.VMEM((1, 2 * t), F32),
                        pltpu.VMEM((1, 2 * t), F32),
                        pltpu.VMEM((1, 2 * t), F32)],
        compiler_params=_params(("arbitrary", "arbitrary"), _vmem_limit(
            ((s, LANES), BF16, 6), ((s // t, ATT_VROWS, t), BF16, 1), ((t, LANES), BF16, 6),
            ((t, 2 * t + ATT_PAD), F32, 9), ((ATT_VROWS, 2 * t), F32, 4))),
        name="diff_attn",
    )(qkb, qkb, v_sg, lambdas, g_diff)


def _mix_out_kernel(a_ref, b_ref, ga_ref, gb_ref, wa_ref, wb_ref, wo_ref, x_ref, g_ref, gt_ref, gn_ref, sc_ref,
                    sh_ref, o_ref, hn_ref, m_ref):
    a = a_ref[...]
    b = b_ref[...]
    for cs in _col_subtiles(m_ref.shape[1]):
        ya = _dot(a, wa_ref[:, cs])
        yb = _dot(b, wb_ref[:, cs])
        m_ref[:, cs] = (ga_ref[:, cs].astype(F32) * ya + gb_ref[:, cs].astype(F32) * yb).astype(m_ref.dtype)
    y = _dot(m_ref[...], wo_ref[...])
    _residual_epilogue(y, x_ref, g_ref, gt_ref, (gn_ref, sc_ref, sh_ref), o_ref, hn_ref)


def _mix_out(ya_in, yb_in, v_sg, w_a, w_b, w_out, x, g_post, mod, g_next, l):
    s, d = x.shape
    tm = min(MXU_COLS, s)
    row = lambda c: pl.BlockSpec((tm, d), lambda i: (i, c))
    weight = pl.BlockSpec((None, d, d), lambda i: (l, 0, 0), pipeline_mode=pl.Buffered(1))
    vec = lambda blk: pl.BlockSpec((None, 1, d), lambda i: (l, 0, blk))
    return pl.pallas_call(
        _mix_out_kernel,
        out_shape=[jax.ShapeDtypeStruct((s, d), F32), jax.ShapeDtypeStruct((s, d), BF16)],
        grid=(s // tm,),
        in_specs=[row(0), row(0), row(1), row(2), weight, weight, weight, row(0), vec(0), vec(2), vec(0), vec(4),
                  vec(3)],
        out_specs=[row(0), row(0)],
        scratch_shapes=[pltpu.VMEM((tm, d), BF16)],
        compiler_params=_params(("parallel",), _vmem_limit(
            ((d, d), BF16, 3), ((tm, d), BF16, 11), ((tm, d), F32, 7))),
        name="mix_out",
    )(ya_in, yb_in, v_sg, v_sg, w_a, w_b, w_out, x, g_post, mod, g_next, mod, mod)


def _residual_epilogue(y, x_ref, g_ref, gt_ref, nxt_refs, o_ref, hn_ref):
    ms = jnp.mean(y * y, axis=-1, keepdims=True)
    x_new = x_ref[...] + (y * (gt_ref[...] * g_ref[...])) * lax.rsqrt(ms + EPS)
    o_ref[...] = x_new
    if nxt_refs is not None:
        gn_ref, sc_ref, sh_ref = nxt_refs
        hn_ref[...] = _modulated_norm(x_new, gn_ref[...], sc_ref[...], sh_ref[...]).astype(hn_ref.dtype)


def _proj_res_kernel(a_ref, w_ref, x_ref, g_ref, gt_ref, *rest, with_next):
    if with_next:
        *nxt_refs, o_ref, hn_ref = rest
    else:
        (o_ref,), nxt_refs, hn_ref = rest, None, None
    _residual_epilogue(_dot(a_ref[...], w_ref[...]), x_ref, g_ref, gt_ref, nxt_refs, o_ref, hn_ref)


def _proj_res(a, w, x, gains, mod, l, gt_blk, tm, name, nxt=None):
    s, k = a.shape
    d = w.shape[2]
    tm = min(tm, s)
    rows = lambda width: pl.BlockSpec((tm, width), lambda i: (i, 0))
    vec = lambda arr, lay, blk: (arr, pl.BlockSpec((None, 1, d), lambda i: (lay, 0, blk)))
    operands = [(a, rows(k)),
                (w, pl.BlockSpec((None, k, d), lambda i: (l, 0, 0), pipeline_mode=pl.Buffered(1))),
                (x, rows(d)), vec(gains, l, 0), vec(mod, l, gt_blk)]
    out_shape = [jax.ShapeDtypeStruct((s, d), F32)]
    out_specs = [rows(d)]
    if nxt is not None:
        n_gains, n_l, n_sc, n_sh = nxt
        operands += [vec(n_gains, n_l, 0), vec(mod, n_l, n_sc), vec(mod, n_l, n_sh)]
        out_shape.append(jax.ShapeDtypeStruct((s, d), BF16))
        out_specs.append(rows(d))
    outs = pl.pallas_call(
        functools.partial(_proj_res_kernel, with_next=nxt is not None),
        out_shape=out_shape,
        grid=(s // tm,),
        in_specs=[spec for _, spec in operands],
        out_specs=out_specs,
        compiler_params=_params(("parallel",), _vmem_limit(
            ((tm, k), BF16, 2), ((k, d), BF16, 1), ((tm, d), F32, 10), ((tm, d), BF16, 2))),
        name=name,
    )(*[arr for arr, _ in operands])
    return outs if nxt is not None else (outs[0], None)


def _ffn_up_kernel(h_ref, wg_ref, wu_ref, cw_ref, cb_ref, o_ref, wgb_ref, wub_ref, buf_ref, *, tm):
    _cast_weight(wg_ref, wgb_ref)
    _cast_weight(wu_ref, wub_ref)

    @pl.when(_first_row_tile())
    def _():
        buf_ref[0:SUBLANES, :] = jnp.zeros((SUBLANES, buf_ref.shape[1]), F32)

    h = h_ref[...]
    subtiles = _col_subtiles(o_ref.shape[1])
    gates = [_silu(_causal_conv(_dot(h, wgb_ref[:, cs]), buf_ref, cw_ref, cb_ref, cs, tm)) for cs in subtiles]
    for cs, g in zip(subtiles, gates):
        o_ref[:, cs] = (g * _dot(h, wub_ref[:, cs])).astype(o_ref.dtype)


def _ffn_up(h, w_gate, w_up, conv_w, conv_b, l):
    s, d = h.shape
    n = w_gate.shape[2]
    tm, tn = min(1024, s), 512
    kconv = conv_w.shape[1]
    kern = functools.partial(_ffn_up_kernel, tm=tm)
    return pl.pallas_call(
        kern,
        out_shape=jax.ShapeDtypeStruct((s, n), BF16),
        grid=(n // tn, s // tm),
        in_specs=[pl.BlockSpec((tm, d), lambda j, i: (i, 0)),
                  pl.BlockSpec((None, d, tn), lambda j, i: (l, 0, j)),
                  pl.BlockSpec((None, d, tn), lambda j, i: (l, 0, j)),
                  pl.BlockSpec((None, kconv, tn), lambda j, i: (l, 0, j)),
                  pl.BlockSpec((None, 1, tn), lambda j, i: (l, 0, j))],
        out_specs=pl.BlockSpec((tm, tn), lambda j, i: (i, j)),
        scratch_shapes=[pltpu.VMEM((d, tn), BF16), pltpu.VMEM((d, tn), BF16),
                        pltpu.VMEM((2 * SUBLANES, tn), F32)],
        compiler_params=_params(("parallel", "arbitrary"), _vmem_limit(
            ((tm, d), BF16, 2), ((d, tn), F32, 4), ((d, tn), BF16, 2), ((tm, tn), BF16, 2),
            ((tm, tn), F32, 6))),
        name="ffn_up",
    )(h, w_gate, w_up, conv_w, conv_b)


def _rope_tables(positions):
    half = ROPE_DIM // 2
    inv = jnp.power(ROPE_THETA, -(jnp.arange(half, dtype=F32) * 2.0 / ROPE_DIM))
    ang = positions.astype(F32)[:, None] * inv
    cos, sin = jnp.cos(ang), jnp.sin(ang)
    s = positions.shape[0]
    rest = B_HEAD_DIM - ROPE_DIM
    reps = LANES // B_HEAD_DIM
    ca = jnp.tile(jnp.concatenate([cos, cos, jnp.ones((s, rest), F32)], axis=1), (1, reps))
    cb = jnp.tile(jnp.concatenate([-sin, jnp.zeros((s, B_HEAD_DIM - half), F32)], axis=1), (1, reps))
    cc = jnp.tile(jnp.concatenate([jnp.zeros((s, half), F32), sin, jnp.zeros((s, rest), F32)], axis=1),
                  (1, reps))
    return ca, cb, cc


def kernel(x, c, positions, w_ada, b_ada, g_pre_mix, g_post_mix, g_pre_ffn, g_post_ffn, w_in, conv_qk_w, conv_qk_b, b_igate, b_fgate, g_mlstm, lambdas, g_diff, w_a, w_b, w_out, w_gate, w_up, conv_ffn_w, conv_ffn_b, w_down):
    batch, s, d = x.shape
    depth = w_in.shape[0]
    assert batch == 1 and s % 1024 == 0 and d == M_V == B_V
    xs = x.reshape(s, d)
    tm_proj = min(1024, s)

    mod = _adaln(c.reshape(d, 1), w_ada, b_ada)
    ca, cb, cc = _rope_tables(positions[0])
    rope_specs = [pl.BlockSpec((tm_proj, LANES), lambda j, i: (i, 0))] * 3

    def gains(g):
        return g.reshape(depth, 1, g.shape[1])

    g_pre_mix, g_post_mix, g_pre_ffn, g_post_ffn = map(gains, (g_pre_mix, g_post_mix, g_pre_ffn, g_post_ffn))
    g_mlstm3, g_diff3 = gains(g_mlstm), gains(g_diff)
    conv_qk_b3, conv_ffn_b3 = gains(conv_qk_b), gains(conv_ffn_b)
    o_vm = 2 * M_QK
    o_om = o_vm + M_V
    o_gates = o_om + M_V
    o_tail = o_gates + 2 * M_HEADS
    gate_bias = jnp.concatenate([b_igate, b_fgate, jnp.zeros((depth, LANES - 2 * M_HEADS), F32)],
                                axis=1).reshape(depth, 1, LANES)
    assert math.log2(M_QK_DIM) % 2 == 0
    w_in_t = jnp.swapaxes(w_in, 1, 2)
    w_head = w_in_t[:, :o_gates + LANES, :].astype(BF16)
    w_tail = _rebase_rows(w_in_t, o_tail)
    w_a16, w_b16, w_out16, w_down16 = (w.astype(BF16) for w in (w_a, w_b, w_out, w_down))
    tn, tw = 1024, 2048

    h = _prenorm(xs, g_pre_mix, mod, 0, 1, 0)
    for l in range(depth):
        lam_init = 0.8 - 0.6 * math.exp(-0.3 * l)

        qk = _proj(functools.partial(_proj_conv_kernel, tm=tm_proj), h, w_head, l, 0, 2 * M_QK, BF16, tn=tn,
                   extra=(conv_qk_w, conv_qk_b3),
                   extra_specs=[pl.BlockSpec((None, conv_qk_w.shape[1], tn), lambda j, i: (l, 0, j)),
                                pl.BlockSpec((None, 1, tn), lambda j, i: (l, 0, j))],
                   scratch=[pltpu.VMEM((2 * SUBLANES, tn), F32)],
                   extra_vmem=(((tm_proj, tn), F32, 3),), name="proj_conv")
        v_so = _proj(functools.partial(_proj_mixed_kernel, n_plain=M_V // tw), h, w_head, l, o_vm, 2 * M_V,
                     BF16, tn=tw, name="proj_v_so")
        gates = _proj(_proj_plain_kernel, h, w_head, l, o_gates, LANES, F32, tn=LANES, name="proj_gates")
        qkb = _proj(functools.partial(_proj_rope_kernel, n_qtiles=B_QK // tw, qscale=B_HEAD_DIM ** -0.5 * LOG2E),
                    h, w_tail, l, 0, 2 * B_QK, BF16, tn=tw, extra=(ca, cb, cc), extra_specs=rope_specs,
                    extra_vmem=(((tm_proj, LANES), F32, 9),), name="proj_rope")
        v_sg = _proj(functools.partial(_proj_mixed_kernel, n_plain=B_V // tw), h, w_tail, l, 2 * B_QK,
                     B_V + 2 * d, BF16, tn=tw, name="proj_v_sg")

        y_a_in = _mlstm(qk, v_so, gates, gate_bias, g_mlstm3, l)
        y_b_in = _attention(qkb, v_sg, lambdas, g_diff3, l, lam_init)
        xs, h = _mix_out(y_a_in, y_b_in, v_sg, w_a16, w_b16, w_out16, xs, g_post_mix, mod, g_pre_ffn, l)

        act = _ffn_up(h, w_gate, w_up, conv_ffn_w, conv_ffn_b3, l)
        nxt = (g_pre_mix, l + 1, 1, 0) if l + 1 < depth else None
        xs, h = _proj_res(act, w_down16, xs, g_post_ffn, mod, l, 5, 256, "ffn_down", nxt=nxt)

    return xs.reshape(batch, s, d)
```

```python
import functools
import math

import jax
import jax.numpy as jnp
from jax import lax
from jax.experimental import pallas as pl
from jax.experimental.pallas import tpu as pltpu

F32 = jnp.float32
BF16 = jnp.bfloat16

M_HEADS = 4
M_QK_DIM = 256
M_V_DIM = 512
M_QK = M_HEADS * M_QK_DIM
M_V = M_HEADS * M_V_DIM
B_HEADS = 16
B_HEAD_DIM = 64
B_V_DIM = 2 * B_HEAD_DIM
B_QK = B_HEADS * 2 * B_HEAD_DIM
B_V = B_HEADS * B_V_DIM
CHUNK = 64
ROPE_THETA = 500000.0
ROPE_DIM = B_HEAD_DIM // 4
EPS = 1e-6

LANES = 128
SUBLANES = 8
MXU_COLS = 256
V7X_VMEM_REQUEST_CAP = 60000 * 1024
V7X_COMPILER_SCRATCH = 8 * 1024 * 1024

MLSTM_L = 512
ATT_T = 512
ATT_VROWS = B_V_DIM + 16
ATT_PAD = LANES
LOG2E = math.log2(math.e)
NEG = -1e30


def _nbytes(shape, dtype):
    return math.prod(shape) * jnp.dtype(dtype).itemsize


def _vmem_limit(*terms):
    total = sum(_nbytes(s, d) * n for s, d, n in terms) + V7X_COMPILER_SCRATCH
    return int(min(V7X_VMEM_REQUEST_CAP, total))


def _params(sem, limit):
    return pltpu.CompilerParams(dimension_semantics=sem, vmem_limit_bytes=limit)


def _silu(x):
    return x * jax.nn.sigmoid(x)


def _dot(a, b):
    return jnp.dot(a, b, preferred_element_type=F32)


def _dot_nt(a, b):
    return lax.dot_general(a, b, (((1,), (1,)), ((), ())), preferred_element_type=F32)


def _dot_tn(a, b):
    return lax.dot_general(a, b, (((0,), (0,)), ((), ())), preferred_element_type=F32)


def _adaln_kernel(c_ref, w_ref, b_ref, o_ref):
    c = c_ref[...]
    o_ref[...] = jnp.sum(_silu(c) * w_ref[...], axis=0, keepdims=True) + b_ref[...]


def _adaln(c_col, w_ada, b_ada):
    depth, d, n = w_ada.shape
    tn = 1024
    return pl.pallas_call(
        _adaln_kernel,
        out_shape=jax.ShapeDtypeStruct((depth, 1, n), F32),
        grid=(depth, n // tn),
        in_specs=[pl.BlockSpec((d, 1), lambda l, j: (0, 0)),
                  pl.BlockSpec((None, d, tn), lambda l, j: (l, 0, j)),
                  pl.BlockSpec((None, 1, tn), lambda l, j: (l, 0, j))],
        out_specs=pl.BlockSpec((None, 1, tn), lambda l, j: (l, 0, j)),
        compiler_params=_params(("parallel", "parallel"),
                                _vmem_limit(((d, LANES), F32, 2), ((d, tn), F32, 3))),
        name="adaln",
    )(c_col, w_ada, b_ada.reshape(depth, 1, n))


def _modulated_norm(x, g, sc, sh):
    ms = jnp.mean(x * x, axis=-1, keepdims=True)
    return (x * (g * (1.0 + sc))) * lax.rsqrt(ms + EPS) + sh


def _prenorm_kernel(x_ref, g_ref, sc_ref, sh_ref, o_ref):
    o_ref[...] = _modulated_norm(x_ref[...], g_ref[...], sc_ref[...], sh_ref[...]).astype(o_ref.dtype)


def _prenorm(x, gains, mod, l, sc_blk, sh_blk):
    s, d = x.shape
    tm = min(512, s)
    return pl.pallas_call(
        _prenorm_kernel,
        out_shape=jax.ShapeDtypeStruct((s, d), BF16),
        grid=(s // tm,),
        in_specs=[pl.BlockSpec((tm, d), lambda i: (i, 0)),
                  pl.BlockSpec((None, 1, d), lambda i: (l, 0, 0)),
                  pl.BlockSpec((None, 1, d), lambda i: (l, 0, sc_blk)),
                  pl.BlockSpec((None, 1, d), lambda i: (l, 0, sh_blk))],
        out_specs=pl.BlockSpec((tm, d), lambda i: (i, 0)),
        compiler_params=_params(("parallel",), _vmem_limit(((tm, d), F32, 4), ((tm, d), BF16, 2))),
        name="prenorm",
    )(x, gains, mod, mod)


REBASE_ROWS = 512


def _rebase_kernel(cur_ref, nxt_ref, o_ref):
    skip = nxt_ref.shape[0]
    o_ref[...] = jnp.concatenate([cur_ref[skip:, :], nxt_ref[...]], axis=0).astype(o_ref.dtype)


def _rebase_rows(w_t, row0):
    depth, rows, d = w_t.shape
    r = REBASE_ROWS
    base = row0 - SUBLANES
    n = rows - row0
    assert base % r == 0 and n % r == 0
    blk0, per = base // r, r // SUBLANES
    return pl.pallas_call(
        _rebase_kernel,
        out_shape=jax.ShapeDtypeStruct((depth, n, d), BF16),
        grid=(depth, n // r),
        in_specs=[pl.BlockSpec((None, r, d), lambda l, j: (l, blk0 + j, 0)),
                  pl.BlockSpec((None, SUBLANES, d), lambda l, j: (l, (blk0 + j + 1) * per, 0))],
        out_specs=pl.BlockSpec((None, r, d), lambda l, j: (l, j, 0)),
        compiler_params=_params(("parallel", "parallel"), _vmem_limit(
            ((r, d), F32, 3), ((SUBLANES, d), F32, 2), ((r, d), BF16, 3))),
        name="rebase_w_in",
    )(w_t, w_t)


def _first_row_tile():
    return pl.program_id(1) == 0


def _cast_weight(w_ref, wb_ref):
    @pl.when(_first_row_tile())
    def _():
        wb_ref[...] = w_ref[...].astype(BF16)


def _causal_conv(acc, buf_ref, cw_ref, cb_ref, cs, tm):
    kconv = cw_ref.shape[0]
    buf_ref[SUBLANES:2 * SUBLANES, cs] = acc[0:SUBLANES, :]
    out = cb_ref[:, cs] + cw_ref[kconv - 1:kconv, cs] * acc
    for j in range(kconv - 1):
        dist = kconv - 1 - j
        top = buf_ref[SUBLANES - dist:2 * SUBLANES - dist, cs]
        shifted = jnp.concatenate([top, pltpu.roll(acc, dist, 0)[SUBLANES:, :]], axis=0)
        out = out + cw_ref[j:j + 1, cs] * shifted
    buf_ref[0:SUBLANES, cs] = acc[tm - SUBLANES:tm, :]
    return out


def _col_subtiles(n):
    sub = min(MXU_COLS, n)
    return [slice(c * sub, (c + 1) * sub) for c in range(n // sub)]


def _proj_conv_kernel(h_ref, w_ref, cw_ref, cb_ref, o_ref, buf_ref, *, tm):
    @pl.when(_first_row_tile())
    def _():
        buf_ref[0:SUBLANES, :] = jnp.zeros((SUBLANES, buf_ref.shape[1]), F32)

    h = h_ref[...]
    for cs in _col_subtiles(o_ref.shape[1]):
        out = _causal_conv(_dot_nt(h, w_ref[cs, :]), buf_ref, cw_ref, cb_ref, cs, tm)
        o_ref[:, cs] = _silu(out).astype(o_ref.dtype)


def _proj_plain_kernel(h_ref, w_ref, o_ref):
    h = h_ref[...]
    for cs in _col_subtiles(o_ref.shape[1]):
        o_ref[:, cs] = _dot_nt(h, w_ref[cs, :]).astype(o_ref.dtype)


def _proj_mixed_kernel(h_ref, w_ref, o_ref, *, n_plain):
    j = pl.program_id(0)

    @pl.when(j < n_plain)
    def _():
        _proj_plain_kernel(h_ref, w_ref, o_ref)

    @pl.when(j >= n_plain)
    def _():
        h = h_ref[...]
        for cs in _col_subtiles(o_ref.shape[1]):
            o_ref[:, cs] = jax.nn.sigmoid(_dot_nt(h, w_ref[cs, :])).astype(o_ref.dtype)


def _proj_rope_kernel(h_ref, w_ref, ca_ref, cb_ref, cc_ref, o_ref, *, n_qtiles, qscale):
    scale = jnp.where(pl.program_id(0) < n_qtiles, qscale, 1.0).astype(F32)
    ca = ca_ref[...] * scale
    cb = cb_ref[...] * scale
    cc = cc_ref[...] * scale
    h = h_ref[...]
    for cs in _col_subtiles(o_ref.shape[1]):
        acc = _dot_nt(h, w_ref[cs, :])
        for s in range(acc.shape[1] // LANES):
            xs = acc[:, s * LANES:(s + 1) * LANES]
            r = (xs * ca + pltpu.roll(xs, LANES - ROPE_DIM // 2, 1) * cb
                 + pltpu.roll(xs, ROPE_DIM // 2, 1) * cc)
            o_ref[:, cs.start + s * LANES:cs.start + (s + 1) * LANES] = r.astype(o_ref.dtype)


def _proj(kernel, h, w, l, col0, n, out_dtype, *, tn, extra=(), extra_specs=(), scratch=(), extra_vmem=(),
          name):
    s, d = h.shape
    tm = min(1024, s)
    blk0 = col0 // tn
    assert blk0 * tn == col0 and n % tn == 0
    return pl.pallas_call(
        kernel,
        out_shape=jax.ShapeDtypeStruct((s, n), out_dtype),
        grid=(n // tn, s // tm),
        in_specs=[pl.BlockSpec((tm, d), lambda j, i: (i, 0)),
                  pl.BlockSpec((None, tn, d), lambda j, i: (l, blk0 + j, 0))] + list(extra_specs),
        out_specs=pl.BlockSpec((tm, tn), lambda j, i: (i, j)),
        scratch_shapes=list(scratch),
        compiler_params=_params(("parallel", "arbitrary"), _vmem_limit(
            ((tm, d), BF16, 2), ((d, tn), BF16, 2), ((tm, tn), out_dtype, 2),
            ((tm, MXU_COLS), F32, 8), *extra_vmem)),
        name=name,
    )(h, w, *extra)


def _mlstm_kernel(qk_ref, v_ref, gate_ref, bias_ref, so_ref, g_ref, o_ref, st_ref, m_ref):
    L = MLSTM_L

    @pl.when(pl.program_id(0) == 0)
    def _():
        st_ref[...] = jnp.zeros(st_ref.shape, F32)
        m_ref[...] = jnp.zeros(m_ref.shape, F32)

    ipre = gate_ref[...] + bias_ref[...]
    fpre = pltpu.roll(ipre, LANES - M_HEADS, 1)
    logf = jnp.minimum(fpre, 0.0) - jnp.log1p(jnp.exp(-jnp.abs(fpre)))

    row = lax.broadcasted_iota(jnp.int32, (L, L), 0)
    col = lax.broadcasted_iota(jnp.int32, (L, L), 1)
    tri = col <= row
    tri_b = jnp.where(tri, 1.0, 0.0).astype(BF16)
    hi = logf.astype(BF16)
    r1 = logf - hi.astype(F32)
    mid = r1.astype(BF16)
    lo = (r1 - mid.astype(F32)).astype(BF16)
    b = _dot(tri_b, hi) + _dot(tri_b, mid) + _dot(tri_b, lo)
    a = ipre - b
    a_t = a.T
    m_all = m_ref[...]
    lane = lax.broadcasted_iota(jnp.int32, (1, LANES), 1)
    ones_col = jnp.where(lax.broadcasted_iota(jnp.int32, (L, LANES), 1) == 0, 1.0, 0.0).astype(BF16)
    m_next = m_all

    for h in range(M_HEADS):
        a_row = a_t[h:h + 1, :]
        a_col = a[:, h:h + 1]
        b_col = b[:, h:h + 1]
        m_prev = m_all[:, h:h + 1]
        amask = jnp.where(tri, a_row, NEG)
        m_run = jnp.maximum(jnp.max(amask, axis=1, keepdims=True), m_prev)
        dmat = jnp.exp(amask - m_run)
        w_inter = jnp.exp(m_prev - m_run)
        q = qk_ref[:, h * M_QK_DIM:(h + 1) * M_QK_DIM] * jnp.asarray(M_QK_DIM ** -0.5, BF16)
        k = qk_ref[:, M_QK + h * M_QK_DIM:M_QK + (h + 1) * M_QK_DIM]
        v_ext = jnp.concatenate([v_ref[:, h * M_V_DIM:(h + 1) * M_V_DIM], ones_col], axis=1)
        sd = (_dot_nt(q, k) * dmat).astype(BF16)
        st = st_ref[h]
        tot = w_inter * _dot(q, st.astype(BF16)) + _dot(sd, v_ext)
        num = tot[:, :M_V_DIM]
        den = tot[:, M_V_DIM:M_V_DIM + 1]
        hval = num / jnp.maximum(jnp.abs(den), jnp.exp(-(b_col + m_run)))
        m_last = m_run[L - 1:L, :]
        w_s = jnp.exp(a_col - m_last)
        decay = jnp.exp(m_prev - m_last)
        rhs = (w_s * v_ext.astype(F32)).astype(BF16)
        st_ref[h] = decay * st + _dot_tn(k, rhs)
        m_next = jnp.where(lane == h, b_col[L - 1:L, :] + m_last, m_next)
        ms = jnp.mean(hval * hval, axis=-1, keepdims=True)
        sl = slice(h * M_V_DIM, (h + 1) * M_V_DIM)
        hn = hval * lax.rsqrt(ms + EPS) * g_ref[:, sl]
        o_ref[:, sl] = (so_ref[:, sl].astype(F32) * hn).astype(o_ref.dtype)

    m_ref[...] = m_next


def _mlstm(qk, v_so, gates, gate_bias, g_mlstm, l):
    s = qk.shape[0]
    L = MLSTM_L
    st_shape = (M_HEADS, M_QK_DIM, M_V_DIM + LANES)
    return pl.pallas_call(
        _mlstm_kernel,
        out_shape=jax.ShapeDtypeStruct((s, M_V), BF16),
        grid=(s // L,),
        in_specs=[pl.BlockSpec((L, 2 * M_QK), lambda c: (c, 0)),
                  pl.BlockSpec((L, M_V), lambda c: (c, 0)),
                  pl.BlockSpec((L, LANES), lambda c: (c, 0)),
                  pl.BlockSpec((None, 1, LANES), lambda c: (l, 0, 0)),
                  pl.BlockSpec((L, M_V), lambda c: (c, 1)),
                  pl.BlockSpec((None, 1, M_V), lambda c: (l, 0, 0))],
        out_specs=pl.BlockSpec((L, M_V), lambda c: (c, 0)),
        scratch_shapes=[pltpu.VMEM(st_shape, F32), pltpu.VMEM((1, LANES), F32)],
        compiler_params=_params(("arbitrary",), _vmem_limit(
            ((L, 2 * M_QK), BF16, 2), ((L, M_V), BF16, 6), ((L, LANES), F32, 2),
            (st_shape, F32, 2), ((L, M_V + LANES), F32, 8), ((L, L), F32, 8))),
        name="mlstm",
    )(qk, v_so, gates, gate_bias, v_so, g_mlstm)


def _attn_kernel(q_ref, k_ref, v_ref, lam_ref, g_ref, o_ref, vt_ref, acc_ref, m_ref, sa_ref, sb_ref,
                 mxa_ref, mxb_ref, *, lam_init, seq):
    t = ATT_T
    qi = pl.program_id(1)

    @pl.when(qi == 0)
    def _():
        ones_row = jnp.where(lax.broadcasted_iota(jnp.int32, (ATT_VROWS - B_V_DIM, t), 0) == 0, 1.0, 0.0)
        for kb in range(seq // t):
            v_t = v_ref[kb * t:(kb + 1) * t, :].astype(F32).T
            vt_ref[kb] = jnp.concatenate([v_t, ones_row], axis=0).astype(BF16)

    def query_block(qb):
        q = q_ref[pl.ds(pl.multiple_of(qb * t, t), t), :]
        qlane = lax.broadcasted_iota(jnp.int32, q.shape, 1)
        zero = jnp.zeros_like(q)
        return jnp.concatenate([jnp.where(qlane < B_HEAD_DIM, q, zero),
                                jnp.where(qlane >= B_HEAD_DIM, q, zero)], axis=0)

    qcat = query_block(qi)
    shift = CHUNK.bit_length() - 1
    kchunk = lax.broadcasted_iota(jnp.int32, (t, 2 * t), 0) >> shift
    qchunk = (lax.broadcasted_iota(jnp.int32, (t, 2 * t), 1) & (t - 1)) >> shift
    diag_mask = kchunk <= qchunk

    def scores(kb, s_ref, mx_ref, mask=None, qc=qcat):
        kblk = k_ref[pl.ds(pl.multiple_of(kb * t, t), t), :]
        s_t = _dot_nt(kblk, qc)
        if mask is not None:
            s_t = jnp.where(mask, s_t, NEG)
        s_ref[:, :2 * t] = s_t
        mx_ref[...] = jnp.max(s_t, axis=0, keepdims=True)

    def absorb(kb, s_t, mx):
        m_old = m_ref[...]
        m_new = jnp.maximum(m_old, mx)
        p = jnp.exp2(s_t - m_new).astype(BF16)
        acc_ref[...] = jnp.exp2(m_old - m_new) * acc_ref[...] + _dot(vt_ref[kb], p)
        m_ref[...] = m_new

    acc_ref[...] = jnp.zeros(acc_ref.shape, F32)
    m_ref[...] = jnp.full(m_ref.shape, NEG, F32)

    @pl.when(qi == 0)
    def _():
        scores(0, sa_ref, mxa_ref)

    def pair(kb):
        scores(kb + 1, sb_ref, mxb_ref)
        absorb(kb, sa_ref[:, :2 * t], mxa_ref[...])
        scores(kb + 2, sa_ref, mxa_ref)
        absorb(kb + 1, sb_ref[:, :2 * t], mxb_ref[...])

    def body(it, carry):
        for u in range(0, 8, 2):
            pair(8 * it + u)
        return carry

    lax.fori_loop(0, qi >> 3, body, 0)

    @pl.when((qi & 4) == 4)
    def _():
        pair(qi & ~7)
        pair((qi & ~7) + 2)

    def finish():
        lf = lam_ref[...]
        lam = (jnp.exp(jnp.sum(lf[0:1] * lf[1:2], axis=1, keepdims=True))
               - jnp.exp(jnp.sum(lf[2:3] * lf[3:4], axis=1, keepdims=True)) + lam_init)
        acc = acc_ref[...]
        o1 = acc[:B_V_DIM, :t] / acc[B_V_DIM:B_V_DIM + 1, :t]
        o2 = acc[:B_V_DIM, t:] / acc[B_V_DIM:B_V_DIM + 1, t:]
        o_t = o1 - lam * o2
        ms = jnp.mean(o_t * o_t, axis=0, keepdims=True)
        o = (o_t * lax.rsqrt(ms + EPS)).T
        o_ref[...] = (o * (g_ref[...] * (1.0 - lam_init))).astype(o_ref.dtype)
        q_next = jnp.minimum(qi + 1, pl.num_programs(1) - 1)
        scores(0, sa_ref, mxa_ref, qc=query_block(q_next))

    def tail(rem):
        if rem & 2:
            pair(qi - rem)
        if rem & 1:
            scores(qi, sb_ref, mxb_ref, diag_mask)
            absorb(qi - 1, sa_ref[:, :2 * t], mxa_ref[...])
            absorb(qi, sb_ref[:, :2 * t], mxb_ref[...])
        else:
            s_d = jnp.where(diag_mask, sa_ref[:, :2 * t], NEG)
            absorb(qi, s_d, jnp.max(s_d, axis=0, keepdims=True))
        finish()

    for rem in range(4):
        pl.when((qi & 3) == rem)(functools.partial(tail, rem))


def _attention(qkb, v_sg, lambdas, g_diff, l, lam_init):
    s = qkb.shape[0]
    t = ATT_T
    kern = functools.partial(_attn_kernel, lam_init=lam_init, seq=s)
    return pl.pallas_call(
        kern,
        out_shape=jax.ShapeDtypeStruct((s, B_V), BF16),
        grid=(B_HEADS, s // t),
        in_specs=[pl.BlockSpec((s, LANES), lambda h, i: (0, h)),
                  pl.BlockSpec((s, LANES), lambda h, i: (0, B_HEADS + h)),
                  pl.BlockSpec((s, LANES), lambda h, i: (0, h)),
                  pl.BlockSpec((None, 4, B_HEAD_DIM), lambda h, i: (l, 0, 0)),
                  pl.BlockSpec((None, 1, LANES), lambda h, i: (l, 0, h))],
        out_specs=pl.BlockSpec((t, LANES), lambda h, i: (i, h)),
        scratch_shapes=[pltpu.VMEM((s // t, ATT_VROWS, t), BF16),
                        pltpu.VMEM((ATT_VROWS, 2 * t), F32),
                        pltpu.VMEM((1, 2 * t), F32),
                        pltpu.VMEM((t, 2 * t + ATT_PAD), F32),
                        pltpu.VMEM((t, 2 * t + ATT_PAD), F32),
                        pltpu.VMEM((1, 2 * t), F32),
                        pltpu.VMEM((1, 2 * t), F32)],
        compiler_params=_params(("arbitrary", "arbitrary"), _vmem_limit(
            ((s, LANES), BF16, 6), ((s // t, ATT_VROWS, t), BF16, 1), ((t, LANES), BF16, 6),
            ((t, 2 * t), F32, 8), ((ATT_VROWS, 2 * t), F32, 4))),
        name="diff_attn",
    )(qkb, qkb, v_sg, lambdas, g_diff)


def _mix_out_kernel(a_ref, b_ref, ga_ref, gb_ref, wa_ref, wb_ref, wo_ref, x_ref, g_ref, gt_ref, gn_ref, sc_ref,
                    sh_ref, o_ref, hn_ref, m_ref):
    a = a_ref[...]
    b = b_ref[...]
    for cs in _col_subtiles(m_ref.shape[1]):
        ya = _dot(a, wa_ref[:, cs])
        yb = _dot(b, wb_ref[:, cs])
        m_ref[:, cs] = (ga_ref[:, cs].astype(F32) * ya + gb_ref[:, cs].astype(F32) * yb).astype(m_ref.dtype)
    y = _dot(m_ref[...], wo_ref[...])
    _residual_epilogue(y, x_ref, g_ref, gt_ref, (gn_ref, sc_ref, sh_ref), o_ref, hn_ref)


def _mix_out(ya_in, yb_in, v_sg, w_a, w_b, w_out, x, g_post, mod, g_next, l):
    s, d = x.shape
    tm = min(MXU_COLS, s)
    row = lambda c: pl.BlockSpec((tm, d), lambda i: (i, c))
    weight = pl.BlockSpec((None, d, d), lambda i: (l, 0, 0), pipeline_mode=pl.Buffered(1))
    vec = lambda blk: pl.BlockSpec((None, 1, d), lambda i: (l, 0, blk))
    return pl.pallas_call(
        _mix_out_kernel,
        out_shape=[jax.ShapeDtypeStruct((s, d), F32), jax.ShapeDtypeStruct((s, d), BF16)],
        grid=(s // tm,),
        in_specs=[row(0), row(0), row(1), row(2), weight, weight, weight, row(0), vec(0), vec(2), vec(0), vec(4),
                  vec(3)],
        out_specs=[row(0), row(0)],
        scratch_shapes=[pltpu.VMEM((tm, d), BF16)],
        compiler_params=_params(("parallel",), _vmem_limit(
            ((d, d), BF16, 3), ((tm, d), BF16, 11), ((tm, d), F32, 7))),
        name="mix_out",
    )(ya_in, yb_in, v_sg, v_sg, w_a, w_b, w_out, x, g_post, mod, g_next, mod, mod)


def _residual_epilogue(y, x_ref, g_ref, gt_ref, nxt_refs, o_ref, hn_ref):
    ms = jnp.mean(y * y, axis=-1, keepdims=True)
    x_new = x_ref[...] + (y * (gt_ref[...] * g_ref[...])) * lax.rsqrt(ms + EPS)
    o_ref[...] = x_new
    if nxt_refs is not None:
        gn_ref, sc_ref, sh_ref = nxt_refs
        hn_ref[...] = _modulated_norm(x_new, gn_ref[...], sc_ref[...], sh_ref[...]).astype(hn_ref.dtype)


def _proj_res_kernel(a_ref, w_ref, x_ref, g_ref, gt_ref, *rest, with_next):
    if with_next:
        *nxt_refs, o_ref, hn_ref = rest
    else:
        (o_ref,), nxt_refs, hn_ref = rest, None, None
    _residual_epilogue(_dot(a_ref[...], w_ref[...]), x_ref, g_ref, gt_ref, nxt_refs, o_ref, hn_ref)


def _proj_res(a, w, x, gains, mod, l, gt_blk, tm, name, nxt=None):
    s, k = a.shape
    d = w.shape[2]
    tm = min(tm, s)
    rows = lambda width: pl.BlockSpec((tm, width), lambda i: (i, 0))
    vec = lambda arr, lay, blk: (arr, pl.BlockSpec((None, 1, d), lambda i: (lay, 0, blk)))
    operands = [(a, rows(k)),
                (w, pl.BlockSpec((None, k, d), lambda i: (l, 0, 0), pipeline_mode=pl.Buffered(1))),
                (x, rows(d)), vec(gains, l, 0), vec(mod, l, gt_blk)]
    out_shape = [jax.ShapeDtypeStruct((s, d), F32)]
    out_specs = [rows(d)]
    if nxt is not None:
        n_gains, n_l, n_sc, n_sh = nxt
        operands += [vec(n_gains, n_l, 0), vec(mod, n_l, n_sc), vec(mod, n_l, n_sh)]
        out_shape.append(jax.ShapeDtypeStruct((s, d), BF16))
        out_specs.append(rows(d))
    outs = pl.pallas_call(
        functools.partial(_proj_res_kernel, with_next=nxt is not None),
        out_shape=out_shape,
        grid=(s // tm,),
        in_specs=[spec for _, spec in operands],
        out_specs=out_specs,
        compiler_params=_params(("parallel",), _vmem_limit(
            ((tm, k), BF16, 2), ((k, d), BF16, 1), ((tm, d), F32, 10), ((tm, d), BF16, 2))),
        name=name,
    )(*[arr for arr, _ in operands])
    return outs if nxt is not None else (outs[0], None)


def _ffn_up_kernel(h_ref, wg_ref, wu_ref, cw_ref, cb_ref, o_ref, wgb_ref, wub_ref, buf_ref, *, tm):
    _cast_weight(wg_ref, wgb_ref)
    _cast_weight(wu_ref, wub_ref)

    @pl.when(_first_row_tile())
    def _():
        buf_ref[0:SUBLANES, :] = jnp.zeros((SUBLANES, buf_ref.shape[1]), F32)

    h = h_ref[...]
    subtiles = _col_subtiles(o_ref.shape[1])
    gates = [_silu(_causal_conv(_dot(h, wgb_ref[:, cs]), buf_ref, cw_ref, cb_ref, cs, tm)) for cs in subtiles]
    for cs, g in zip(subtiles, gates):
        o_ref[:, cs] = (g * _dot(h, wub_ref[:, cs])).astype(o_ref.dtype)


def _ffn_up(h, w_gate, w_up, conv_w, conv_b, l):
    s, d = h.shape
    n = w_gate.shape[2]
    tm, tn = min(1024, s), 512
    kconv = conv_w.shape[1]
    kern = functools.partial(_ffn_up_kernel, tm=tm)
    return pl.pallas_call(
        kern,
        out_shape=jax.ShapeDtypeStruct((s, n), BF16),
        grid=(n // tn, s // tm),
        in_specs=[pl.BlockSpec((tm, d), lambda j, i: (i, 0)),
                  pl.BlockSpec((None, d, tn), lambda j, i: (l, 0, j)),
                  pl.BlockSpec((None, d, tn), lambda j, i: (l, 0, j)),
                  pl.BlockSpec((None, kconv, tn), lambda j, i: (l, 0, j)),
                  pl.BlockSpec((None, 1, tn), lambda j, i: (l, 0, j))],
        out_specs=pl.BlockSpec((tm, tn), lambda j, i: (i, j)),
        scratch_shapes=[pltpu.VMEM((d, tn), BF16), pltpu.VMEM((d, tn), BF16),
                        pltpu.VMEM((2 * SUBLANES, tn), F32)],
        compiler_params=_params(("parallel", "arbitrary"), _vmem_limit(
            ((tm, d), BF16, 2), ((d, tn), F32, 4), ((d, tn), BF16, 2), ((tm, tn), BF16, 2),
            ((tm, tn), F32, 6))),
        name="ffn_up",
    )(h, w_gate, w_up, conv_w, conv_b)


def _rope_tables(positions):
    half = ROPE_DIM // 2
    inv = jnp.power(ROPE_THETA, -(jnp.arange(half, dtype=F32) * 2.0 / ROPE_DIM))
    ang = positions.astype(F32)[:, None] * inv
    cos, sin = jnp.cos(ang), jnp.sin(ang)
    s = positions.shape[0]
    rest = B_HEAD_DIM - ROPE_DIM
    reps = LANES // B_HEAD_DIM
    ca = jnp.tile(jnp.concatenate([cos, cos, jnp.ones((s, rest), F32)], axis=1), (1, reps))
    cb = jnp.tile(jnp.concatenate([-sin, jnp.zeros((s, B_HEAD_DIM - half), F32)], axis=1), (1, reps))
    cc = jnp.tile(jnp.concatenate([jnp.zeros((s, half), F32), sin, jnp.zeros((s, rest), F32)], axis=1),
                  (1, reps))
    return ca, cb, cc


def kernel(x, c, positions, w_ada, b_ada, g_pre_mix, g_post_mix, g_pre_ffn, g_post_ffn, w_in, conv_qk_w, conv_qk_b, b_igate, b_fgate, g_mlstm, lambdas, g_diff, w_a, w_b, w_out, w_gate, w_up, conv_ffn_w, conv_ffn_b, w_down):
    batch, s, d = x.shape
    depth = w_in.shape[0]
    assert batch == 1 and s % 1024 == 0 and d == M_V == B_V
    xs = x.reshape(s, d)
    tm_proj = min(1024, s)

    mod = _adaln(c.reshape(d, 1), w_ada, b_ada)
    ca, cb, cc = _rope_tables(positions[0])
    rope_specs = [pl.BlockSpec((tm_proj, LANES), lambda j, i: (i, 0))] * 3

    def gains(g):
        return g.reshape(depth, 1, g.shape[1])

    g_pre_mix, g_post_mix, g_pre_ffn, g_post_ffn = map(gains, (g_pre_mix, g_post_mix, g_pre_ffn, g_post_ffn))
    g_mlstm3, g_diff3 = gains(g_mlstm), gains(g_diff)
    conv_qk_b3, conv_ffn_b3 = gains(conv_qk_b), gains(conv_ffn_b)
    o_vm = 2 * M_QK
    o_om = o_vm + M_V
    o_gates = o_om + M_V
    o_tail = o_gates + 2 * M_HEADS
    gate_bias = jnp.concatenate([b_igate, b_fgate, jnp.zeros((depth, LANES - 2 * M_HEADS), F32)],
                                axis=1).reshape(depth, 1, LANES)
    assert math.log2(M_QK_DIM) % 2 == 0
    w_in_t = jnp.swapaxes(w_in, 1, 2)
    w_head = w_in_t[:, :o_gates + LANES, :].astype(BF16)
    w_tail = _rebase_rows(w_in_t, o_tail)
    w_a16, w_b16, w_out16, w_down16 = (w.astype(BF16) for w in (w_a, w_b, w_out, w_down))
    tn, tw = 1024, 2048

    h = _prenorm(xs, g_pre_mix, mod, 0, 1, 0)
    for l in range(depth):
        lam_init = 0.8 - 0.6 * math.exp(-0.3 * l)

        qk = _proj(functools.partial(_proj_conv_kernel, tm=tm_proj), h, w_head, l, 0, 2 * M_QK, BF16, tn=tn,
                   extra=(conv_qk_w, conv_qk_b3),
                   extra_specs=[pl.BlockSpec((None, conv_qk_w.shape[1], tn), lambda j, i: (l, 0, j)),
                                pl.BlockSpec((None, 1, tn), lambda j, i: (l, 0, j))],
                   scratch=[pltpu.VMEM((2 * SUBLANES, tn), F32)],
                   extra_vmem=(((tm_proj, tn), F32, 3),), name="proj_conv")
        v_so = _proj(functools.partial(_proj_mixed_kernel, n_plain=M_V // tw), h, w_head, l, o_vm, 2 * M_V,
                     BF16, tn=tw, name="proj_v_so")
        gates = _proj(_proj_plain_kernel, h, w_head, l, o_gates, LANES, F32, tn=LANES, name="proj_gates")
        qkb = _proj(functools.partial(_proj_rope_kernel, n_qtiles=B_QK // tw, qscale=B_HEAD_DIM ** -0.5 * LOG2E),
                    h, w_tail, l, 0, 2 * B_QK, BF16, tn=tw, extra=(ca, cb, cc), extra_specs=rope_specs,
                    extra_vmem=(((tm_proj, LANES), F32, 9),), name="proj_rope")
        v_sg = _proj(functools.partial(_proj_mixed_kernel, n_plain=B_V // tw), h, w_tail, l, 2 * B_QK,
                     B_V + 2 * d, BF16, tn=tw, name="proj_v_sg")

        y_a_in = _mlstm(qk, v_so, gates, gate_bias, g_mlstm3, l)
        y_b_in = _attention(qkb, v_sg, lambdas, g_diff3, l, lam_init)
        xs, h = _mix_out(y_a_in, y_b_in, v_sg, w_a16, w_b16, w_out16, xs, g_post_mix, mod, g_pre_ffn, l)

        act = _ffn_up(h, w_gate, w_up, conv_ffn_w, conv_ffn_b3, l)
        nxt = (g_pre_mix, l + 1, 1, 0) if l + 1 < depth else None
        xs, h = _proj_res(act, w_down16, xs, g_post_ffn, mod, l, 5, 256, "ffn_down", nxt=nxt)

    return xs.reshape(batch, s, d)
```
